```python
import jax, jax.numpy as jnp
from jax import lax
import numpy as np

D_MODEL = 2048
BATCH = 1
SEQ = 8192
DEPTH = 1
DEC_BATCH = 128
DEC_SEQ = 8
PAST_LEN = 2048
PAGE_SIZE = 128

HEAD_DIM = 128
MIX_WIDTH = D_MODEL
N_SB_HEADS = MIX_WIDTH // 2 // HEAD_DIM
N_NSA_HEADS = MIX_WIDTH // 2 // HEAD_DIM
N_NSA_KV = 2
NSA_GROUP = N_NSA_HEADS // N_NSA_KV
CMP_BLOCK = 32
CMP_STRIDE = 16
SLC_BLOCK = 64
SLC_TOP = 16
WINDOW = 512
Q_BLOCK = 128
FORCE = 1e4
NEG = -1e30
N_EXPERTS = 256
TOP_K = 8
N_EXPERT_GROUPS = 8
TOPK_GROUPS = 4
D_EXPERT = D_MODEL // 4
D_SHARED = D_EXPERT
ROUTED_SCALE = 2.5
MOE_ROWS = 128
N_ADA = 8
EPS = 1e-6
SB_W = N_SB_HEADS * HEAD_DIM
NSA_W = N_NSA_HEADS * HEAD_DIM
KV_W = N_NSA_KV * HEAD_DIM
IN_SIZES = (SB_W, SB_W, SB_W, NSA_W, KV_W, KV_W, KV_W, KV_W, KV_W, KV_W, 3 * N_NSA_HEADS)
D_IN = sum(IN_SIZES)

kernel_name = 'hybrid_stickbreak_nsa_moe_adaln_step'


def alibi_slopes():
    n = N_NSA_HEADS
    s = np.array([2.0 ** (-8.0 * (i + 1) / n) for i in range(n)], dtype=np.float32)
    return jnp.asarray(s).reshape(N_NSA_KV, NSA_GROUP)


def rmsnorm(x, g):
    xf = x.astype(jnp.float32)
    y = xf * lax.rsqrt(jnp.mean(xf * xf, axis=-1, keepdims=True) + EPS)
    return (y * g.astype(jnp.float32)).astype(x.dtype)


def modulate(x, g, shift, scale):
    return rmsnorm(x, g) * (1 + scale[:, None, :]) + shift[:, None, :]


def ada(c, w_ada, b_ada):
    return (jax.nn.silu(c) @ w_ada + b_ada).reshape(c.shape[0], N_ADA, D_MODEL)


def masked_softmax(s, mask):
    s = jnp.where(mask, s.astype(jnp.float32), NEG)
    m = jnp.max(s, axis=-1, keepdims=True)
    e = jnp.where(mask, jnp.exp(s - m), 0.0)
    return e / jnp.maximum(jnp.sum(e, axis=-1, keepdims=True), 1e-30)


def project(h, w_in):
    B, T, _ = h.shape
    parts = jnp.split(h @ w_in, np.cumsum(IN_SIZES)[:-1].tolist(), axis=-1)
    hsb = lambda t: t.reshape(B, T, N_SB_HEADS, HEAD_DIM)
    hkv = lambda t: t.reshape(B, T, N_NSA_KV, HEAD_DIM)
    q_n = parts[3].reshape(B, T, N_NSA_KV, NSA_GROUP, HEAD_DIM)
    g = jax.nn.sigmoid(parts[10].astype(jnp.float32)).astype(h.dtype).reshape(B, T, 3, N_NSA_KV, NSA_GROUP)
    return (hsb(parts[0]), hsb(parts[1]), hsb(parts[2]), q_n,
            hkv(parts[4]), hkv(parts[5]), hkv(parts[6]), hkv(parts[7]), hkv(parts[8]), hkv(parts[9]), g)


def sb_attend(q, k, v, q_pos, k_pos):
    z = jnp.einsum('bqhd,bkhd->bhqk', q, k).astype(jnp.float32) * (HEAD_DIM ** -0.5)
    causal = k_pos[None, :] < q_pos[:, None]
    log_1m = jnp.where(causal, jax.nn.log_sigmoid(-z), 0.0)
    suffix = lax.cumsum(log_1m, axis=3, reverse=True) - log_1m
    a = jnp.where(causal, jnp.exp(jax.nn.log_sigmoid(z) + suffix), 0.0)
    return jnp.einsum('bhqk,bkhd->bqhd', a.astype(v.dtype), v)


def compress(rows, w_pos, w_c):
    B, L, G, d = rows.shape
    ch = rows.reshape(B, L // CMP_STRIDE, CMP_STRIDE, G, d)
    first = jnp.einsum('bcjgd,jg->bcgd', ch, w_pos[:CMP_STRIDE])
    second = jnp.einsum('bcjgd,jg->bcgd', ch, w_pos[CMP_STRIDE:])
    return jnp.einsum('bngd,gde->bnge', first[:, :-1] + second[:, 1:], w_c)


def cmp_ends(nc):
    return CMP_STRIDE * jnp.arange(nc) + CMP_BLOCK - 1


def to_blocks(rows):
    B, L, G, d = rows.shape
    return rows.reshape(B, L // SLC_BLOCK, SLC_BLOCK, G, d).transpose(0, 3, 1, 2, 4)


def slc_importance(p, n_slc):
    R = SLC_BLOCK // CMP_STRIDE
    nc = p.shape[-1]
    pad = [(0, 0)] * (p.ndim - 1) + [(1, R * (n_slc + 1) - 1 - nc)]
    rows = jnp.pad(p, pad).reshape(p.shape[:-1] + (n_slc + 1, R))
    w = jnp.asarray([0.5] + [1.0] * (R - 1), p.dtype)
    return jnp.sum(rows[..., :-1, :] * w, axis=-1) + 0.5 * rows[..., 1:, 0]


def nsa_core(q, q_pos, kc, vc, kc_end, ksb, vsb, kw, vw, kw_pos, g):
    B, Q = q.shape[:2]
    sl = alibi_slopes()
    scale = HEAD_DIM ** -0.5
    dist_c = (q_pos[:, None] - kc_end[None, :]).astype(jnp.float32)
    s = jnp.einsum('bqghd,bngd->bghqn', q, kc).astype(jnp.float32) * scale - sl[None, :, :, None, None] * dist_c
    p_cmp = masked_softmax(s, kc_end[None, :] <= q_pos[:, None])
    o_cmp = jnp.einsum('bghqn,bngd->bqghd', p_cmp.astype(vc.dtype), vc)
    n_slc = ksb.shape[2]
    imp = slc_importance(jnp.sum(p_cmp, axis=2), n_slc)
    blk = jnp.arange(n_slc)[None, :]
    qb = (q_pos // SLC_BLOCK)[:, None]
    forced = (blk == 0) | ((blk <= qb) & (blk >= qb - 1))
    score = jnp.where(blk > qb, -1.0, jnp.where(forced, FORCE, imp))
    n_top = min(SLC_TOP, n_slc)
    _, idx = lax.top_k(score, n_top)
    bi = jnp.arange(B)[:, None, None, None]
    gi = jnp.arange(N_NSA_KV)[None, :, None, None]
    ks_sel = ksb[bi, gi, idx]
    vs_sel = vsb[bi, gi, idx].reshape(B, N_NSA_KV, Q, n_top * SLC_BLOCK, HEAD_DIM)
    k_pos = idx[..., None] * SLC_BLOCK + jnp.arange(SLC_BLOCK)
    dist = (q_pos[:, None, None] - k_pos)[:, :, None]
    s = jnp.einsum('bqghd,bgqnkd->bghqnk', q, ks_sel).astype(jnp.float32) * scale - sl[None, :, :, None, None, None] * dist.astype(jnp.float32)
    p = masked_softmax(s.reshape(B, N_NSA_KV, NSA_GROUP, Q, n_top * SLC_BLOCK), (dist >= 0).reshape(B, N_NSA_KV, 1, Q, n_top * SLC_BLOCK))
    o_slc = jnp.einsum('bghqm,bgqmd->bqghd', p.astype(vs_sel.dtype), vs_sel)
    dist_w = q_pos[:, None] - kw_pos[None, :]
    mask_w = (dist_w >= 0) & (dist_w < WINDOW) & (kw_pos[None, :] >= 0)
    s = jnp.einsum('bqghd,bwgd->bghqw', q, kw).astype(jnp.float32) * scale - sl[None, :, :, None, None] * dist_w.astype(jnp.float32)
    p = masked_softmax(s, mask_w)
    o_win = jnp.einsum('bghqw,bwgd->bqghd', p.astype(vw.dtype), vw)
    return g[:, :, 0, ..., None] * o_cmp + g[:, :, 1, ..., None] * o_slc + g[:, :, 2, ..., None] * o_win


def unblock(o):
    o = jnp.moveaxis(o, 0, 1)
    return o.reshape((o.shape[0], o.shape[1] * o.shape[2]) + o.shape[3:])


def prompt_mixers(h, w_in, cmp_pos_k, cmp_pos_v, cmp_wk, cmp_wv):
    B, T, _ = h.shape
    q_sb, k_sb, v_sb, q_n, ck, cv, sk, sv, wk, wv, g = project(h, w_in)
    n_qb = T // Q_BLOCK
    pos_all = jnp.arange(T)

    def sb_blk(b):
        start = b * Q_BLOCK
        qb = lax.dynamic_slice_in_dim(q_sb, start, Q_BLOCK, axis=1)
        return sb_attend(qb, k_sb, v_sb, start + jnp.arange(Q_BLOCK), pos_all)

    o_sb = unblock(lax.map(sb_blk, jnp.arange(n_qb)))
    kc = compress(ck, cmp_pos_k, cmp_wk)
    vc = compress(cv, cmp_pos_v, cmp_wv)
    kc_end = cmp_ends(kc.shape[1])
    ksb, vsb = to_blocks(sk), to_blocks(sv)
    kw_pad = jnp.pad(wk, ((0, 0), (WINDOW, 0), (0, 0), (0, 0)))
    vw_pad = jnp.pad(wv, ((0, 0), (WINDOW, 0), (0, 0), (0, 0)))

    def nsa_blk(b):
        start = b * Q_BLOCK
        sl = lambda a, n: lax.dynamic_slice_in_dim(a, start, n, axis=1)
        return nsa_core(sl(q_n, Q_BLOCK), start + jnp.arange(Q_BLOCK), kc, vc, kc_end, ksb, vsb,
                        sl(kw_pad, WINDOW + Q_BLOCK), sl(vw_pad, WINDOW + Q_BLOCK),
                        start - WINDOW + jnp.arange(WINDOW + Q_BLOCK), sl(g, Q_BLOCK))

    o_n = unblock(lax.map(nsa_blk, jnp.arange(n_qb)))
    keep = min(WINDOW, T)
    sb_rows = jnp.stack([k_sb, v_sb], axis=2)
    nsa_rows = jnp.stack([ck, cv, sk, sv], axis=2)
    win = jnp.stack([wk, wv], axis=2)[:, T - keep:]
    return o_sb, o_n, sb_rows, nsa_rows, win


def sample_mixers(h, cache_sb, cache_nsa, state_win, page_table, w_in, cmp_pos_k, cmp_pos_v, cmp_wk, cmp_wv):
    Bs, S, _ = h.shape
    q_sb, k_sb, v_sb, q_n, ck, cv, sk, sv, wk, wv, g = project(h, w_in)
    past = page_table.shape[1] * cache_sb.shape[1]
    w_buf = state_win.shape[1]
    nsa_new = jnp.stack([ck, cv, sk, sv], axis=2)
    win_new = jnp.stack([wk, wv], axis=2)
    q_pos = past + jnp.arange(S)
    L = past + S
    Lp = -(-L // SLC_BLOCK) * SLC_BLOCK
    k_pos_all = jnp.arange(L)
    kw_pos = jnp.arange(past - w_buf, past + S)

    def seq_fn(args):
        qs, ks, vs, qn, nn, wb, wn, gg, pages = args
        past_sb = cache_sb[pages].reshape(past, 2, N_SB_HEADS, HEAD_DIM)
        k = jnp.concatenate([past_sb[:, 0], ks], axis=0)[None]
        v = jnp.concatenate([past_sb[:, 1], vs], axis=0)[None]
        o_sb = sb_attend(qs[None], k, v, q_pos, k_pos_all)[0]
        rows = jnp.concatenate([cache_nsa[pages].reshape(past, 4, N_NSA_KV, HEAD_DIM), nn], axis=0)
        rows = jnp.pad(rows, ((0, Lp - L), (0, 0), (0, 0), (0, 0)))[None]
        kc = compress(rows[:, :, 0], cmp_pos_k, cmp_wk)
        vc = compress(rows[:, :, 1], cmp_pos_v, cmp_wv)
        kw = jnp.concatenate([wb[:, 0], wn[:, 0]], axis=0)[None]
        vw = jnp.concatenate([wb[:, 1], wn[:, 1]], axis=0)[None]
        o_n = nsa_core(qn[None], q_pos, kc, vc, cmp_ends(kc.shape[1]), to_blocks(rows[:, :, 2]),
                       to_blocks(rows[:, :, 3]), kw, vw, kw_pos, gg[None])[0]
        return o_sb, o_n

    o_sb, o_n = lax.map(seq_fn, (q_sb, k_sb, v_sb, q_n, nsa_new, state_win, win_new, g, page_table))
    sb_rows = jnp.stack([k_sb, v_sb], axis=2)
    win = jnp.concatenate([state_win, win_new], axis=1)[:, S:]
    return o_sb, o_n, sb_rows, nsa_new, win


def mix_out(o_sb, o_n, out_norm_g, w_out):
    B, T = o_sb.shape[:2]
    o = jnp.concatenate([o_sb, o_n.reshape(B, T, N_NSA_HEADS, HEAD_DIM)], axis=2)
    return rmsnorm(o, out_norm_g).reshape(B, T, MIX_WIDTH) @ w_out


def swiglu(x, wg, wu, wd):
    return (jax.nn.silu(x @ wg) * (x @ wu)) @ wd


def grouped_experts(h, eidx, w, wg, wu, wd):
    T, D = h.shape
    A = T * TOP_K
    flat_e = eidx.reshape(-1)
    order = jnp.argsort(flat_e)
    sorted_e = flat_e[order]
    counts = jnp.bincount(flat_e, length=N_EXPERTS)
    padded = (counts + MOE_ROWS - 1) // MOE_ROWS * MOE_ROWS
    pad_end = jnp.cumsum(padded)
    pad_start = pad_end - padded
    start = jnp.cumsum(counts) - counts
    dest = pad_start[sorted_e] + jnp.arange(A) - start[sorted_e]
    n_blk = -(-A // MOE_ROWS) + N_EXPERTS
    P = n_blk * MOE_ROWS
    slot_tok = jnp.full((P,), T, jnp.int32).at[dest].set((order // TOP_K).astype(jnp.int32))
    slot_w = jnp.zeros((P,), h.dtype).at[dest].set(w.reshape(-1)[order].astype(h.dtype))
    blk_e = jnp.minimum(jnp.searchsorted(pad_end, jnp.arange(n_blk) * MOE_ROWS, side='right'), N_EXPERTS - 1)
    h_pad = jnp.concatenate([h, jnp.zeros((1, D), h.dtype)], axis=0)

    def run(args):
        tok, wt, e = args
        return swiglu(h_pad[tok], wg[e], wu[e], wd[e]) * wt[:, None]

    y = lax.map(run, (slot_tok.reshape(n_blk, MOE_ROWS), slot_w.reshape(n_blk, MOE_ROWS), blk_e))
    return jax.ops.segment_sum(y.reshape(P, D), slot_tok, num_segments=T + 1)[:T]


def moe(h, w_router, router_bias, w_gate_e, w_up_e, w_down_e, w_gate_s, w_up_s, w_down_s):
    T = h.shape[0]
    s = jax.nn.sigmoid((h @ w_router).astype(jnp.float32))
    biased = s + router_bias.astype(jnp.float32)
    grp = biased.reshape(T, N_EXPERT_GROUPS, N_EXPERTS // N_EXPERT_GROUPS)
    grp_score = jnp.sum(lax.top_k(grp, 2)[0], axis=-1)
    _, gidx = lax.top_k(grp_score, TOPK_GROUPS)
    gmask = jnp.sum(jax.nn.one_hot(gidx, N_EXPERT_GROUPS), axis=1) > 0
    emask = jnp.repeat(gmask, N_EXPERTS // N_EXPERT_GROUPS, axis=1)
    _, eidx = lax.top_k(jnp.where(emask, biased, -jnp.inf), TOP_K)
    w = jnp.take_along_axis(s, eidx, axis=1)
    w = w / jnp.sum(w, axis=-1, keepdims=True) * ROUTED_SCALE
    return grouped_experts(h, eidx, w, w_gate_e, w_up_e, w_down_e) + swiglu(h, w_gate_s, w_up_s, w_down_s)


def setup_inputs(seed: int = 0) -> dict:
    key = jax.random.key(seed)
    ks = jax.random.split(key, 32)
    nrm = lambda k, shape, sc: jax.random.normal(k, shape, jnp.float32) * sc
    n_pages = PAST_LEN // PAGE_SIZE
    n_phys = (5 * DEC_BATCH * n_pages) // 4
    w_buf = min(WINDOW, PAST_LEN)
    perm = jax.random.permutation(ks[5], n_phys)[:DEC_BATCH * n_pages]
    return {
        'x_prompt': nrm(ks[0], (BATCH, SEQ, D_MODEL), 1.0),
        'x_sample': nrm(ks[1], (DEC_BATCH, DEC_SEQ, D_MODEL), 1.0),
        'cache_sb': nrm(ks[2], (n_phys, PAGE_SIZE, 2, N_SB_HEADS, HEAD_DIM), 1.0),
        'cache_nsa': nrm(ks[3], (n_phys, PAGE_SIZE, 4, N_NSA_KV, HEAD_DIM), 1.0),
        'state_win': nrm(ks[4], (DEC_BATCH, w_buf, 2, N_NSA_KV, HEAD_DIM), 1.0),
        'page_table': perm.reshape(DEC_BATCH, n_pages).astype(jnp.int32),
        'c_prompt': nrm(ks[6], (BATCH, D_MODEL), 1.0),
        'c_sample': nrm(ks[7], (DEC_BATCH, D_MODEL), 1.0),
        'w_ada': nrm(ks[8], (D_MODEL, N_ADA * D_MODEL), 0.5 * D_MODEL ** -0.5),
        'b_ada': nrm(ks[9], (N_ADA * D_MODEL,), 0.01),
        'norm1_g': 1.0 + nrm(ks[10], (D_MODEL,), 0.01),
        'w_in': nrm(ks[11], (D_MODEL, D_IN), D_MODEL ** -0.5),
        'cmp_pos_k': (1.0 + nrm(ks[12], (CMP_BLOCK, N_NSA_KV), 0.1)) * CMP_BLOCK ** -0.5,
        'cmp_pos_v': (1.0 + nrm(ks[13], (CMP_BLOCK, N_NSA_KV), 0.1)) * CMP_BLOCK ** -0.5,
        'cmp_wk': nrm(ks[14], (N_NSA_KV, HEAD_DIM, HEAD_DIM), HEAD_DIM ** -0.5),
        'cmp_wv': nrm(ks[15], (N_NSA_KV, HEAD_DIM, HEAD_DIM), HEAD_DIM ** -0.5),
        'out_norm_g': 1.0 + nrm(ks[16], (N_SB_HEADS + N_NSA_HEADS, HEAD_DIM), 0.01),
        'w_out': nrm(ks[17], (MIX_WIDTH, D_MODEL), MIX_WIDTH ** -0.5),
        'norm2_g': 1.0 + nrm(ks[18], (D_MODEL,), 0.01),
        'w_router': nrm(ks[19], (D_MODEL, N_EXPERTS), D_MODEL ** -0.5),
        'router_bias': nrm(ks[20], (N_EXPERTS,), 0.01),
        'w_gate_e': nrm(ks[21], (N_EXPERTS, D_MODEL, D_EXPERT), D_MODEL ** -0.5),
        'w_up_e': nrm(ks[22], (N_EXPERTS, D_MODEL, D_EXPERT), D_MODEL ** -0.5),
        'w_down_e': nrm(ks[23], (N_EXPERTS, D_EXPERT, D_MODEL), D_EXPERT ** -0.5),
        'w_gate_s': nrm(ks[24], (D_MODEL, D_SHARED), D_MODEL ** -0.5),
        'w_up_s': nrm(ks[25], (D_MODEL, D_SHARED), D_MODEL ** -0.5),
        'w_down_s': nrm(ks[26], (D_SHARED, D_MODEL), D_SHARED ** -0.5),
        'normf_g': 1.0 + nrm(ks[27], (D_MODEL,), 0.01),
    }


def reference(x_prompt, x_sample, cache_sb, cache_nsa, state_win, page_table, c_prompt, c_sample,
              w_ada, b_ada, norm1_g, w_in, cmp_pos_k, cmp_pos_v, cmp_wk, cmp_wv, out_norm_g, w_out,
              norm2_g, w_router, router_bias, w_gate_e, w_up_e, w_down_e, w_gate_s, w_up_s, w_down_s, normf_g):
    mod_p = ada(c_prompt, w_ada, b_ada)
    mod_s = ada(c_sample, w_ada, b_ada)
    xp, xs = x_prompt, x_sample
    for _layer in range(DEPTH):
        hp = modulate(xp, norm1_g, mod_p[:, 0], mod_p[:, 1])
        hs = modulate(xs, norm1_g, mod_s[:, 0], mod_s[:, 1])
        osb_p, on_p, sb_rows_p, nsa_rows_p, win_p = prompt_mixers(hp, w_in, cmp_pos_k, cmp_pos_v, cmp_wk, cmp_wv)
        osb_s, on_s, sb_rows_s, nsa_rows_s, win_s = sample_mixers(hs, cache_sb, cache_nsa, state_win, page_table,
                                                                  w_in, cmp_pos_k, cmp_pos_v, cmp_wk, cmp_wv)
        xp = xp + mod_p[:, 2][:, None] * mix_out(osb_p, on_p, out_norm_g, w_out)
        xs = xs + mod_s[:, 2][:, None] * mix_out(osb_s, on_s, out_norm_g, w_out)
        h2p = modulate(xp, norm2_g, mod_p[:, 3], mod_p[:, 4])
        h2s = modulate(xs, norm2_g, mod_s[:, 3], mod_s[:, 4])
        n_p = xp.shape[0] * xp.shape[1]
        f = moe(jnp.concatenate([h2p.reshape(n_p, D_MODEL), h2s.reshape(-1, D_MODEL)], axis=0),
                w_router, router_bias, w_gate_e, w_up_e, w_down_e, w_gate_s, w_up_s, w_down_s)
        xp = xp + mod_p[:, 5][:, None] * f[:n_p].reshape(xp.shape)
        xs = xs + mod_s[:, 5][:, None] * f[n_p:].reshape(xs.shape)
    y_prompt = modulate(xp, normf_g, mod_p[:, 6], mod_p[:, 7])
    y_sample = modulate(xs, normf_g, mod_s[:, 6], mod_s[:, 7])
    return (y_prompt, y_sample, sb_rows_p, sb_rows_s, nsa_rows_p, nsa_rows_s, win_p, win_s)
```

```python
import functools

import numpy as np
import jax
import jax.numpy as jnp
from jax import lax
from jax.experimental import pallas as pl
from jax.experimental.pallas import tpu as pltpu

F32 = jnp.float32
BF16 = jnp.bfloat16

D_MODEL = 2048
HEAD_DIM = 128
N_SB_HEADS = 8
N_NSA_HEADS = 8
N_NSA_KV = 2
NSA_GROUP = N_NSA_HEADS // N_NSA_KV
CMP_BLOCK = 32
CMP_STRIDE = 16
SLC_BLOCK = 64
SLC_TOP = 16
WINDOW = 512
FORCE = 1e4
NEG = -1e30
N_EXPERTS = 256
TOP_K = 8
N_EXPERT_GROUPS = 8
TOPK_GROUPS = 4
D_EXPERT = D_MODEL // 4
ROUTED_SCALE = 2.5
MOE_ROWS = 128
N_ADA = 8
EPS = 1e-6
ATT_SCALE = HEAD_DIM ** -0.5

SB_W = N_SB_HEADS * HEAD_DIM
NSA_W = N_NSA_HEADS * HEAD_DIM
KV_W = N_NSA_KV * HEAD_DIM
COL_QSB = 0
COL_KVSB = SB_W
COL_QN = 3 * SB_W
COL_NSA = COL_QN + NSA_W
COL_WIN = COL_NSA + 4 * KV_W
COL_GATE = COL_WIN + 2 * KV_W
N_GATE = 3 * N_NSA_HEADS
LANES = 128
V7X_VMEM_LIMIT = 56 * 1024 * 1024

NT = (((1,), (1,)), ((), ()))


def _cparams(sem):
    return pltpu.CompilerParams(dimension_semantics=sem, vmem_limit_bytes=V7X_VMEM_LIMIT)


def _sigmoid(x):
    return 1.0 / (1.0 + jnp.exp(-x))


def _split_bf16(x):
    hi = x.astype(BF16)
    lo = (x - hi.astype(F32)).astype(BF16)
    return hi, lo


def _dot(a, b):
    return jnp.dot(a, b, preferred_element_type=F32)


def _dot_nt(a, b):
    return lax.dot_general(a, b, NT, preferred_element_type=F32)


def _rms(x):
    return x * lax.rsqrt(jnp.mean(x * x, axis=-1, keepdims=True) + EPS)


def _masked_softmax(s, mask):
    s = jnp.where(mask, s, NEG)
    m = jnp.max(s, axis=-1, keepdims=True)
    e = jnp.where(mask, jnp.exp(s - m), 0.0)
    return e / jnp.maximum(jnp.sum(e, axis=-1, keepdims=True), 1e-30)


def _head_slope(head):
    out = jnp.full(head.shape, 2.0 ** -N_NSA_HEADS, F32)
    for i in range(N_NSA_HEADS - 1):
        out = jnp.where(head == i, 2.0 ** -(i + 1), out)
    return out


def _top_blocks(score, n_top):
    blk = lax.broadcasted_iota(jnp.int32, score.shape, 1).astype(F32)
    sel = jnp.zeros(score.shape, F32)
    for _ in range(n_top):
        m = jnp.max(score, axis=-1, keepdims=True)
        idx = jnp.min(jnp.where(score == m, blk, float(score.shape[1])), axis=-1, keepdims=True)
        pick = blk == idx
        sel = jnp.where(pick, 1.0, sel)
        score = jnp.where(pick, -2.0, score)
    return sel


def _slc_scores(imp, qpos):
    blk = lax.broadcasted_iota(jnp.int32, imp.shape, 1)
    qb = qpos // SLC_BLOCK
    forced = (blk == 0) | ((blk <= qb) & (blk >= qb - 1))
    return jnp.where(blk > qb, -1.0, jnp.where(forced, FORCE, imp))


def _ada_kernel(c_ref, w_ref, b_ref, o_ref):
    c = c_ref[...]
    a = (c * _sigmoid(c)).astype(BF16)
    o_ref[...] = _dot(a, w_ref[...].astype(BF16)) + b_ref[...]


def _ada(c, w_ada, b_ada, tn=1024):
    m, d = c.shape
    n = w_ada.shape[1]
    return pl.pallas_call(
        _ada_kernel,
        grid=(n // tn,),
        in_specs=[pl.BlockSpec((m, d), lambda j: (0, 0)),
                  pl.BlockSpec((d, tn), lambda j: (0, j)),
                  pl.BlockSpec((1, tn), lambda j: (0, j))],
        out_specs=pl.BlockSpec((m, tn), lambda j: (0, j)),
        out_shape=jax.ShapeDtypeStruct((m, n), F32),
        compiler_params=_cparams(("arbitrary",)),
        name="ada",
    )(c, w_ada, b_ada.reshape(1, n))


def _proj_kernel(x_ref, g_ref, shift_ref, scale_ref, w_ref,
                 qsb_ref, sbrows_ref, kvsb_ref, qn_ref, nsarows_ref, nsab_ref,
                 winrows_ref, winb_ref, gates_ref):
    h = (_rms(x_ref[...]) * g_ref[...] * (1.0 + scale_ref[...]) + shift_ref[...]).astype(BF16)
    ch = 512

    def mm(c0, n):
        return _dot(h, w_ref[:, c0:c0 + n])

    for c in range(0, SB_W, ch):
        qsb_ref[:, c:c + ch] = mm(COL_QSB + c, ch).astype(BF16)
    for c in range(0, 2 * SB_W, ch):
        r = mm(COL_KVSB + c, ch)
        sbrows_ref[:, c:c + ch] = r
        kvsb_ref[:, c:c + ch] = r.astype(BF16)
    for c in range(0, NSA_W, ch):
        qn_ref[:, c:c + ch] = mm(COL_QN + c, ch).astype(BF16)
    for c in range(0, 4 * KV_W, ch):
        r = mm(COL_NSA + c, ch)
        nsarows_ref[:, c:c + ch] = r
        nsab_ref[:, c:c + ch] = r.astype(BF16)
    r = mm(COL_WIN, 2 * KV_W)
    winrows_ref[...] = r
    winb_ref[...] = r.astype(BF16)
    gates_ref[...] = _sigmoid(mm(COL_GATE, LANES))


def _project(x, g, shift, scale, w_cat, tm=256):
    m, d = x.shape
    row = lambda i: (i, 0)
    const = lambda i: (0, 0)
    mod_spec = lambda a: pl.BlockSpec((tm, d), row) if a.shape[0] == m else pl.BlockSpec((1, d), const)
    widths = [(SB_W, BF16), (2 * SB_W, F32), (2 * SB_W, BF16), (NSA_W, BF16), (4 * KV_W, F32),
              (4 * KV_W, BF16), (2 * KV_W, F32), (2 * KV_W, BF16), (LANES, F32)]
    return pl.pallas_call(
        _proj_kernel,
        grid=(m // tm,),
        in_specs=[pl.BlockSpec((tm, d), row),
                  pl.BlockSpec((1, d), const),
                  mod_spec(shift), mod_spec(scale),
                  pl.BlockSpec(w_cat.shape, const, pipeline_mode=pl.Buffered(1))],
        out_specs=[pl.BlockSpec((tm, w), row) for w, _ in widths],
        out_shape=[jax.ShapeDtypeStruct((m, w), dt) for w, dt in widths],
        compiler_params=_cparams(("arbitrary",)),
        name="project",
    )(x, g, shift, scale, w_cat)


def _sb_logits(z):
    sp = jnp.log1p(jnp.exp(-jnp.abs(z)))
    return jnp.minimum(z, 0.0) - sp, jnp.minimum(-z, 0.0) - sp


def _sb_prompt_kernel(q_ref, k_ref, v_ref, o_ref, *, tq):
    i = pl.program_id(1)
    q = q_ref[...]
    row = lax.broadcasted_iota(jnp.int32, (tq, tq), 0)
    col = lax.broadcasted_iota(jnp.int32, (tq, tq), 1)
    later = jnp.where(row > col, 1.0, 0.0).astype(BF16)
    causal = col < row

    def block(j, carry, acc, diagonal):
        off = pl.multiple_of(j * tq, tq)
        k = k_ref[pl.ds(off, tq), :]
        v = v_ref[pl.ds(off, tq), :]
        z = _dot_nt(q, k) * ATT_SCALE
        ls, l1m = _sb_logits(z)
        if diagonal:
            l1m = jnp.where(causal, l1m, 0.0)
        hi, lo = _split_bf16(l1m)
        suffix = _dot(hi, later) + _dot(lo, later) + carry
        a = jnp.exp(ls + suffix)
        if diagonal:
            a = jnp.where(causal, a, 0.0)
        acc = acc + _dot(a.astype(BF16), v)
        return suffix[:, :1] + l1m[:, :1], acc

    carry, acc = block(i, jnp.zeros((tq, 1), F32), jnp.zeros((tq, HEAD_DIM), F32), True)
    carry, acc = lax.fori_loop(0, i, lambda t, c: block(i - 1 - t, c[0], c[1], False), (carry, acc))
    o_ref[...] = acc


def _sb_prompt(qsb, kvsb, tq=256):
    t = qsb.shape[0]
    return pl.pallas_call(
        functools.partial(_sb_prompt_kernel, tq=tq),
        grid=(N_SB_HEADS, t // tq),
        in_specs=[pl.BlockSpec((tq, HEAD_DIM), lambda h, i: (i, h)),
                  pl.BlockSpec((t, HEAD_DIM), lambda h, i: (0, h)),
                  pl.BlockSpec((t, HEAD_DIM), lambda h, i: (0, N_SB_HEADS + h))],
        out_specs=pl.BlockSpec((tq, HEAD_DIM), lambda h, i: (i, h)),
        out_shape=jax.ShapeDtypeStruct((t, SB_W), F32),
        compiler_params=_cparams(("arbitrary", "arbitrary")),
        name="sb_prompt",
    )(qsb, kvsb, kvsb)


def _sb_sample_kernel(pt_ref, q_ref, new_ref, cache_ref, o_ref, carry_ref, acc_ref, *, n_new):
    p = pl.program_id(1)
    rows = N_SB_HEADS * n_new
    kk = lax.broadcasted_iota(jnp.int32, (LANES, 2 * LANES), 0)
    cc = lax.broadcasted_iota(jnp.int32, (LANES, 2 * LANES), 1)
    later_and_all = jnp.where((cc >= LANES) | (kk > cc), 1.0, 0.0).astype(BF16)

    def process(k_of, v_of, diagonal):
        z = jnp.concatenate(
            [_dot_nt(q_ref[0, :, h * HEAD_DIM:(h + 1) * HEAD_DIM], k_of(h)) for h in range(N_SB_HEADS)],
            axis=0) * ATT_SCALE
        ls, l1m = _sb_logits(z)
        if diagonal:
            key = lax.broadcasted_iota(jnp.int32, (rows, LANES), 1)
            qry = lax.broadcasted_iota(jnp.int32, (rows, LANES), 0) % n_new
            causal = key < qry
            l1m = jnp.where(causal, l1m, 0.0)
        hi, lo = _split_bf16(l1m)
        st = _dot(hi, later_and_all) + _dot(lo, later_and_all)
        a = jnp.exp(ls + st[:, :LANES] + carry_ref[...])
        if diagonal:
            a = jnp.where(causal, a, 0.0)
        a = a.astype(BF16)
        for h in range(N_SB_HEADS):
            sl = slice(h * n_new, (h + 1) * n_new)
            acc_ref[sl, :] += _dot(a[sl], v_of(h))
        carry_ref[...] += st[:, LANES:]

    @pl.when(p == 0)
    def _():
        carry_ref[...] = jnp.zeros_like(carry_ref)
        acc_ref[...] = jnp.zeros_like(acc_ref)
        pad = jnp.zeros((LANES - n_new, HEAD_DIM), F32)

        def new_tile(c0):
            return lambda h: jnp.concatenate(
                [new_ref[0, :, c0 + h * HEAD_DIM:c0 + (h + 1) * HEAD_DIM], pad], axis=0).astype(BF16)

        process(new_tile(0), new_tile(SB_W), True)

    def page_tile(c0):
        return lambda h: cache_ref[0, :, c0 + h * HEAD_DIM:c0 + (h + 1) * HEAD_DIM].astype(BF16)

    process(page_tile(0), page_tile(SB_W), False)

    @pl.when(p == pl.num_programs(1) - 1)
    def _():
        for h in range(N_SB_HEADS):
            o_ref[0, :, h * HEAD_DIM:(h + 1) * HEAD_DIM] = acc_ref[h * n_new:(h + 1) * n_new, :]


def _sb_sample(page_table, qsb, sbrows, cache):
    bs, n_new, _ = qsb.shape
    n_pages = page_table.shape[1]
    page = cache.shape[1]
    assert page == LANES
    rows = N_SB_HEADS * n_new
    grid_spec = pltpu.PrefetchScalarGridSpec(
        num_scalar_prefetch=1,
        grid=(bs, n_pages),
        in_specs=[pl.BlockSpec((1, n_new, SB_W), lambda b, p, pt: (b, 0, 0)),
                  pl.BlockSpec((1, n_new, 2 * SB_W), lambda b, p, pt: (b, 0, 0)),
                  pl.BlockSpec((1, page, 2 * SB_W), lambda b, p, pt: (pt[b * n_pages + n_pages - 1 - p], 0, 0))],
        out_specs=pl.BlockSpec((1, n_new, SB_W), lambda b, p, pt: (b, 0, 0)),
        scratch_shapes=[pltpu.VMEM((rows, LANES), F32), pltpu.VMEM((rows, HEAD_DIM), F32)],
    )
    return pl.pallas_call(
        functools.partial(_sb_sample_kernel, n_new=n_new),
        grid_spec=grid_spec,
        out_shape=jax.ShapeDtypeStruct((bs, n_new, SB_W), F32),
        compiler_params=_cparams(("arbitrary", "arbitrary")),
        name="sb_sample",
    )(page_table.reshape(-1), qsb, sbrows, cache)


def _pool_chunks(x, w1, w2):
    x3 = x.reshape(x.shape[0] // CMP_STRIDE, CMP_STRIDE, x.shape[1])
    return jnp.sum(x3 * w1[None], axis=1), jnp.sum(x3 * w2[None], axis=1)


def _compress_kernel(rows_ref, wpos_ref, wc_ref, o_ref, first_ref, second_ref, *, step_rows):
    n_rows = rows_ref.shape[0]
    n_chunks = n_rows // CMP_STRIDE
    w1 = wpos_ref[0, :CMP_STRIDE, :]
    w2 = wpos_ref[0, CMP_STRIDE:, :]
    cps = step_rows // CMP_STRIDE

    def body(s, carry):
        r0 = pl.multiple_of(s * step_rows, step_rows)
        c0 = pl.multiple_of(s * cps, cps)
        f, sec = _pool_chunks(rows_ref[pl.ds(r0, step_rows), :], w1, w2)
        first_ref[pl.ds(c0, cps), :] = f
        second_ref[pl.ds(c0, cps), :] = sec
        return carry

    lax.fori_loop(0, n_rows // step_rows, body, 0)
    pre = first_ref[...] + pltpu.roll(second_ref[...], n_chunks - 1, axis=0)
    for g in range(N_NSA_KV):
        sl = slice(g * HEAD_DIM, (g + 1) * HEAD_DIM)
        o_ref[0, :, sl] = _dot(pre[:, sl].astype(BF16), wc_ref[0, g]).astype(BF16)


def _compress_prompt(nsarows, wpos, wc, step_rows=1024):
    t = nsarows.shape[0]
    n_chunks = t // CMP_STRIDE
    return pl.pallas_call(
        functools.partial(_compress_kernel, step_rows=step_rows),
        grid=(2,),
        in_specs=[pl.BlockSpec((t, KV_W), lambda c: (0, c)),
                  pl.BlockSpec((1, CMP_BLOCK, KV_W), lambda c: (c, 0, 0)),
                  pl.BlockSpec((1, N_NSA_KV, HEAD_DIM, HEAD_DIM), lambda c: (c, 0, 0, 0))],
        out_specs=pl.BlockSpec((1, n_chunks, KV_W), lambda c: (c, 0, 0)),
        out_shape=jax.ShapeDtypeStruct((2, n_chunks, KV_W), BF16),
        scratch_shapes=[pltpu.VMEM((n_chunks, KV_W), F32), pltpu.VMEM((n_chunks, KV_W), F32)],
        compiler_params=_cparams(("arbitrary",)),
        name="compress_prompt",
    )(nsarows, wpos, wc)


def _importance_matrix(n_cmp, n_slc_lanes):
    m = np.arange(n_cmp)[:, None]
    j = np.arange(n_slc_lanes)[None, :]
    d = m - (SLC_BLOCK // CMP_STRIDE) * j
    w = np.where((d >= 0) & (d <= 2), 1.0, np.where((d == -1) | (d == 3), 0.5, 0.0))
    return jnp.asarray(w, BF16)


def _block_expansion(n_keys):
    b = np.arange(LANES)[:, None]
    k = np.arange(n_keys)[None, :]
    return jnp.asarray((b == k // SLC_BLOCK).astype(np.float32), BF16)


def _nsa_prompt_kernel(q_ref, gate_ref, kc_ref, vc_ref, sk_ref, sv_ref, wk_ref, wv_ref,
                       e_ref, mimp_ref, o_ref, *, tq, tk):
    g = pl.program_id(0)
    i = pl.program_id(1)
    start = i * tq
    rows = NSA_GROUP * tq
    qg = jnp.concatenate([q_ref[:, h * HEAD_DIM:(h + 1) * HEAD_DIM] for h in range(NSA_GROUP)], axis=0)
    r = lax.broadcasted_iota(jnp.int32, (rows, 1), 0)
    slope = _head_slope(g * NSA_GROUP + r // tq)
    qpos = start + r % tq

    n_cmp = kc_ref.shape[1]
    kc_end = CMP_STRIDE * lax.broadcasted_iota(jnp.int32, (1, n_cmp), 1) + (CMP_BLOCK - 1)
    s = _dot_nt(qg, kc_ref[0]) * ATT_SCALE - slope * (qpos - kc_end).astype(F32)
    p = _masked_softmax(s, kc_end <= qpos)
    o_cmp = _dot(p.astype(BF16), vc_ref[0])

    p4 = p[0:tq] + p[tq:2 * tq] + p[2 * tq:3 * tq] + p[3 * tq:4 * tq]
    hi, lo = _split_bf16(p4)
    imp = _dot(hi, mimp_ref[...]) + _dot(lo, mimp_ref[...])
    qpos_q = start + lax.broadcasted_iota(jnp.int32, (tq, 1), 0)
    sel = _top_blocks(_slc_scores(imp, qpos_q), SLC_TOP).astype(BF16)

    def attend(k_ref, v_ref, t_lo, t_hi, selected):
        def body(t, c):
            m, l, acc = c
            off = pl.multiple_of(t * tk, tk)
            k = k_ref[pl.ds(off, tk), :]
            v = v_ref[pl.ds(off, tk), :]
            dist = qpos - (off + lax.broadcasted_iota(jnp.int32, (1, tk), 1))
            s = _dot_nt(qg, k) * ATT_SCALE - slope * dist.astype(F32)
            if selected:
                mt = _dot(sel, e_ref[t])
                keep = jnp.concatenate([mt] * NSA_GROUP, axis=0) > 0.5
                valid = (dist >= 0) & keep
            else:
                valid = (dist >= 0) & (dist < WINDOW)
            m_new = jnp.maximum(m, jnp.max(jnp.where(valid, s, NEG), axis=-1, keepdims=True))
            alpha = jnp.exp(m - m_new)
            e = jnp.where(valid, jnp.exp(s - m_new), 0.0)
            l = l * alpha + jnp.sum(e, axis=-1, keepdims=True)
            acc = acc * alpha + _dot(e.astype(BF16), v)
            return m_new, l, acc

        init = (jnp.full((rows, 1), NEG, F32), jnp.zeros((rows, 1), F32), jnp.zeros((rows, HEAD_DIM), F32))
        _, l, acc = lax.fori_loop(t_lo, t_hi, body, init)
        return acc / jnp.maximum(l, 1e-30)

    last = (start + tq - 1) // tk
    o_slc = attend(sk_ref, sv_ref, 0, last + 1, True)
    o_win = attend(wk_ref, wv_ref, jnp.maximum(start - WINDOW, 0) // tk, last + 1, False)

    gates = gate_ref[...]
    lane = lax.broadcasted_iota(jnp.int32, gates.shape, 1)

    def gate_col(branch):
        cols = [jnp.sum(jnp.where(lane == branch * N_NSA_HEADS + g * NSA_GROUP + h, gates, 0.0),
                        axis=-1, keepdims=True) for h in range(NSA_GROUP)]
        return jnp.concatenate(cols, axis=0)

    o = gate_col(0) * o_cmp + gate_col(1) * o_slc + gate_col(2) * o_win
    for h in range(NSA_GROUP):
        o_ref[:, h * HEAD_DIM:(h + 1) * HEAD_DIM] = o[h * tq:(h + 1) * tq]


def _nsa_prompt(qn, gates, kvc, nsab, winb, tq=128, tk=256):
    t = qn.shape[0]
    n_cmp = kvc.shape[1]
    gw = NSA_GROUP * HEAD_DIM
    e3 = _block_expansion(t).reshape(LANES, t // tk, tk).transpose(1, 0, 2)
    mimp = _importance_matrix(n_cmp, LANES)
    col = lambda c: (lambda g, i: (0, c(g)))
    return pl.pallas_call(
        functools.partial(_nsa_prompt_kernel, tq=tq, tk=tk),
        grid=(N_NSA_KV, t // tq),
        in_specs=[pl.BlockSpec((tq, gw), lambda g, i: (i, g)),
                  pl.BlockSpec((tq, LANES), lambda g, i: (i, 0)),
                  pl.BlockSpec((1, n_cmp, HEAD_DIM), lambda g, i: (0, 0, g)),
                  pl.BlockSpec((1, n_cmp, HEAD_DIM), lambda g, i: (1, 0, g)),
                  pl.BlockSpec((t, HEAD_DIM), col(lambda g: 2 * N_NSA_KV + g)),
                  pl.BlockSpec((t, HEAD_DIM), col(lambda g: 3 * N_NSA_KV + g)),
                  pl.BlockSpec((t, HEAD_DIM), col(lambda g: g)),
                  pl.BlockSpec((t, HEAD_DIM), col(lambda g: N_NSA_KV + g)),
                  pl.BlockSpec(e3.shape, lambda g, i: (0, 0, 0)),
                  pl.BlockSpec(mimp.shape, lambda g, i: (0, 0))],
        out_specs=pl.BlockSpec((tq, gw), lambda g, i: (i, g)),
        out_shape=jax.ShapeDtypeStruct((t, NSA_W), F32),
        compiler_params=_cparams(("arbitrary", "arbitrary")),
        name="nsa_prompt",
    )(qn, gates, kvc, kvc, nsab, nsab, winb, winb, e3, mimp)


def _nsa_sample_kernel(pt_ref, q_ref, gate_ref, newrows_ref, newwin_ref, cache_ref, win_ref,
                       wpos_ref, wc_ref, e_ref, mimp_ref, o_ref, winout_ref,
                       first_ref, second_ref, ks_ref, vs_ref, *, n_new, n_pages):
    p = pl.program_id(1)
    page = cache_ref.shape[1]
    past = n_pages * page
    cpp = page // CMP_STRIDE
    n_cmp = first_ref.shape[1]
    n_keys = ks_ref.shape[0]
    w_buf = win_ref.shape[1]

    @pl.when(p == 0)
    def _():
        first_ref[...] = jnp.zeros_like(first_ref)
        second_ref[...] = jnp.zeros_like(second_ref)

    x = cache_ref[0]
    c0 = pl.multiple_of(p * cpp, cpp)
    r0 = pl.multiple_of(p * page, page)
    for kv in range(2):
        f, sec = _pool_chunks(x[:, kv * KV_W:(kv + 1) * KV_W], wpos_ref[kv, :CMP_STRIDE, :], wpos_ref[kv, CMP_STRIDE:, :])
        first_ref[kv, pl.ds(c0, cpp), :] = f
        second_ref[kv, pl.ds(c0, cpp), :] = sec
    ks_ref[pl.ds(r0, page), :] = x[:, 2 * KV_W:3 * KV_W].astype(BF16)
    vs_ref[pl.ds(r0, page), :] = x[:, 3 * KV_W:4 * KV_W].astype(BF16)

    @pl.when(p == n_pages - 1)
    def _():
        new = newrows_ref[0]
        zpad = jnp.zeros((CMP_STRIDE - n_new, KV_W), F32)
        kvc = []
        for kv in range(2):
            xc = jnp.concatenate([new[:, kv * KV_W:(kv + 1) * KV_W], zpad], axis=0)
            first_ref[kv, past // CMP_STRIDE:past // CMP_STRIDE + 1, :] = jnp.sum(
                xc * wpos_ref[kv, :CMP_STRIDE, :], axis=0, keepdims=True)
            second_ref[kv, past // CMP_STRIDE:past // CMP_STRIDE + 1, :] = jnp.sum(
                xc * wpos_ref[kv, CMP_STRIDE:, :], axis=0, keepdims=True)
            pre = first_ref[kv] + pltpu.roll(second_ref[kv], n_cmp - 1, axis=0)
            kvc.append([_dot(pre[:, g * HEAD_DIM:(g + 1) * HEAD_DIM].astype(BF16), wc_ref[kv, g]).astype(BF16)
                        for g in range(N_NSA_KV)])
        tail = jnp.zeros((n_keys - past - n_new, KV_W), F32)
        ks_ref[past:, :] = jnp.concatenate([new[:, 2 * KV_W:3 * KV_W], tail], axis=0).astype(BF16)
        vs_ref[past:, :] = jnp.concatenate([new[:, 3 * KV_W:4 * KV_W], tail], axis=0).astype(BF16)

        win = win_ref[0]
        nwin = newwin_ref[0]
        winout_ref[0, :w_buf - n_new, :] = win[n_new:, :]
        winout_ref[0, w_buf - n_new:, :] = nwin
        wtail = jnp.zeros((LANES - n_new, 2 * KV_W), F32)
        win_all = jnp.concatenate([win, nwin, wtail], axis=0).astype(BF16)
        n_win = win_all.shape[0]

        rows = NSA_GROUP * n_new
        r = lax.broadcasted_iota(jnp.int32, (rows, 1), 0)
        qpos = past + r % n_new
        qpos_q = past + lax.broadcasted_iota(jnp.int32, (n_new, 1), 0)
        kc_end = CMP_STRIDE * lax.broadcasted_iota(jnp.int32, (1, n_cmp), 1) + (CMP_BLOCK - 1)
        dist_s = qpos - lax.broadcasted_iota(jnp.int32, (1, n_keys), 1)
        dist_w = qpos - (past - w_buf + lax.broadcasted_iota(jnp.int32, (1, n_win), 1))
        gates = gate_ref[0]
        lane = lax.broadcasted_iota(jnp.int32, gates.shape, 1)

        for g in range(N_NSA_KV):
            gsl = slice(g * HEAD_DIM, (g + 1) * HEAD_DIM)
            qg = jnp.concatenate(
                [q_ref[0, :, (g * NSA_GROUP + h) * HEAD_DIM:(g * NSA_GROUP + h + 1) * HEAD_DIM]
                 for h in range(NSA_GROUP)], axis=0)
            slope = _head_slope(g * NSA_GROUP + r // n_new)
            s = _dot_nt(qg, kvc[0][g]) * ATT_SCALE - slope * (qpos - kc_end).astype(F32)
            pc = _masked_softmax(s, kc_end <= qpos)
            o_cmp = _dot(pc.astype(BF16), kvc[1][g])
            p4 = pc[0:n_new] + pc[n_new:2 * n_new] + pc[2 * n_new:3 * n_new] + pc[3 * n_new:4 * n_new]
            hi, lo = _split_bf16(p4)
            imp = _dot(hi, mimp_ref[...]) + _dot(lo, mimp_ref[...])
            sel = _top_blocks(_slc_scores(imp, qpos_q), SLC_TOP)
            keep = _dot(jnp.concatenate([sel] * NSA_GROUP, axis=0).astype(BF16), e_ref[...]) > 0.5
            s = _dot_nt(qg, ks_ref[:, gsl]) * ATT_SCALE - slope * dist_s.astype(F32)
            ps = _masked_softmax(s, (dist_s >= 0) & keep)
            o_slc = _dot(ps.astype(BF16), vs_ref[:, gsl])
            s = _dot_nt(qg, win_all[:, gsl]) * ATT_SCALE - slope * dist_w.astype(F32)
            pw = _masked_softmax(s, (dist_w >= 0) & (dist_w < WINDOW))
            o_win = _dot(pw.astype(BF16), win_all[:, KV_W + g * HEAD_DIM:KV_W + (g + 1) * HEAD_DIM])

            def gate_col(branch):
                cols = [jnp.sum(jnp.where(lane == branch * N_NSA_HEADS + g * NSA_GROUP + h, gates, 0.0),
                                axis=-1, keepdims=True) for h in range(NSA_GROUP)]
                return jnp.concatenate(cols, axis=0)

            o = gate_col(0) * o_cmp + gate_col(1) * o_slc + gate_col(2) * o_win
            for h in range(NSA_GROUP):
                c = (g * NSA_GROUP + h) * HEAD_DIM
                o_ref[0, :, c:c + HEAD_DIM] = o[h * n_new:(h + 1) * n_new]


def _nsa_sample(page_table, qn, gates, nsarows, winrows, cache, state_win, wpos, wc):
    bs, n_new, _ = qn.shape
    n_pages = page_table.shape[1]
    page = cache.shape[1]
    past = n_pages * page
    w_buf = state_win.shape[1]
    n_keys = past + LANES
    n_cmp = 2 * (past // CMP_STRIDE)
    e = _block_expansion(n_keys)
    mimp = _importance_matrix(n_cmp, LANES)
    seq = lambda b, p, pt: (b, 0, 0)
    c3 = lambda b, p, pt: (0, 0, 0)
    grid_spec = pltpu.PrefetchScalarGridSpec(
        num_scalar_prefetch=1,
        grid=(bs, n_pages),
        in_specs=[pl.BlockSpec((1, n_new, NSA_W), seq),
                  pl.BlockSpec((1, n_new, LANES), seq),
                  pl.BlockSpec((1, n_new, 4 * KV_W), seq),
                  pl.BlockSpec((1, n_new, 2 * KV_W), seq),
                  pl.BlockSpec((1, page, 4 * KV_W), lambda b, p, pt: (pt[b * n_pages + p], 0, 0)),
                  pl.BlockSpec((1, w_buf, 2 * KV_W), seq),
                  pl.BlockSpec(wpos.shape, c3),
                  pl.BlockSpec(wc.shape, lambda b, p, pt: (0, 0, 0, 0)),
                  pl.BlockSpec(e.shape, lambda b, p, pt: (0, 0)),
                  pl.BlockSpec(mimp.shape, lambda b, p, pt: (0, 0))],
        out_specs=[pl.BlockSpec((1, n_new, NSA_W), seq),
                   pl.BlockSpec((1, w_buf, 2 * KV_W), seq)],
        scratch_shapes=[pltpu.VMEM((2, n_cmp, KV_W), F32), pltpu.VMEM((2, n_cmp, KV_W), F32),
                        pltpu.VMEM((n_keys, KV_W), BF16), pltpu.VMEM((n_keys, KV_W), BF16)],
    )
    return pl.pallas_call(
        functools.partial(_nsa_sample_kernel, n_new=n_new, n_pages=n_pages),
        grid_spec=grid_spec,
        out_shape=[jax.ShapeDtypeStruct((bs, n_new, NSA_W), F32),
                   jax.ShapeDtypeStruct((bs, w_buf, 2 * KV_W), F32)],
        compiler_params=_cparams(("arbitrary", "arbitrary")),
        name="nsa_sample",
    )(page_table.reshape(-1), qn, gates, nsarows, winrows, cache, state_win, wpos, wc, e, mimp)


def _mixout_kernel(osb_ref, on_ref, x_ref, gout_ref, w_ref, gate_ref, g2_ref, shift_ref, scale_ref,
                   x1_ref, h2_ref, nrm_ref):
    for h in range(N_SB_HEADS + N_NSA_HEADS):
        src = osb_ref if h < N_SB_HEADS else on_ref
        c = (h % N_SB_HEADS) * HEAD_DIM
        sl = slice(h * HEAD_DIM, (h + 1) * HEAD_DIM)
        nrm_ref[:, sl] = (_rms(src[:, c:c + HEAD_DIM]) * gout_ref[:, sl]).astype(BF16)
    x1 = x_ref[...] + gate_ref[...] * _dot(nrm_ref[...], w_ref[...])
    x1_ref[...] = x1
    h2_ref[...] = (_rms(x1) * g2_ref[...] * (1.0 + scale_ref[...]) + shift_ref[...]).astype(BF16)


def _mixout(osb, on, x, gout, w_out, gate, g2, shift, scale, tm=256):
    m, d = x.shape
    row = lambda i: (i, 0)
    const = lambda i: (0, 0)
    mod_spec = lambda a: pl.BlockSpec((tm, d), row) if a.shape[0] == m else pl.BlockSpec((1, d), const)
    return pl.pallas_call(
        _mixout_kernel,
        grid=(m // tm,),
        in_specs=[pl.BlockSpec((tm, SB_W), row), pl.BlockSpec((tm, NSA_W), row), pl.BlockSpec((tm, d), row),
                  pl.BlockSpec((1, d), const),
                  pl.BlockSpec(w_out.shape, const, pipeline_mode=pl.Buffered(1)),
                  mod_spec(gate), pl.BlockSpec((1, d), const), mod_spec(shift), mod_spec(scale)],
        out_specs=[pl.BlockSpec((tm, d), row), pl.BlockSpec((tm, d), row)],
        out_shape=[jax.ShapeDtypeStruct((m, d), F32), jax.ShapeDtypeStruct((m, d), BF16)],
        scratch_shapes=[pltpu.VMEM((tm, d), BF16)],
        compiler_params=_cparams(("arbitrary",)),
        name="mixout",
    )(osb, on, x, gout, w_out, gate, g2, shift, scale)


def _first_max(v, idx, axis, n):
    m = jnp.max(v, axis=axis, keepdims=True)
    return m, jnp.min(jnp.where(v == m, idx, float(n)), axis=axis, keepdims=True)


def _router_kernel(h_ref, wrt_ref, bias_ref, eidx_ref, w_ref):
    tm = h_ref.shape[0]
    per = N_EXPERTS // N_EXPERT_GROUPS
    s = _sigmoid(_dot_nt(wrt_ref[...], h_ref[...]))
    biased = s + bias_ref[...]
    b3 = biased.reshape(N_EXPERT_GROUPS, per, tm)
    r3 = lax.broadcasted_iota(jnp.int32, b3.shape, 1).astype(F32)
    m1, i1 = _first_max(b3, r3, 1, per)
    m2 = jnp.max(jnp.where(r3 == i1, -jnp.inf, b3), axis=1, keepdims=True)
    gscore = (m1 + m2).reshape(N_EXPERT_GROUPS, tm)
    gi = lax.broadcasted_iota(jnp.int32, gscore.shape, 0).astype(F32)
    gmask = jnp.zeros(gscore.shape, F32)
    for _ in range(TOPK_GROUPS):
        _, idx = _first_max(gscore, gi, 0, N_EXPERT_GROUPS)
        pick = gi == idx
        gmask = jnp.where(pick, 1.0, gmask)
        gscore = jnp.where(pick, -jnp.inf, gscore)
    cand = jnp.where(gmask.reshape(N_EXPERT_GROUPS, 1, tm) > 0.5, b3, -jnp.inf).reshape(N_EXPERTS, tm)
    ei = lax.broadcasted_iota(jnp.int32, cand.shape, 0).astype(F32)
    ids, ws = [], []
    for _ in range(TOP_K):
        _, idx = _first_max(cand, ei, 0, N_EXPERTS)
        pick = ei == idx
        ids.append(idx)
        ws.append(jnp.sum(jnp.where(pick, s, 0.0), axis=0, keepdims=True))
        cand = jnp.where(pick, -jnp.inf, cand)
    w = jnp.concatenate(ws, axis=0)
    eidx_ref[...] = jnp.concatenate(ids, axis=0).astype(jnp.int32)
    w_ref[...] = w / jnp.sum(w, axis=0, keepdims=True) * ROUTED_SCALE


def _router(h2, w_router_t, router_bias, tm=512):
    m, d = h2.shape
    return pl.pallas_call(
        _router_kernel,
        grid=(m // tm,),
        in_specs=[pl.BlockSpec((tm, d), lambda i: (i, 0)),
                  pl.BlockSpec((N_EXPERTS, d), lambda i: (0, 0)),
                  pl.BlockSpec((N_EXPERTS, 1), lambda i: (0, 0))],
        out_specs=[pl.BlockSpec((TOP_K, tm), lambda i: (0, i)), pl.BlockSpec((TOP_K, tm), lambda i: (0, i))],
        out_shape=[jax.ShapeDtypeStruct((TOP_K, m), jnp.int32), jax.ShapeDtypeStruct((TOP_K, m), F32)],
        compiler_params=_cparams(("arbitrary",)),
        name="router",
    )(h2, w_router_t, router_bias.reshape(N_EXPERTS, 1))


def _experts_kernel(be_ref, nact_ref, x_ref, wg_ref, wu_ref, wd_ref, sw_ref, o_ref, wgb, wub, wdb):
    b = pl.program_id(0)
    changed = be_ref[b] != be_ref[jnp.maximum(b - 1, 0)]

    @pl.when((b == 0) | changed)
    def _():
        ch = 256
        for c in range(0, D_MODEL, ch):
            wgb[c:c + ch, :] = wg_ref[0, c:c + ch, :].astype(BF16)
            wub[c:c + ch, :] = wu_ref[0, c:c + ch, :].astype(BF16)
        for c in range(0, D_EXPERT, ch // 4):
            wdb[c:c + ch // 4, :] = wd_ref[0, c:c + ch // 4, :].astype(BF16)

    @pl.when(b < nact_ref[0])
    def _():
        x = x_ref[...]
        gt = _dot(x, wgb[...])
        a = (gt * _sigmoid(gt) * _dot(x, wub[...])).astype(BF16)
        o_ref[...] = _dot(a, wdb[...]) * sw_ref[...]

    @pl.when(b >= nact_ref[0])
    def _():
        o_ref[...] = jnp.zeros_like(o_ref)


def _experts(blk_e, n_active, x_sorted, slot_w, wg, wu, wd):
    p_rows, d = x_sorted.shape
    n_blk = p_rows // MOE_ROWS
    grid_spec = pltpu.PrefetchScalarGridSpec(
        num_scalar_prefetch=2,
        grid=(n_blk,),
        in_specs=[pl.BlockSpec((MOE_ROWS, d), lambda b, be, na: (b, 0)),
                  pl.BlockSpec((1, d, D_EXPERT), lambda b, be, na: (be[b], 0, 0)),
                  pl.BlockSpec((1, d, D_EXPERT), lambda b, be, na: (be[b], 0, 0)),
                  pl.BlockSpec((1, D_EXPERT, d), lambda b, be, na: (be[b], 0, 0)),
                  pl.BlockSpec((MOE_ROWS, 1), lambda b, be, na: (b, 0))],
        out_specs=pl.BlockSpec((MOE_ROWS, d), lambda b, be, na: (b, 0)),
        scratch_shapes=[pltpu.VMEM((d, D_EXPERT), BF16), pltpu.VMEM((d, D_EXPERT), BF16),
                        pltpu.VMEM((D_EXPERT, d), BF16)],
    )
    return pl.pallas_call(
        _experts_kernel,
        grid_spec=grid_spec,
        out_shape=jax.ShapeDtypeStruct((p_rows, d), F32),
        compiler_params=_cparams(("arbitrary",)),
        name="experts",
    )(blk_e, n_active, x_sorted, wg, wu, wd, slot_w)


def _final_kernel(x1_ref, h2_ref, ymoe_ref, wg_ref, wu_ref, wd_ref, gate_ref, gf_ref, shift_ref, scale_ref, y_ref):
    h = h2_ref[...]
    gt = _dot(h, wg_ref[...])
    a = (gt * _sigmoid(gt) * _dot(h, wu_ref[...])).astype(BF16)
    f = ymoe_ref[...] + _dot(a, wd_ref[...])
    x2 = x1_ref[...] + gate_ref[...] * f
    y_ref[...] = _rms(x2) * gf_ref[...] * (1.0 + scale_ref[...]) + shift_ref[...]


def _final(x1, h2, ymoe, wg, wu, wd, gate, gf, shift, scale, tm=256):
    m, d = x1.shape
    row = lambda i: (i, 0)
    const = lambda i: (0, 0)
    mod_spec = lambda a: pl.BlockSpec((tm, d), row) if a.shape[0] == m else pl.BlockSpec((1, d), const)
    return pl.pallas_call(
        _final_kernel,
        grid=(m // tm,),
        in_specs=[pl.BlockSpec((tm, d), row), pl.BlockSpec((tm, d), row), pl.BlockSpec((tm, d), row),
                  pl.BlockSpec(wg.shape, const), pl.BlockSpec(wu.shape, const), pl.BlockSpec(wd.shape, const),
                  mod_spec(gate), pl.BlockSpec((1, d), const), mod_spec(shift), mod_spec(scale)],
        out_specs=pl.BlockSpec((tm, d), row),
        out_shape=jax.ShapeDtypeStruct((m, d), F32),
        compiler_params=_cparams(("arbitrary",)),
        name="final",
    )(x1, h2, ymoe, wg, wu, wd, gate, gf, shift, scale)


def _dispatch_plan(eidx, w):
    t = eidx.shape[0]
    a = t * TOP_K
    flat_e = eidx.reshape(-1)
    order = jnp.argsort(flat_e)
    sorted_e = flat_e[order]
    counts = jnp.bincount(flat_e, length=N_EXPERTS)
    padded = (counts + MOE_ROWS - 1) // MOE_ROWS * MOE_ROWS
    pad_end = jnp.cumsum(padded)
    pad_start = pad_end - padded
    start = jnp.cumsum(counts) - counts
    dest = (pad_start[sorted_e] + jnp.arange(a) - start[sorted_e]).astype(jnp.int32)
    n_blk = -(-a // MOE_ROWS) + N_EXPERTS
    p_rows = n_blk * MOE_ROWS
    slot_tok = jnp.full((p_rows,), t, jnp.int32).at[dest].set((order // TOP_K).astype(jnp.int32))
    slot_w = jnp.zeros((p_rows,), F32).at[dest].set(w.reshape(-1)[order])
    blk_e = jnp.minimum(jnp.searchsorted(pad_end, jnp.arange(n_blk) * MOE_ROWS, side='right'),
                        N_EXPERTS - 1).astype(jnp.int32)
    n_active = (pad_end[-1] // MOE_ROWS).astype(jnp.int32).reshape(1)
    return slot_tok, slot_w, blk_e, n_active


def kernel(x_prompt, x_sample, cache_sb, cache_nsa, state_win, page_table, c_prompt, c_sample, w_ada, b_ada, norm1_g, w_in, cmp_pos_k, cmp_pos_v, cmp_wk, cmp_wv, out_norm_g, w_out, norm2_g, w_router, router_bias, w_gate_e, w_up_e, w_down_e, w_gate_s, w_up_s, w_down_s, normf_g):
    b_p, t, d = x_prompt.shape
    bs, n_new, _ = x_sample.shape
    assert b_p == 1 and d == D_MODEL
    n_s = bs * n_new

    c_all = jnp.concatenate([c_prompt, c_sample], axis=0)
    m_pad = -(-c_all.shape[0] // 8) * 8
    c_all = jnp.pad(c_all, ((0, m_pad - c_all.shape[0]), (0, 0)))
    mod = _ada(c_all, w_ada, b_ada).reshape(m_pad, N_ADA, d)
    mod_p = [mod[0:1, i] for i in range(N_ADA)]
    mod_s = [jnp.repeat(mod[1:1 + bs, i], n_new, axis=0) for i in range(N_ADA)]

    w_main = w_in[:, :COL_GATE]
    w_gate = jnp.pad(w_in[:, COL_GATE:], ((0, 0), (0, LANES - N_GATE)))
    w_cat = jnp.concatenate([w_main, w_gate], axis=1).astype(BF16)
    row1 = lambda v: v.reshape(1, -1)

    xp = x_prompt.reshape(t, d)
    xs = x_sample.reshape(n_s, d)
    (qsb_p, sbrows_p, kvsb_p, qn_p, nsarows_p, nsab_p, winrows_p, winb_p, gates_p) = _project(
        xp, row1(norm1_g), mod_p[0], mod_p[1], w_cat)
    (qsb_s, sbrows_s, _, qn_s, nsarows_s, _, winrows_s, _, gates_s) = _project(
        xs, row1(norm1_g), mod_s[0], mod_s[1], w_cat)

    osb_p = _sb_prompt(qsb_p, kvsb_p)
    wpos = jnp.stack([jnp.repeat(cmp_pos_k, HEAD_DIM, axis=1), jnp.repeat(cmp_pos_v, HEAD_DIM, axis=1)])
    wc = jnp.stack([cmp_wk, cmp_wv]).astype(BF16)
    kvc_p = _compress_prompt(nsarows_p, wpos, wc)
    on_p = _nsa_prompt(qn_p, gates_p, kvc_p, nsab_p, winb_p)

    per_seq = lambda a: a.reshape(bs, n_new, a.shape[-1])
    n_phys, page = cache_sb.shape[:2]
    osb_s = _sb_sample(page_table, per_seq(qsb_s), per_seq(sbrows_s), cache_sb.reshape(n_phys, page, 2 * SB_W))
    on_s, win_s = _nsa_sample(page_table, per_seq(qn_s), per_seq(gates_s), per_seq(nsarows_s), per_seq(winrows_s),
                              cache_nsa.reshape(n_phys, page, 4 * KV_W),
                              state_win.reshape(bs, state_win.shape[1], 2 * KV_W), wpos, wc)

    w_out_b = w_out.astype(BF16)
    gout = out_norm_g.reshape(1, -1)
    x1_p, h2_p = _mixout(osb_p, on_p, xp, gout, w_out_b, mod_p[2], row1(norm2_g), mod_p[3], mod_p[4])
    x1_s, h2_s = _mixout(osb_s.reshape(n_s, SB_W), on_s.reshape(n_s, NSA_W), xs, gout, w_out_b,
                         mod_s[2], row1(norm2_g), mod_s[3], mod_s[4])

    h2 = jnp.concatenate([h2_p, h2_s], axis=0)
    n_tok = h2.shape[0]
    eidx_t, w_t = _router(h2, w_router.T.astype(BF16), router_bias)
    slot_tok, slot_w, blk_e, n_active = _dispatch_plan(eidx_t.T, w_t.T)
    h2_pad = jnp.concatenate([h2, jnp.zeros((1, d), BF16)], axis=0)
    y_sorted = _experts(blk_e, n_active, h2_pad[slot_tok], slot_w.reshape(-1, 1), w_gate_e, w_up_e, w_down_e)
    ymoe = jax.ops.segment_sum(y_sorted, slot_tok, num_segments=n_tok + 1)[:n_tok]

    ws = (w_gate_s.astype(BF16), w_up_s.astype(BF16), w_down_s.astype(BF16))
    y_p = _final(x1_p, h2_p, ymoe[:t], *ws, mod_p[5], row1(normf_g), mod_p[6], mod_p[7])
    y_s = _final(x1_s, h2_s, ymoe[t:], *ws, mod_s[5], row1(normf_g), mod_s[6], mod_s[7])

    keep = min(WINDOW, t)
    return (y_p.reshape(1, t, d), y_s.reshape(bs, n_new, d),
            sbrows_p.reshape(1, t, 2, N_SB_HEADS, HEAD_DIM), sbrows_s.reshape(bs, n_new, 2, N_SB_HEADS, HEAD_DIM),
            nsarows_p.reshape(1, t, 4, N_NSA_KV, HEAD_DIM), nsarows_s.reshape(bs, n_new, 4, N_NSA_KV, HEAD_DIM),
            winrows_p[t - keep:].reshape(1, keep, 2, N_NSA_KV, HEAD_DIM),
            win_s.reshape(bs, state_win.shape[1], 2, N_NSA_KV, HEAD_DIM))
```

```python
import functools

import numpy as np
import jax
import jax.numpy as jnp
from jax import lax
from jax.experimental import pallas as pl
from jax.experimental.pallas import tpu as pltpu

F32 = jnp.float32
BF16 = jnp.bfloat16
U32 = jnp.uint32

D_MODEL = 2048
HEAD_DIM = 128
N_SB_HEADS = 8
N_NSA_HEADS = 8
N_NSA_KV = 2
NSA_GROUP = N_NSA_HEADS // N_NSA_KV
CMP_BLOCK = 32
CMP_STRIDE = 16
SLC_BLOCK = 64
SLC_TOP = 16
WINDOW = 512
FORCE = 1e4
NEG = -1e30
N_EXPERTS = 256
TOP_K = 8
N_EXPERT_GROUPS = 8
TOPK_GROUPS = 4
D_EXPERT = D_MODEL // 4
ROUTED_SCALE = 2.5
MOE_ROWS = 128
N_ADA = 8
EPS = 1e-6
ATT_SCALE = HEAD_DIM ** -0.5

SB_W = N_SB_HEADS * HEAD_DIM
NSA_W = N_NSA_HEADS * HEAD_DIM
KV_W = N_NSA_KV * HEAD_DIM
COL_QSB = 0
COL_KVSB = SB_W
COL_QN = 3 * SB_W
COL_NSA = COL_QN + NSA_W
COL_WIN = COL_NSA + 4 * KV_W
COL_GATE = COL_WIN + 2 * KV_W
N_GATE = 3 * N_NSA_HEADS
LANES = 128
V7X_VMEM_LIMIT = 56 * 1024 * 1024
D_CHUNKS = D_MODEL // LANES
ROW_WORDS = D_CHUNKS // 2

NT = (((1,), (1,)), ((), ()))


def _cparams(sem):
    return pltpu.CompilerParams(dimension_semantics=sem, vmem_limit_bytes=V7X_VMEM_LIMIT)


def _sigmoid(x):
    return 1.0 / (1.0 + jnp.exp(-x))


def _split_bf16(x):
    hi = x.astype(BF16)
    lo = (x - hi.astype(F32)).astype(BF16)
    return hi, lo


def _dot(a, b):
    return jnp.dot(a, b, preferred_element_type=F32)


def _dot_nt(a, b):
    return lax.dot_general(a, b, NT, preferred_element_type=F32)


def _rms(x):
    return x * lax.rsqrt(jnp.mean(x * x, axis=-1, keepdims=True) + EPS)


def _masked_softmax(s, mask):
    s = jnp.where(mask, s, NEG)
    m = jnp.max(s, axis=-1, keepdims=True)
    e = jnp.where(mask, jnp.exp(s - m), 0.0)
    return e / jnp.maximum(jnp.sum(e, axis=-1, keepdims=True), 1e-30)


def _head_slope(head):
    out = jnp.full(head.shape, 2.0 ** -N_NSA_HEADS, F32)
    for i in range(N_NSA_HEADS - 1):
        out = jnp.where(head == i, 2.0 ** -(i + 1), out)
    return out


def _top_blocks(score, n_top):
    blk = lax.broadcasted_iota(jnp.int32, score.shape, 1).astype(F32)
    sel = jnp.zeros(score.shape, F32)
    for _ in range(n_top):
        m = jnp.max(score, axis=-1, keepdims=True)
        idx = jnp.min(jnp.where(score == m, blk, float(score.shape[1])), axis=-1, keepdims=True)
        pick = blk == idx
        sel = jnp.where(pick, 1.0, sel)
        score = jnp.where(pick, -2.0, score)
    return sel


def _slc_scores(imp, qpos):
    blk = lax.broadcasted_iota(jnp.int32, imp.shape, 1)
    qb = qpos // SLC_BLOCK
    forced = (blk == 0) | ((blk <= qb) & (blk >= qb - 1))
    return jnp.where(blk > qb, -1.0, jnp.where(forced, FORCE, imp))


def _gate_cols(gates, branch, group, n_rows_per_head):
    lane = lax.broadcasted_iota(jnp.int32, gates.shape, 1)
    cols = [jnp.sum(jnp.where(lane == branch * N_NSA_HEADS + group * NSA_GROUP + h, gates, 0.0),
                    axis=-1, keepdims=True) for h in range(NSA_GROUP)]
    return jnp.concatenate(cols, axis=0)


def _ada_kernel(c_ref, w_ref, b_ref, o_ref):
    c = c_ref[...]
    a = (c * _sigmoid(c)).astype(BF16)
    o_ref[...] = _dot(a, w_ref[...].astype(BF16)) + b_ref[...]


def _ada(c, w_ada, b_ada, tn=1024):
    m, d = c.shape
    n = w_ada.shape[1]
    return pl.pallas_call(
        _ada_kernel,
        grid=(n // tn,),
        in_specs=[pl.BlockSpec((m, d), lambda j: (0, 0)),
                  pl.BlockSpec((d, tn), lambda j: (0, j)),
                  pl.BlockSpec((1, tn), lambda j: (0, j))],
        out_specs=pl.BlockSpec((m, tn), lambda j: (0, j)),
        out_shape=jax.ShapeDtypeStruct((m, n), F32),
        compiler_params=_cparams(("arbitrary",)),
        name="ada",
    )(c, w_ada, b_ada.reshape(1, n))


def _proj_kernel(x_ref, g_ref, shift_ref, scale_ref, w_ref,
                 qsb_ref, sbrows_ref, kvsb_ref, qn_ref, nsarows_ref, nsab_ref,
                 winrows_ref, winb_ref, gates_ref):
    h = (_rms(x_ref[...]) * g_ref[...] * (1.0 + scale_ref[...]) + shift_ref[...]).astype(BF16)
    ch = 512

    def mm(c0, n):
        return _dot(h, w_ref[:, c0:c0 + n])

    for c in range(0, SB_W, ch):
        qsb_ref[:, c:c + ch] = mm(COL_QSB + c, ch).astype(BF16)
    for c in range(0, 2 * SB_W, ch):
        r = mm(COL_KVSB + c, ch)
        sbrows_ref[:, c:c + ch] = r
        kvsb_ref[:, c:c + ch] = r.astype(BF16)
    for c in range(0, NSA_W, ch):
        qn_ref[:, c:c + ch] = mm(COL_QN + c, ch).astype(BF16)
    for c in range(0, 4 * KV_W, ch):
        r = mm(COL_NSA + c, ch)
        nsarows_ref[:, c:c + ch] = r
        nsab_ref[:, c:c + ch] = r.astype(BF16)
    r = mm(COL_WIN, 2 * KV_W)
    winrows_ref[...] = r
    winb_ref[...] = r.astype(BF16)
    gates_ref[...] = _sigmoid(mm(COL_GATE, LANES))


def _project(x, g, shift, scale, w_cat, tm=256):
    m, d = x.shape
    row = lambda i: (i, 0)
    const = lambda i: (0, 0)
    mod_spec = lambda a: pl.BlockSpec((tm, d), row) if a.shape[0] == m else pl.BlockSpec((1, d), const)
    widths = [(SB_W, BF16), (2 * SB_W, F32), (2 * SB_W, BF16), (NSA_W, BF16), (4 * KV_W, F32),
              (4 * KV_W, BF16), (2 * KV_W, F32), (2 * KV_W, BF16), (LANES, F32)]
    return pl.pallas_call(
        _proj_kernel,
        grid=(m // tm,),
        in_specs=[pl.BlockSpec((tm, d), row),
                  pl.BlockSpec((1, d), const),
                  mod_spec(shift), mod_spec(scale),
                  pl.BlockSpec(w_cat.shape, const, pipeline_mode=pl.Buffered(1))],
        out_specs=[pl.BlockSpec((tm, w), row) for w, _ in widths],
        out_shape=[jax.ShapeDtypeStruct((m, w), dt) for w, dt in widths],
        compiler_params=_cparams(("arbitrary",)),
        name="project",
    )(x, g, shift, scale, w_cat)


def _sb_logits(z):
    sp = jnp.log1p(jnp.exp(-jnp.abs(z)))
    return jnp.minimum(z, 0.0) - sp, jnp.minimum(-z, 0.0) - sp


def _sb_prompt_kernel(q_ref, k_ref, v_ref, o_ref, *, tq, heads):
    i = pl.program_id(1)
    row = lax.broadcasted_iota(jnp.int32, (2 * tq, tq), 0)
    col = lax.broadcasted_iota(jnp.int32, (2 * tq, tq), 1)
    later2 = jnp.where(jnp.where(row >= tq, row - tq, row) > col, 1.0, 0.0).astype(BF16)
    causal = lax.broadcasted_iota(jnp.int32, (tq, tq), 1) < lax.broadcasted_iota(jnp.int32, (tq, tq), 0)

    def block(j, state, diagonal):
        off = pl.multiple_of(j * tq, tq)
        out = []
        for hh in range(heads):
            carry, acc = state[hh]
            hs = slice(hh * HEAD_DIM, (hh + 1) * HEAD_DIM)
            z = _dot_nt(q_ref[:, hs], k_ref[pl.ds(off, tq), hs]) * ATT_SCALE
            ls, l1m = _sb_logits(z)
            if diagonal:
                l1m = jnp.where(causal, l1m, 0.0)
            hi, lo = _split_bf16(l1m)
            suffix = _dot(jnp.concatenate([hi, lo], axis=1), later2) + carry
            a = jnp.exp(ls + suffix)
            if diagonal:
                a = jnp.where(causal, a, 0.0)
            acc = acc + _dot(a.astype(BF16), v_ref[pl.ds(off, tq), hs])
            out.append((suffix[:, :1] + l1m[:, :1], acc))
        return tuple(out)

    init = tuple((jnp.zeros((tq, 1), F32), jnp.zeros((tq, HEAD_DIM), F32)) for _ in range(heads))
    state = block(i, init, True)
    state = lax.fori_loop(0, i, lambda t, st: block(i - 1 - t, st, False), state)
    for hh in range(heads):
        o_ref[:, hh * HEAD_DIM:(hh + 1) * HEAD_DIM] = state[hh][1]


def _sb_prompt(qsb, kvsb, tq=256, heads=2):
    t = qsb.shape[0]
    w = heads * HEAD_DIM
    n_hp = N_SB_HEADS // heads
    return pl.pallas_call(
        functools.partial(_sb_prompt_kernel, tq=tq, heads=heads),
        grid=(n_hp, t // tq),
        in_specs=[pl.BlockSpec((tq, w), lambda h, i: (i, h)),
                  pl.BlockSpec((t, w), lambda h, i: (0, h)),
                  pl.BlockSpec((t, w), lambda h, i: (0, n_hp + h))],
        out_specs=pl.BlockSpec((tq, w), lambda h, i: (i, h)),
        out_shape=jax.ShapeDtypeStruct((t, SB_W), F32),
        compiler_params=_cparams(("arbitrary", "arbitrary")),
        name="sb_prompt",
    )(qsb, kvsb, kvsb)


def _sb_sample_kernel(pt_ref, q_ref, new_ref, *refs, n_new, ppg):
    cache_refs = refs[:ppg]
    o_ref, carry_ref, acc_ref = refs[ppg:]
    p = pl.program_id(1)
    rows = N_SB_HEADS * n_new
    kk = lax.broadcasted_iota(jnp.int32, (2 * LANES, 2 * LANES), 0)
    cc = lax.broadcasted_iota(jnp.int32, (2 * LANES, 2 * LANES), 1)
    kk = jnp.where(kk >= LANES, kk - LANES, kk)
    later_and_all = jnp.where((cc >= LANES) | (kk > cc), 1.0, 0.0).astype(BF16)
    stride = 2 * N_SB_HEADS

    def scores(k_of, diagonal):
        z = jnp.concatenate(
            [_dot_nt(q_ref[0, :, h * HEAD_DIM:(h + 1) * HEAD_DIM], k_of(h)) for h in range(N_SB_HEADS)],
            axis=0) * ATT_SCALE
        ls, l1m = _sb_logits(z)
        causal = None
        if diagonal:
            key = lax.broadcasted_iota(jnp.int32, (rows, LANES), 1)
            qry = lax.broadcasted_iota(jnp.int32, (rows, LANES), 0) % n_new
            causal = key < qry
            l1m = jnp.where(causal, l1m, 0.0)
        hi, lo = _split_bf16(l1m)
        return ls, _dot(jnp.concatenate([hi, lo], axis=1), later_and_all), causal

    def accumulate(ls, st, causal, v_of, carry):
        a = jnp.exp(ls + st[:, :LANES] + carry)
        if causal is not None:
            a = jnp.where(causal, a, 0.0)
        a = a.astype(BF16)
        for h in range(N_SB_HEADS):
            sl = slice(h * n_new, (h + 1) * n_new)
            acc_ref[sl, :] += _dot(a[sl], v_of(h))
        return carry + st[:, LANES:]

    @pl.when(p == 0)
    def _():
        acc_ref[...] = jnp.zeros_like(acc_ref)
        pad = jnp.zeros((LANES - n_new, HEAD_DIM), F32)

        def new_tile(c0):
            return lambda h: jnp.concatenate(
                [new_ref[0, :, c0 + h * HEAD_DIM:c0 + (h + 1) * HEAD_DIM], pad], axis=0).astype(BF16)

        ls, st, causal = scores(new_tile(0), True)
        carry_ref[...] = accumulate(ls, st, causal, new_tile(SB_W), jnp.zeros((rows, LANES), F32))

    def page_tile(ref, c0):
        return lambda h: ref[0, pl.ds(c0 + h, LANES, stride=stride), :].astype(BF16)

    pre = [scores(page_tile(ref, 0), False) for ref in cache_refs]
    carry = carry_ref[...]
    for ref, (ls, st, _) in zip(cache_refs, pre):
        carry = accumulate(ls, st, None, page_tile(ref, N_SB_HEADS), carry)
    carry_ref[...] = carry

    @pl.when(p == pl.num_programs(1) - 1)
    def _():
        for h in range(N_SB_HEADS):
            o_ref[0, :, h * HEAD_DIM:(h + 1) * HEAD_DIM] = acc_ref[h * n_new:(h + 1) * n_new, :]


def _sb_sample(page_table, qsb, sbrows, cache, ppg=4):
    bs, n_new, _ = qsb.shape
    n_pages = page_table.shape[1]
    page_rows = cache.shape[1]
    assert page_rows == LANES * 2 * N_SB_HEADS and n_pages % ppg == 0
    rows = N_SB_HEADS * n_new

    def page_map(q):
        return lambda b, p, pt: (pt[b * n_pages + n_pages - 1 - (p * ppg + q)], 0, 0)

    grid_spec = pltpu.PrefetchScalarGridSpec(
        num_scalar_prefetch=1,
        grid=(bs, n_pages // ppg),
        in_specs=[pl.BlockSpec((1, n_new, SB_W), lambda b, p, pt: (b, 0, 0)),
                  pl.BlockSpec((1, n_new, 2 * SB_W), lambda b, p, pt: (b, 0, 0))]
                 + [pl.BlockSpec((1, page_rows, HEAD_DIM), page_map(q)) for q in range(ppg)],
        out_specs=pl.BlockSpec((1, n_new, SB_W), lambda b, p, pt: (b, 0, 0)),
        scratch_shapes=[pltpu.VMEM((rows, LANES), F32), pltpu.VMEM((rows, HEAD_DIM), F32)],
    )
    return pl.pallas_call(
        functools.partial(_sb_sample_kernel, n_new=n_new, ppg=ppg),
        grid_spec=grid_spec,
        out_shape=jax.ShapeDtypeStruct((bs, n_new, SB_W), F32),
        compiler_params=_cparams(("arbitrary", "arbitrary")),
        name="sb_sample",
    )(page_table.reshape(-1), qsb, sbrows, *([cache] * ppg))


def _pool_chunks(x, w1, w2):
    x3 = x.reshape(x.shape[0] // CMP_STRIDE, CMP_STRIDE, x.shape[1])
    return jnp.sum(x3 * w1[None], axis=1), jnp.sum(x3 * w2[None], axis=1)


def _compress_kernel(rows_ref, wpos_ref, wc_ref, o_ref, first_ref, second_ref, *, step_rows):
    n_rows = rows_ref.shape[0]
    n_chunks = n_rows // CMP_STRIDE
    w1 = wpos_ref[0, :CMP_STRIDE, :]
    w2 = wpos_ref[0, CMP_STRIDE:, :]
    cps = step_rows // CMP_STRIDE

    def body(s, carry):
        r0 = pl.multiple_of(s * step_rows, step_rows)
        c0 = pl.multiple_of(s * cps, cps)
        f, sec = _pool_chunks(rows_ref[pl.ds(r0, step_rows), :], w1, w2)
        first_ref[pl.ds(c0, cps), :] = f
        second_ref[pl.ds(c0, cps), :] = sec
        return carry

    lax.fori_loop(0, n_rows // step_rows, body, 0)
    pre = first_ref[...] + pltpu.roll(second_ref[...], n_chunks - 1, axis=0)
    for g in range(N_NSA_KV):
        sl = slice(g * HEAD_DIM, (g + 1) * HEAD_DIM)
        o_ref[0, :, sl] = _dot(pre[:, sl].astype(BF16), wc_ref[0, g]).astype(BF16)


def _compress_prompt(nsarows, wpos, wc, step_rows=1024):
    t = nsarows.shape[0]
    n_chunks = t // CMP_STRIDE
    return pl.pallas_call(
        functools.partial(_compress_kernel, step_rows=step_rows),
        grid=(2,),
        in_specs=[pl.BlockSpec((t, KV_W), lambda c: (0, c)),
                  pl.BlockSpec((1, CMP_BLOCK, KV_W), lambda c: (c, 0, 0)),
                  pl.BlockSpec((1, N_NSA_KV, HEAD_DIM, HEAD_DIM), lambda c: (c, 0, 0, 0))],
        out_specs=pl.BlockSpec((1, n_chunks, KV_W), lambda c: (c, 0, 0)),
        out_shape=jax.ShapeDtypeStruct((2, n_chunks, KV_W), BF16),
        scratch_shapes=[pltpu.VMEM((n_chunks, KV_W), F32), pltpu.VMEM((n_chunks, KV_W), F32)],
        compiler_params=_cparams(("arbitrary",)),
        name="compress_prompt",
    )(nsarows, wpos, wc)


def _importance_matrix(n_cmp, n_slc_lanes):
    m = np.arange(n_cmp)[:, None]
    j = np.arange(n_slc_lanes)[None, :]
    d = m - (SLC_BLOCK // CMP_STRIDE) * j
    w = np.where((d >= 0) & (d <= 2), 1.0, np.where((d == -1) | (d == 3), 0.5, 0.0))
    return jnp.asarray(w, BF16)


def _block_expansion(n_keys):
    b = np.arange(LANES)[:, None]
    k = np.arange(n_keys)[None, :]
    return jnp.asarray((b == k // SLC_BLOCK).astype(np.float32), BF16)


def _nsa_prompt_kernel(q_ref, gate_ref, kc_ref, vc_ref, sk_ref, sv_ref, wk_ref, wv_ref,
                       e_ref, mimp_ref, o_ref, *, tq, tk):
    i = pl.program_id(0)
    start = i * tq
    rows = NSA_GROUP * tq
    groups = range(N_NSA_KV)
    r = lax.broadcasted_iota(jnp.int32, (rows, 1), 0)
    qpos = start + r % tq
    qpos_q = start + lax.broadcasted_iota(jnp.int32, (tq, 1), 0)
    n_cmp = kc_ref.shape[1]
    kc_end = CMP_STRIDE * lax.broadcasted_iota(jnp.int32, (1, n_cmp), 1) + (CMP_BLOCK - 1)
    gsl = [slice(g * HEAD_DIM, (g + 1) * HEAD_DIM) for g in groups]

    qg, slope, o_cmp, sel = [], [], [], []
    for g in groups:
        qg.append(jnp.concatenate(
            [q_ref[:, (g * NSA_GROUP + h) * HEAD_DIM:(g * NSA_GROUP + h + 1) * HEAD_DIM] for h in range(NSA_GROUP)],
            axis=0))
        slope.append(_head_slope(g * NSA_GROUP + r // tq))
        s = _dot_nt(qg[g], kc_ref[0, :, gsl[g]]) * ATT_SCALE - slope[g] * (qpos - kc_end).astype(F32)
        p = _masked_softmax(s, kc_end <= qpos)
        o_cmp.append(_dot(p.astype(BF16), vc_ref[0, :, gsl[g]]))
        p4 = p[0:tq] + p[tq:2 * tq] + p[2 * tq:3 * tq] + p[3 * tq:4 * tq]
        hi, lo = _split_bf16(p4)
        imp = _dot(hi, mimp_ref[...]) + _dot(lo, mimp_ref[...])
        sel.append(_top_blocks(_slc_scores(imp, qpos_q), SLC_TOP).astype(BF16))

    def attend(k_ref, v_ref, t_lo, t_hi, selected):
        def body(t, state):
            off = pl.multiple_of(t * tk, tk)
            dist = qpos - (off + lax.broadcasted_iota(jnp.int32, (1, tk), 1))
            out = []
            for g in groups:
                m, l, acc = state[g]
                s = _dot_nt(qg[g], k_ref[pl.ds(off, tk), gsl[g]]) * ATT_SCALE - slope[g] * dist.astype(F32)
                if selected:
                    mt = _dot(sel[g], e_ref[t])
                    valid = (dist >= 0) & (jnp.concatenate([mt] * NSA_GROUP, axis=0) > 0.5)
                else:
                    valid = (dist >= 0) & (dist < WINDOW)
                m_new = jnp.maximum(m, jnp.max(jnp.where(valid, s, NEG), axis=-1, keepdims=True))
                alpha = jnp.exp(m - m_new)
                e = jnp.where(valid, jnp.exp(s - m_new), 0.0)
                l = l * alpha + jnp.sum(e, axis=-1, keepdims=True)
                acc = acc * alpha + _dot(e.astype(BF16), v_ref[pl.ds(off, tk), gsl[g]])
                out.append((m_new, l, acc))
            return tuple(out)

        init = tuple((jnp.full((rows, 1), NEG, F32), jnp.zeros((rows, 1), F32), jnp.zeros((rows, HEAD_DIM), F32))
                     for _ in groups)
        state = lax.fori_loop(t_lo, t_hi, body, init)
        return [state[g][2] / jnp.maximum(state[g][1], 1e-30) for g in groups]

    last = (start + tq - 1) // tk
    o_slc = attend(sk_ref, sv_ref, 0, last + 1, True)
    o_win = attend(wk_ref, wv_ref, jnp.maximum(start - WINDOW, 0) // tk, last + 1, False)

    gates = gate_ref[...]
    for g in groups:
        o = (_gate_cols(gates, 0, g, tq) * o_cmp[g] + _gate_cols(gates, 1, g, tq) * o_slc[g]
             + _gate_cols(gates, 2, g, tq) * o_win[g])
        for h in range(NSA_GROUP):
            c = (g * NSA_GROUP + h) * HEAD_DIM
            o_ref[:, c:c + HEAD_DIM] = o[h * tq:(h + 1) * tq]


def _nsa_prompt(qn, gates, kvc, nsab, winb, tq=128, tk=256):
    t = qn.shape[0]
    n_cmp = kvc.shape[1]
    e3 = _block_expansion(t).reshape(LANES, t // tk, tk).transpose(1, 0, 2)
    mimp = _importance_matrix(n_cmp, LANES)
    once = pl.Buffered(1)
    col = lambda c: pl.BlockSpec((t, KV_W), lambda i: (0, c), pipeline_mode=once)
    return pl.pallas_call(
        functools.partial(_nsa_prompt_kernel, tq=tq, tk=tk),
        grid=(t // tq,),
        in_specs=[pl.BlockSpec((tq, NSA_W), lambda i: (i, 0)),
                  pl.BlockSpec((tq, LANES), lambda i: (i, 0)),
                  pl.BlockSpec((1, n_cmp, KV_W), lambda i: (0, 0, 0), pipeline_mode=once),
                  pl.BlockSpec((1, n_cmp, KV_W), lambda i: (1, 0, 0), pipeline_mode=once),
                  col(2), col(3), col(0), col(1),
                  pl.BlockSpec(e3.shape, lambda i: (0, 0, 0), pipeline_mode=once),
                  pl.BlockSpec(mimp.shape, lambda i: (0, 0), pipeline_mode=once)],
        out_specs=pl.BlockSpec((tq, NSA_W), lambda i: (i, 0)),
        out_shape=jax.ShapeDtypeStruct((t, NSA_W), F32),
        compiler_params=_cparams(("arbitrary",)),
        name="nsa_prompt",
    )(qn, gates, kvc, kvc, nsab, nsab, winb, winb, e3, mimp)


def _nsa_sample_kernel(pt_ref, q_ref, gate_ref, newrows_ref, newwin_ref, win_ref, wpos_ref, wc_ref,
                       e_ref, mimp_ref, *refs, n_new, n_pages, ppg):
    cache_refs = refs[:ppg]
    o_ref, winout_ref, first_ref, second_ref, ks_ref, vs_ref = refs[ppg:]
    p = pl.program_id(1)
    page = LANES
    past = n_pages * page
    cpp = page // CMP_STRIDE
    n_cmp = first_ref.shape[1]
    n_keys = ks_ref.shape[0]
    w_buf = win_ref.shape[1] // (2 * N_NSA_KV)
    gsl = [slice(g * HEAD_DIM, (g + 1) * HEAD_DIM) for g in range(N_NSA_KV)]
    kinds = 4 * N_NSA_KV

    @pl.when(p == 0)
    def _():
        first_ref[...] = jnp.zeros_like(first_ref)
        second_ref[...] = jnp.zeros_like(second_ref)

    for q, cache_ref in enumerate(cache_refs):
        pg = p * ppg + q
        c0 = pl.multiple_of(pg * cpp, cpp)
        r0 = pl.multiple_of(pg * page, page)
        for g in range(N_NSA_KV):
            for kv in range(2):
                x = cache_ref[0, pl.ds(kv * N_NSA_KV + g, page, stride=kinds), :]
                f, sec = _pool_chunks(x, wpos_ref[kv, :CMP_STRIDE, gsl[g]], wpos_ref[kv, CMP_STRIDE:, gsl[g]])
                first_ref[kv, pl.ds(c0, cpp), gsl[g]] = f
                second_ref[kv, pl.ds(c0, cpp), gsl[g]] = sec
            ks_ref[pl.ds(r0, page), gsl[g]] = cache_ref[0, pl.ds(2 * N_NSA_KV + g, page, stride=kinds), :].astype(BF16)
            vs_ref[pl.ds(r0, page), gsl[g]] = cache_ref[0, pl.ds(3 * N_NSA_KV + g, page, stride=kinds), :].astype(BF16)

    @pl.when(p == pl.num_programs(1) - 1)
    def _():
        new = newrows_ref[0]
        zpad = jnp.zeros((CMP_STRIDE - n_new, KV_W), F32)
        kvc = []
        for kv in range(2):
            xc = jnp.concatenate([new[:, kv * KV_W:(kv + 1) * KV_W], zpad], axis=0)
            first_ref[kv, past // CMP_STRIDE:past // CMP_STRIDE + 1, :] = jnp.sum(
                xc * wpos_ref[kv, :CMP_STRIDE, :], axis=0, keepdims=True)
            second_ref[kv, past // CMP_STRIDE:past // CMP_STRIDE + 1, :] = jnp.sum(
                xc * wpos_ref[kv, CMP_STRIDE:, :], axis=0, keepdims=True)
            pre = first_ref[kv] + pltpu.roll(second_ref[kv], n_cmp - 1, axis=0)
            kvc.append([_dot(pre[:, gsl[g]].astype(BF16), wc_ref[kv, g]).astype(BF16) for g in range(N_NSA_KV)])
        tail = jnp.zeros((n_keys - past - n_new, KV_W), F32)
        ks_ref[past:, :] = jnp.concatenate([new[:, 2 * KV_W:3 * KV_W], tail], axis=0).astype(BF16)
        vs_ref[past:, :] = jnp.concatenate([new[:, 3 * KV_W:4 * KV_W], tail], axis=0).astype(BF16)

        nwin = newwin_ref[0]
        wr = 2 * N_NSA_KV
        winout_ref[0, :(w_buf - n_new) * wr, :] = win_ref[0, n_new * wr:, :]
        for c in range(wr):
            winout_ref[0, pl.ds((w_buf - n_new) * wr + c, n_new, stride=wr), :] = nwin[:, c * HEAD_DIM:(c + 1) * HEAD_DIM]
        wtail = jnp.zeros((LANES - n_new, HEAD_DIM), F32)

        def window(c):
            return jnp.concatenate([win_ref[0, pl.ds(c, w_buf, stride=wr), :],
                                    nwin[:, c * HEAD_DIM:(c + 1) * HEAD_DIM], wtail], axis=0).astype(BF16)

        n_win = w_buf + LANES
        rows = NSA_GROUP * n_new
        r = lax.broadcasted_iota(jnp.int32, (rows, 1), 0)
        qpos = past + r % n_new
        qpos_q = past + lax.broadcasted_iota(jnp.int32, (n_new, 1), 0)
        kc_end = CMP_STRIDE * lax.broadcasted_iota(jnp.int32, (1, n_cmp), 1) + (CMP_BLOCK - 1)
        dist_s = qpos - lax.broadcasted_iota(jnp.int32, (1, n_keys), 1)
        dist_w = qpos - (past - w_buf + lax.broadcasted_iota(jnp.int32, (1, n_win), 1))
        gates = gate_ref[0]

        for g in range(N_NSA_KV):
            qg = jnp.concatenate(
                [q_ref[0, :, (g * NSA_GROUP + h) * HEAD_DIM:(g * NSA_GROUP + h + 1) * HEAD_DIM]
                 for h in range(NSA_GROUP)], axis=0)
            slope = _head_slope(g * NSA_GROUP + r // n_new)
            s = _dot_nt(qg, kvc[0][g]) * ATT_SCALE - slope * (qpos - kc_end).astype(F32)
            pc = _masked_softmax(s, kc_end <= qpos)
            o_cmp = _dot(pc.astype(BF16), kvc[1][g])
            p4 = pc[0:n_new] + pc[n_new:2 * n_new] + pc[2 * n_new:3 * n_new] + pc[3 * n_new:4 * n_new]
            hi, lo = _split_bf16(p4)
            imp = _dot(hi, mimp_ref[...]) + _dot(lo, mimp_ref[...])
            sel = _top_blocks(_slc_scores(imp, qpos_q), SLC_TOP)
            keep = _dot(jnp.concatenate([sel] * NSA_GROUP, axis=0).astype(BF16), e_ref[...]) > 0.5
            s = _dot_nt(qg, ks_ref[:, gsl[g]]) * ATT_SCALE - slope * dist_s.astype(F32)
            ps = _masked_softmax(s, (dist_s >= 0) & keep)
            o_slc = _dot(ps.astype(BF16), vs_ref[:, gsl[g]])
            s = _dot_nt(qg, window(g)) * ATT_SCALE - slope * dist_w.astype(F32)
            pw = _masked_softmax(s, (dist_w >= 0) & (dist_w < WINDOW))
            o_win = _dot(pw.astype(BF16), window(N_NSA_KV + g))

            o = (_gate_cols(gates, 0, g, n_new) * o_cmp + _gate_cols(gates, 1, g, n_new) * o_slc
                 + _gate_cols(gates, 2, g, n_new) * o_win)
            for h in range(NSA_GROUP):
                c = (g * NSA_GROUP + h) * HEAD_DIM
                o_ref[0, :, c:c + HEAD_DIM] = o[h * n_new:(h + 1) * n_new]


def _nsa_sample(page_table, qn, gates, nsarows, winrows, cache, state_win, wpos, wc, ppg=4):
    bs, n_new, _ = qn.shape
    n_pages = page_table.shape[1]
    page_rows = cache.shape[1]
    assert page_rows == LANES * 4 * N_NSA_KV and n_pages % ppg == 0
    past = n_pages * LANES
    win_rows = state_win.shape[1]
    n_keys = past + LANES
    n_cmp = 2 * (past // CMP_STRIDE)
    e = _block_expansion(n_keys)
    mimp = _importance_matrix(n_cmp, LANES)
    seq = lambda b, p, pt: (b, 0, 0)
    c3 = lambda b, p, pt: (0, 0, 0)

    def page_map(q):
        return lambda b, p, pt: (pt[b * n_pages + p * ppg + q], 0, 0)

    grid_spec = pltpu.PrefetchScalarGridSpec(
        num_scalar_prefetch=1,
        grid=(bs, n_pages // ppg),
        in_specs=[pl.BlockSpec((1, n_new, NSA_W), seq),
                  pl.BlockSpec((1, n_new, LANES), seq),
                  pl.BlockSpec((1, n_new, 4 * KV_W), seq),
                  pl.BlockSpec((1, n_new, 2 * KV_W), seq),
                  pl.BlockSpec((1, win_rows, HEAD_DIM), seq),
                  pl.BlockSpec(wpos.shape, c3),
                  pl.BlockSpec(wc.shape, lambda b, p, pt: (0, 0, 0, 0)),
                  pl.BlockSpec(e.shape, lambda b, p, pt: (0, 0)),
                  pl.BlockSpec(mimp.shape, lambda b, p, pt: (0, 0))]
                 + [pl.BlockSpec((1, page_rows, HEAD_DIM), page_map(q)) for q in range(ppg)],
        out_specs=[pl.BlockSpec((1, n_new, NSA_W), seq),
                   pl.BlockSpec((1, win_rows, HEAD_DIM), seq)],
        scratch_shapes=[pltpu.VMEM((2, n_cmp, KV_W), F32), pltpu.VMEM((2, n_cmp, KV_W), F32),
                        pltpu.VMEM((n_keys, KV_W), BF16), pltpu.VMEM((n_keys, KV_W), BF16)],
    )
    return pl.pallas_call(
        functools.partial(_nsa_sample_kernel, n_new=n_new, n_pages=n_pages, ppg=ppg),
        grid_spec=grid_spec,
        out_shape=[jax.ShapeDtypeStruct((bs, n_new, NSA_W), F32),
                   jax.ShapeDtypeStruct((bs, win_rows, HEAD_DIM), F32)],
        compiler_params=_cparams(("arbitrary", "arbitrary")),
        name="nsa_sample",
    )(page_table.reshape(-1), qn, gates, nsarows, winrows, state_win, wpos, wc, e, mimp, *([cache] * ppg))


def _mixout_kernel(osb_ref, on_ref, x_ref, gout_ref, w_ref, gate_ref, g2_ref, shift_ref, scale_ref, *rest):
    x1_ref, h2_ref, h2g_ref, nrm_ref = rest[-4:]
    tm = x_ref.shape[0]
    for h in range(N_SB_HEADS + N_NSA_HEADS):
        src = osb_ref if h < N_SB_HEADS else on_ref
        c = (h % N_SB_HEADS) * HEAD_DIM
        sl = slice(h * HEAD_DIM, (h + 1) * HEAD_DIM)
        nrm_ref[:, sl] = (_rms(src[:, c:c + HEAD_DIM]) * gout_ref[:, sl]).astype(BF16)
    x1 = x_ref[...] + gate_ref[...] * _dot(nrm_ref[...], w_ref[...])
    x1_ref[...] = x1
    h2 = _rms(x1) * g2_ref[...] * (1.0 + scale_ref[...]) + shift_ref[...]
    h2_ref[...] = h2.astype(BF16)
    for s in range(ROW_WORDS):
        lo = h2[:, (2 * s) * LANES:(2 * s + 1) * LANES].astype(BF16).astype(F32)
        hi = h2[:, (2 * s + 1) * LANES:(2 * s + 2) * LANES].astype(BF16).astype(F32)
        word = (lax.bitcast_convert_type(lo, U32) >> 16) | lax.bitcast_convert_type(hi, U32)
        h2g_ref[pl.ds(s, tm, stride=ROW_WORDS), :] = word


def _mixout(osb, on, x, gout, w_out, gate, g2, shift, scale, n_total, row0, prev=None, tm=256):
    m, d = x.shape
    blk0 = row0 // tm
    row = lambda i: (i, 0)
    out_row = lambda i: (blk0 + i, 0)
    const = lambda i: (0, 0)
    mod_spec = lambda a: pl.BlockSpec((tm, d), row) if a.shape[0] == m else pl.BlockSpec((1, d), const)
    in_specs = [pl.BlockSpec((tm, SB_W), row), pl.BlockSpec((tm, NSA_W), row), pl.BlockSpec((tm, d), row),
                pl.BlockSpec((1, d), const),
                pl.BlockSpec(w_out.shape, const, pipeline_mode=pl.Buffered(1)),
                mod_spec(gate), pl.BlockSpec((1, d), const), mod_spec(shift), mod_spec(scale)]
    args = [osb, on, x, gout, w_out, gate, g2, shift, scale]
    aliases = {}
    if prev is not None:
        in_specs += [pl.BlockSpec(memory_space=pl.ANY)] * 3
        aliases = {len(args) + j: j for j in range(3)}
        args += list(prev)
    return pl.pallas_call(
        _mixout_kernel,
        grid=(m // tm,),
        in_specs=in_specs,
        out_specs=[pl.BlockSpec((tm, d), out_row), pl.BlockSpec((tm, d), out_row),
                   pl.BlockSpec((tm * ROW_WORDS, LANES), out_row)],
        out_shape=[jax.ShapeDtypeStruct((n_total, d), F32), jax.ShapeDtypeStruct((n_total, d), BF16),
                   jax.ShapeDtypeStruct((n_total * ROW_WORDS, LANES), U32)],
        scratch_shapes=[pltpu.VMEM((tm, d), BF16)],
        input_output_aliases=aliases,
        compiler_params=_cparams(("arbitrary",)),
        name="mixout",
    )(*args)


def _first_max(v, idx, axis, n):
    m = jnp.max(v, axis=axis, keepdims=True)
    return m, jnp.min(jnp.where(v == m, idx, float(n)), axis=axis, keepdims=True)


def _router_kernel(h_ref, wrt_ref, bias_ref, eidx_ref, w_ref):
    tm = h_ref.shape[0]
    per = N_EXPERTS // N_EXPERT_GROUPS
    s = _sigmoid(_dot_nt(wrt_ref[...], h_ref[...]))
    biased = s + bias_ref[...]
    b3 = biased.reshape(N_EXPERT_GROUPS, per, tm)
    r3 = lax.broadcasted_iota(jnp.int32, b3.shape, 1).astype(F32)
    m1, i1 = _first_max(b3, r3, 1, per)
    m2 = jnp.max(jnp.where(r3 == i1, -jnp.inf, b3), axis=1, keepdims=True)
    gscore = (m1 + m2).reshape(N_EXPERT_GROUPS, tm)
    gi = lax.broadcasted_iota(jnp.int32, gscore.shape, 0).astype(F32)
    gmask = jnp.zeros(gscore.shape, F32)
    for _ in range(TOPK_GROUPS):
        _, idx = _first_max(gscore, gi, 0, N_EXPERT_GROUPS)
        pick = gi == idx
        gmask = jnp.where(pick, 1.0, gmask)
        gscore = jnp.where(pick, -jnp.inf, gscore)
    cand = jnp.where(gmask.reshape(N_EXPERT_GROUPS, 1, tm) > 0.5, b3, -jnp.inf).reshape(N_EXPERTS, tm)
    ei = lax.broadcasted_iota(jnp.int32, cand.shape, 0).astype(F32)
    ids, ws = [], []
    for _ in range(TOP_K):
        _, idx = _first_max(cand, ei, 0, N_EXPERTS)
        pick = ei == idx
        ids.append(idx)
        ws.append(jnp.sum(jnp.where(pick, s, 0.0), axis=0, keepdims=True))
        cand = jnp.where(pick, -jnp.inf, cand)
    w = jnp.concatenate(ws, axis=0)
    eidx_ref[...] = jnp.concatenate(ids, axis=0).astype(jnp.int32)
    w_ref[...] = w / jnp.sum(w, axis=0, keepdims=True) * ROUTED_SCALE


def _router(h2, w_router_t, router_bias, tm=512):
    m, d = h2.shape
    return pl.pallas_call(
        _router_kernel,
        grid=(m // tm,),
        in_specs=[pl.BlockSpec((tm, d), lambda i: (i, 0)),
                  pl.BlockSpec((N_EXPERTS, d), lambda i: (0, 0)),
                  pl.BlockSpec((N_EXPERTS, 1), lambda i: (0, 0))],
        out_specs=[pl.BlockSpec((TOP_K, tm), lambda i: (0, i)), pl.BlockSpec((TOP_K, tm), lambda i: (0, i))],
        out_shape=[jax.ShapeDtypeStruct((TOP_K, m), jnp.int32), jax.ShapeDtypeStruct((TOP_K, m), F32)],
        compiler_params=_cparams(("arbitrary",)),
        name="router",
    )(h2, w_router_t, router_bias.reshape(N_EXPERTS, 1))


def _dispatch_plan(eidx):
    t = eidx.shape[0]
    onehot = eidx[:, :, None] == jnp.arange(N_EXPERTS, dtype=jnp.int32)[None, None, :]
    routed = jnp.any(onehot, axis=1).astype(jnp.int32)
    incl = jnp.cumsum(routed, axis=0)
    counts = incl[-1]
    padded = (counts + MOE_ROWS - 1) // MOE_ROWS * MOE_ROWS
    pad_end = jnp.cumsum(padded)
    base = (pad_end - padded)[None, :] + incl - routed
    dest = jnp.sum(jnp.where(onehot, base[:, None, :], 0), axis=-1).astype(jnp.int32)
    n_blk = -(-(t * TOP_K) // MOE_ROWS) + N_EXPERTS
    blk_e = jnp.minimum(jnp.searchsorted(pad_end, jnp.arange(n_blk) * MOE_ROWS, side='right'),
                        N_EXPERTS - 1).astype(jnp.int32)
    n_active = (pad_end[-1] // MOE_ROWS).astype(jnp.int32).reshape(1)
    return dest, blk_e, n_active, n_blk


def _dispatch_kernel(dest_ref, h2g_ref, xs_ref, sem, *, tm):
    i = pl.program_id(0)

    def body(r, carry):
        src = h2g_ref.at[pl.ds(pl.multiple_of(r * ROW_WORDS, ROW_WORDS), ROW_WORDS), :]
        for k in range(TOP_K):
            d = dest_ref[(i * tm + r) * TOP_K + k]
            pltpu.make_async_copy(src, xs_ref.at[pl.ds(pl.multiple_of(d * ROW_WORDS, ROW_WORDS), ROW_WORDS), :],
                                  sem).start()
        return carry

    lax.fori_loop(0, tm, body, 0)
    done = xs_ref.at[pl.ds(0, tm * TOP_K * ROW_WORDS), :]
    pltpu.make_async_copy(done, done, sem).wait()


def _dispatch(dest_flat, h2g, p_rows, tm=256):
    n_tok = h2g.shape[0] // ROW_WORDS
    grid_spec = pltpu.PrefetchScalarGridSpec(
        num_scalar_prefetch=1,
        grid=(n_tok // tm,),
        in_specs=[pl.BlockSpec((tm * ROW_WORDS, LANES), lambda i, d: (i, 0))],
        out_specs=pl.BlockSpec(memory_space=pl.ANY),
        scratch_shapes=[pltpu.SemaphoreType.DMA(())],
    )
    return pl.pallas_call(
        functools.partial(_dispatch_kernel, tm=tm),
        grid_spec=grid_spec,
        out_shape=jax.ShapeDtypeStruct((p_rows * ROW_WORDS, LANES), U32),
        compiler_params=_cparams(("arbitrary",)),
        name="dispatch",
    )(dest_flat, h2g)


def _experts_kernel(be_ref, nact_ref, x_ref, wg_ref, wu_ref, wd_ref, o_ref, wgb, wub, wdb):
    b = pl.program_id(0)
    active = b < nact_ref[0]
    changed = be_ref[b] != be_ref[jnp.maximum(b - 1, 0)]

    @pl.when(active & ((b == 0) | changed))
    def _():
        ch = 256
        for c in range(0, D_MODEL, ch):
            wgb[c:c + ch, :] = wg_ref[0, c:c + ch, :].astype(BF16)
            wub[c:c + ch, :] = wu_ref[0, c:c + ch, :].astype(BF16)
        for c in range(0, D_EXPERT, ch // 4):
            wdb[c:c + ch // 4, :] = wd_ref[0, c:c + ch // 4, :].astype(BF16)

    @pl.when(active)
    def _():
        chunks = []
        for s in range(ROW_WORDS):
            word = x_ref[pl.ds(s, MOE_ROWS, stride=ROW_WORDS), :]
            chunks.append(lax.bitcast_convert_type(word << 16, F32).astype(BF16))
            chunks.append(lax.bitcast_convert_type(word & jnp.uint32(0xFFFF0000), F32).astype(BF16))
        x = jnp.concatenate(chunks, axis=1)
        gt = _dot(x, wgb[...])
        a = (gt * _sigmoid(gt) * _dot(x, wub[...])).astype(BF16)
        y = _dot(a, wdb[...])
        for j in range(D_CHUNKS):
            o_ref[pl.ds(j, MOE_ROWS, stride=D_CHUNKS), :] = y[:, j * LANES:(j + 1) * LANES]


def _experts(blk_e, n_active, x_sorted, wg, wu, wd):
    d = wg.shape[1]
    n_blk = x_sorted.shape[0] // (MOE_ROWS * ROW_WORDS)
    grid_spec = pltpu.PrefetchScalarGridSpec(
        num_scalar_prefetch=2,
        grid=(n_blk,),
        in_specs=[pl.BlockSpec((MOE_ROWS * ROW_WORDS, LANES), lambda b, be, na: (b, 0)),
                  pl.BlockSpec((1, d, D_EXPERT), lambda b, be, na: (be[b], 0, 0)),
                  pl.BlockSpec((1, d, D_EXPERT), lambda b, be, na: (be[b], 0, 0)),
                  pl.BlockSpec((1, D_EXPERT, d), lambda b, be, na: (be[b], 0, 0))],
        out_specs=pl.BlockSpec((MOE_ROWS * D_CHUNKS, LANES), lambda b, be, na: (b, 0)),
        scratch_shapes=[pltpu.VMEM((d, D_EXPERT), BF16), pltpu.VMEM((d, D_EXPERT), BF16),
                        pltpu.VMEM((D_EXPERT, d), BF16)],
    )
    return pl.pallas_call(
        _experts_kernel,
        grid_spec=grid_spec,
        out_shape=jax.ShapeDtypeStruct((n_blk * MOE_ROWS * D_CHUNKS, LANES), F32),
        compiler_params=_cparams(("arbitrary",)),
        name="experts",
    )(blk_e, n_active, x_sorted, wg, wu, wd)


def _final_kernel(dest_ref, x1_ref, h2_ref, w8_ref, ys_ref, wg_ref, wu_ref, wd_ref, gate_ref, gf_ref,
                  shift_ref, scale_ref, y_ref, ybuf, ymoe_ref, sem, *, tm, row0):
    i = pl.program_id(0)
    n = pl.num_programs(0)
    tile_rows = tm * TOP_K * D_CHUNKS

    def issue(tile, slot):
        def body(r, carry):
            for k in range(TOP_K):
                d = dest_ref[(row0 + tile * tm + r) * TOP_K + k]
                pltpu.make_async_copy(
                    ys_ref.at[pl.ds(pl.multiple_of(d * D_CHUNKS, D_CHUNKS), D_CHUNKS), :],
                    ybuf.at[slot, pl.ds(pl.multiple_of((r * TOP_K + k) * D_CHUNKS, D_CHUNKS), D_CHUNKS), :],
                    sem.at[slot]).start()
            return carry
        lax.fori_loop(0, tm, body, 0)

    @pl.when(i == 0)
    def _():
        issue(0, 0)

    @pl.when(i + 1 < n)
    def _():
        issue(i + 1, (i + 1) % 2)

    slot = i % 2
    pltpu.make_async_copy(ys_ref.at[pl.ds(0, tile_rows), :], ybuf.at[slot], sem.at[slot]).wait()
    w8 = w8_ref[...]
    wk = [w8[:, k:k + 1] for k in range(TOP_K)]
    for j in range(D_CHUNKS):
        acc = None
        for k in range(TOP_K):
            v = ybuf[slot, pl.ds(k * D_CHUNKS + j, tm, stride=TOP_K * D_CHUNKS), :] * wk[k]
            acc = v if acc is None else acc + v
        ymoe_ref[:, j * LANES:(j + 1) * LANES] = acc

    h = h2_ref[...]
    gt = _dot(h, wg_ref[...])
    a = (gt * _sigmoid(gt) * _dot(h, wu_ref[...])).astype(BF16)
    f = ymoe_ref[...] + _dot(a, wd_ref[...])
    x2 = x1_ref[...] + gate_ref[...] * f
    y_ref[...] = _rms(x2) * gf_ref[...] * (1.0 + scale_ref[...]) + shift_ref[...]


def _final(dest_flat, x1, h2, w8, y_sorted, wg, wu, wd, gate, gf, shift, scale, m, row0, tm=64):
    d = x1.shape[1]
    blk0 = row0 // tm
    in_row = lambda i, dr: (blk0 + i, 0)
    row = lambda i, dr: (i, 0)
    const = lambda i, dr: (0, 0)
    mod_spec = lambda a: pl.BlockSpec((tm, d), row) if a.shape[0] == m else pl.BlockSpec((1, d), const)
    grid_spec = pltpu.PrefetchScalarGridSpec(
        num_scalar_prefetch=1,
        grid=(m // tm,),
        in_specs=[pl.BlockSpec((tm, d), in_row), pl.BlockSpec((tm, d), in_row), pl.BlockSpec((tm, TOP_K), in_row),
                  pl.BlockSpec(memory_space=pl.ANY),
                  pl.BlockSpec(wg.shape, const), pl.BlockSpec(wu.shape, const), pl.BlockSpec(wd.shape, const),
                  mod_spec(gate), pl.BlockSpec((1, d), const), mod_spec(shift), mod_spec(scale)],
        out_specs=pl.BlockSpec((tm, d), row),
        scratch_shapes=[pltpu.VMEM((2, tm * TOP_K * D_CHUNKS, LANES), F32), pltpu.VMEM((tm, d), F32),
                        pltpu.SemaphoreType.DMA((2,))],
    )
    return pl.pallas_call(
        functools.partial(_final_kernel, tm=tm, row0=row0),
        grid_spec=grid_spec,
        out_shape=jax.ShapeDtypeStruct((m, d), F32),
        compiler_params=_cparams(("arbitrary",)),
        name="final",
    )(dest_flat, x1, h2, w8, y_sorted, wg, wu, wd, gate, gf, shift, scale)


def kernel(x_prompt, x_sample, cache_sb, cache_nsa, state_win, page_table, c_prompt, c_sample, w_ada, b_ada, norm1_g, w_in, cmp_pos_k, cmp_pos_v, cmp_wk, cmp_wv, out_norm_g, w_out, norm2_g, w_router, router_bias, w_gate_e, w_up_e, w_down_e, w_gate_s, w_up_s, w_down_s, normf_g):
    b_p, t, d = x_prompt.shape
    bs, n_new, _ = x_sample.shape
    assert b_p == 1 and d == D_MODEL
    n_s = bs * n_new
    n_tok = t + n_s

    c_all = jnp.concatenate([c_prompt, c_sample], axis=0)
    m_pad = -(-c_all.shape[0] // 8) * 8
    c_all = jnp.pad(c_all, ((0, m_pad - c_all.shape[0]), (0, 0)))
    mod = _ada(c_all, w_ada, b_ada).reshape(m_pad, N_ADA, d)
    mod_p = [mod[0:1, i] for i in range(N_ADA)]
    mod_s = [jnp.repeat(mod[1:1 + bs, i], n_new, axis=0) for i in range(N_ADA)]

    w_main = w_in[:, :COL_GATE]
    w_gate = jnp.pad(w_in[:, COL_GATE:], ((0, 0), (0, LANES - N_GATE)))
    w_cat = jnp.concatenate([w_main, w_gate], axis=1).astype(BF16)
    row1 = lambda v: v.reshape(1, -1)

    xp = x_prompt.reshape(t, d)
    xs = x_sample.reshape(n_s, d)
    (qsb_p, sbrows_p, kvsb_p, qn_p, nsarows_p, nsab_p, winrows_p, winb_p, gates_p) = _project(
        xp, row1(norm1_g), mod_p[0], mod_p[1], w_cat)
    (qsb_s, sbrows_s, _, qn_s, nsarows_s, _, winrows_s, _, gates_s) = _project(
        xs, row1(norm1_g), mod_s[0], mod_s[1], w_cat)

    osb_p = _sb_prompt(qsb_p, kvsb_p)
    wpos = jnp.stack([jnp.repeat(cmp_pos_k, HEAD_DIM, axis=1), jnp.repeat(cmp_pos_v, HEAD_DIM, axis=1)])
    wc = jnp.stack([cmp_wk, cmp_wv]).astype(BF16)
    kvc_p = _compress_prompt(nsarows_p, wpos, wc)
    on_p = _nsa_prompt(qn_p, gates_p, kvc_p, nsab_p, winb_p)

    per_seq = lambda a: a.reshape(bs, n_new, a.shape[-1])
    n_phys = cache_sb.shape[0]
    w_buf = state_win.shape[1]
    osb_s = _sb_sample(page_table, per_seq(qsb_s), per_seq(sbrows_s), cache_sb.reshape(n_phys, -1, HEAD_DIM))
    on_s, win_s = _nsa_sample(page_table, per_seq(qn_s), per_seq(gates_s), per_seq(nsarows_s), per_seq(winrows_s),
                              cache_nsa.reshape(n_phys, -1, HEAD_DIM), state_win.reshape(bs, -1, HEAD_DIM), wpos, wc)

    w_out_b = w_out.astype(BF16)
    gout = out_norm_g.reshape(1, -1)
    bufs = _mixout(osb_p, on_p, xp, gout, w_out_b, mod_p[2], row1(norm2_g), mod_p[3], mod_p[4], n_tok, 0)
    x1, h2, h2g = _mixout(osb_s.reshape(n_s, SB_W), on_s.reshape(n_s, NSA_W), xs, gout, w_out_b,
                          mod_s[2], row1(norm2_g), mod_s[3], mod_s[4], n_tok, t, prev=bufs)

    eidx_t, w_t = _router(h2, w_router.T.astype(BF16), router_bias)
    dest, blk_e, n_active, n_blk = _dispatch_plan(eidx_t.T)
    dest_flat = dest.reshape(-1)
    x_sorted = _dispatch(dest_flat, h2g, n_blk * MOE_ROWS)
    y_sorted = _experts(blk_e, n_active, x_sorted, w_gate_e, w_up_e, w_down_e)
    w8 = w_t.T
    ws = (w_gate_s.astype(BF16), w_up_s.astype(BF16), w_down_s.astype(BF16))
    y_p = _final(dest_flat, x1, h2, w8, y_sorted, *ws, mod_p[5], row1(normf_g), mod_p[6], mod_p[7], t, 0)
    y_s = _final(dest_flat, x1, h2, w8, y_sorted, *ws, mod_s[5], row1(normf_g), mod_s[6], mod_s[7], n_s, t)

    keep = min(WINDOW, t)
    return (y_p.reshape(1, t, d), y_s.reshape(bs, n_new, d),
            sbrows_p.reshape(1, t, 2, N_SB_HEADS, HEAD_DIM), sbrows_s.reshape(bs, n_new, 2, N_SB_HEADS, HEAD_DIM),
            nsarows_p.reshape(1, t, 4, N_NSA_KV, HEAD_DIM), nsarows_s.reshape(bs, n_new, 4, N_NSA_KV, HEAD_DIM),
            winrows_p[t - keep:].reshape(1, keep, 2, N_NSA_KV, HEAD_DIM),
            win_s.reshape(bs, w_buf, 2, N_NSA_KV, HEAD_DIM))
```

```python
import functools

import numpy as np
import jax
import jax.numpy as jnp
from jax import lax
from jax.experimental import pallas as pl
from jax.experimental.pallas import tpu as pltpu

F32 = jnp.float32
BF16 = jnp.bfloat16
U32 = jnp.uint32

D_MODEL = 2048
HEAD_DIM = 128
N_SB_HEADS = 8
N_NSA_HEADS = 8
N_NSA_KV = 2
NSA_GROUP = N_NSA_HEADS // N_NSA_KV
CMP_BLOCK = 32
CMP_STRIDE = 16
SLC_BLOCK = 64
SLC_TOP = 16
WINDOW = 512
FORCE = 1e4
NEG = -1e30
N_EXPERTS = 256
TOP_K = 8
N_EXPERT_GROUPS = 8
TOPK_GROUPS = 4
D_EXPERT = D_MODEL // 4
ROUTED_SCALE = 2.5
MOE_ROWS = 128
N_ADA = 8
EPS = 1e-6
ATT_SCALE = HEAD_DIM ** -0.5

SB_W = N_SB_HEADS * HEAD_DIM
NSA_W = N_NSA_HEADS * HEAD_DIM
KV_W = N_NSA_KV * HEAD_DIM
COL_QSB = 0
COL_KVSB = SB_W
COL_QN = 3 * SB_W
COL_NSA = COL_QN + NSA_W
COL_WIN = COL_NSA + 4 * KV_W
COL_GATE = COL_WIN + 2 * KV_W
N_GATE = 3 * N_NSA_HEADS
LANES = 128
V7X_VMEM_LIMIT = 56 * 1024 * 1024
D_CHUNKS = D_MODEL // LANES
ROW_WORDS = D_CHUNKS // 2

NT = (((1,), (1,)), ((), ()))


def _cparams(sem):
    return pltpu.CompilerParams(dimension_semantics=sem, vmem_limit_bytes=V7X_VMEM_LIMIT)


def _sigmoid(x):
    return 1.0 / (1.0 + jnp.exp(-x))


def _split_bf16(x):
    hi = x.astype(BF16)
    lo = (x - hi.astype(F32)).astype(BF16)
    return hi, lo


def _dot(a, b):
    return jnp.dot(a, b, preferred_element_type=F32)


def _dot_nt(a, b):
    return lax.dot_general(a, b, NT, preferred_element_type=F32)


def _rms(x):
    return x * lax.rsqrt(jnp.mean(x * x, axis=-1, keepdims=True) + EPS)


def _masked_softmax(s, mask):
    s = jnp.where(mask, s, NEG)
    m = jnp.max(s, axis=-1, keepdims=True)
    e = jnp.where(mask, jnp.exp(s - m), 0.0)
    return e / jnp.maximum(jnp.sum(e, axis=-1, keepdims=True), 1e-30)


def _head_slope(head):
    out = jnp.full(head.shape, 2.0 ** -N_NSA_HEADS, F32)
    for i in range(N_NSA_HEADS - 1):
        out = jnp.where(head == i, 2.0 ** -(i + 1), out)
    return out


def _top_blocks(score, n_top):
    blk = lax.broadcasted_iota(jnp.int32, score.shape, 1).astype(F32)
    sel = jnp.zeros(score.shape, F32)
    for _ in range(n_top):
        m = jnp.max(score, axis=-1, keepdims=True)
        idx = jnp.min(jnp.where(score == m, blk, float(score.shape[1])), axis=-1, keepdims=True)
        pick = blk == idx
        sel = jnp.where(pick, 1.0, sel)
        score = jnp.where(pick, -2.0, score)
    return sel


def _top_blocks_by_rank(score, n_top, n_cand):
    blk = lax.broadcasted_iota(jnp.int32, score.shape, 1)
    rank = jnp.zeros(score.shape, F32)
    for i in range(n_cand):
        c = score[:, i:i + 1]
        tie = jnp.where(blk > i, 1.0, 0.0)
        rank = rank + jnp.where(c > score, 1.0, jnp.where(c == score, tie, 0.0))
    return jnp.where(rank < n_top, 1.0, 0.0)


def _slc_scores(imp, qpos):
    blk = lax.broadcasted_iota(jnp.int32, imp.shape, 1)
    qb = qpos // SLC_BLOCK
    forced = (blk == 0) | ((blk <= qb) & (blk >= qb - 1))
    return jnp.where(blk > qb, -1.0, jnp.where(forced, FORCE, imp))


def _gate_cols(gates, branch, group, n_rows_per_head):
    lane = lax.broadcasted_iota(jnp.int32, gates.shape, 1)
    cols = [jnp.sum(jnp.where(lane == branch * N_NSA_HEADS + group * NSA_GROUP + h, gates, 0.0),
                    axis=-1, keepdims=True) for h in range(NSA_GROUP)]
    return jnp.concatenate(cols, axis=0)


def _ada_kernel(c_ref, w_ref, b_ref, o_ref):
    c = c_ref[...]
    a = (c * _sigmoid(c)).astype(BF16)
    o_ref[...] = _dot(a, w_ref[...].astype(BF16)) + b_ref[...]


def _ada(c, w_ada, b_ada, tn=1024):
    m, d = c.shape
    n = w_ada.shape[1]
    return pl.pallas_call(
        _ada_kernel,
        grid=(n // tn,),
        in_specs=[pl.BlockSpec((m, d), lambda j: (0, 0)),
                  pl.BlockSpec((d, tn), lambda j: (0, j)),
                  pl.BlockSpec((1, tn), lambda j: (0, j))],
        out_specs=pl.BlockSpec((m, tn), lambda j: (0, j)),
        out_shape=jax.ShapeDtypeStruct((m, n), F32),
        compiler_params=_cparams(("arbitrary",)),
        name="ada",
    )(c, w_ada, b_ada.reshape(1, n))


def _proj_kernel(x_ref, g_ref, shift_ref, scale_ref, w_ref,
                 qsb_ref, sbrows_ref, kvsb_ref, qn_ref, nsarows_ref, nsab_ref,
                 winrows_ref, winb_ref, gates_ref):
    h = (_rms(x_ref[...]) * g_ref[...] * (1.0 + scale_ref[...]) + shift_ref[...]).astype(BF16)
    ch = 512

    def mm(c0, n):
        return _dot(h, w_ref[:, c0:c0 + n])

    for c in range(0, SB_W, ch):
        qsb_ref[:, c:c + ch] = mm(COL_QSB + c, ch).astype(BF16)
    for c in range(0, 2 * SB_W, ch):
        r = mm(COL_KVSB + c, ch)
        sbrows_ref[:, c:c + ch] = r
        kvsb_ref[:, c:c + ch] = r.astype(BF16)
    for c in range(0, NSA_W, ch):
        qn_ref[:, c:c + ch] = mm(COL_QN + c, ch).astype(BF16)
    for c in range(0, 4 * KV_W, ch):
        r = mm(COL_NSA + c, ch)
        nsarows_ref[:, c:c + ch] = r
        nsab_ref[:, c:c + ch] = r.astype(BF16)
    r = mm(COL_WIN, 2 * KV_W)
    winrows_ref[...] = r
    winb_ref[...] = r.astype(BF16)
    gates_ref[...] = _sigmoid(mm(COL_GATE, LANES))


def _project(x, g, shift, scale, w_cat, tm=256):
    m, d = x.shape
    row = lambda i: (i, 0)
    const = lambda i: (0, 0)
    mod_spec = lambda a: pl.BlockSpec((tm, d), row) if a.shape[0] == m else pl.BlockSpec((1, d), const)
    widths = [(SB_W, BF16), (2 * SB_W, F32), (2 * SB_W, BF16), (NSA_W, BF16), (4 * KV_W, F32),
              (4 * KV_W, BF16), (2 * KV_W, F32), (2 * KV_W, BF16), (LANES, F32)]
    return pl.pallas_call(
        _proj_kernel,
        grid=(m // tm,),
        in_specs=[pl.BlockSpec((tm, d), row),
                  pl.BlockSpec((1, d), const),
                  mod_spec(shift), mod_spec(scale),
                  pl.BlockSpec(w_cat.shape, const, pipeline_mode=pl.Buffered(1))],
        out_specs=[pl.BlockSpec((tm, w), row) for w, _ in widths],
        out_shape=[jax.ShapeDtypeStruct((m, w), dt) for w, dt in widths],
        compiler_params=_cparams(("arbitrary",)),
        name="project",
    )(x, g, shift, scale, w_cat)


def _sb_logits(z):
    sp = jnp.log1p(jnp.exp(-jnp.abs(z)))
    return jnp.minimum(z, 0.0) - sp, jnp.minimum(-z, 0.0) - sp


def _sb_prompt_kernel(q_ref, k_ref, v_ref, o_ref, *, tq, heads):
    i = pl.program_id(1)
    row = lax.broadcasted_iota(jnp.int32, (2 * tq, tq), 0)
    col = lax.broadcasted_iota(jnp.int32, (2 * tq, tq), 1)
    later2 = jnp.where(jnp.where(row >= tq, row - tq, row) > col, 1.0, 0.0).astype(BF16)
    causal = lax.broadcasted_iota(jnp.int32, (tq, tq), 1) < lax.broadcasted_iota(jnp.int32, (tq, tq), 0)

    hsl = [slice(hh * HEAD_DIM, (hh + 1) * HEAD_DIM) for hh in range(heads)]

    def block(j, state, diagonal):
        off = pl.multiple_of(j * tq, tq)
        zs = [_dot_nt(q_ref[:, hs], k_ref[pl.ds(off, tq), hs]) * ATT_SCALE for hs in hsl]
        logits = [_sb_logits(z) for z in zs]
        if diagonal:
            logits = [(ls, jnp.where(causal, l1m, 0.0)) for ls, l1m in logits]
        split = [jnp.concatenate(_split_bf16(l1m), axis=1) for _, l1m in logits]
        suffix = [_dot(sp, later2) + state[hh][0] for hh, sp in enumerate(split)]
        a = [jnp.exp(logits[hh][0] + suffix[hh]) for hh in range(heads)]
        if diagonal:
            a = [jnp.where(causal, x, 0.0) for x in a]
        acc = [state[hh][1] + _dot(a[hh].astype(BF16), v_ref[pl.ds(off, tq), hsl[hh]]) for hh in range(heads)]
        return tuple((suffix[hh][:, :1] + logits[hh][1][:, :1], acc[hh]) for hh in range(heads))

    init = tuple((jnp.zeros((tq, 1), F32), jnp.zeros((tq, HEAD_DIM), F32)) for _ in range(heads))
    state = block(i, init, True)
    state = lax.fori_loop(0, i, lambda t, st: block(i - 1 - t, st, False), state)
    for hh in range(heads):
        o_ref[:, hh * HEAD_DIM:(hh + 1) * HEAD_DIM] = state[hh][1]


def _sb_prompt(qsb, kvsb, tq=256, heads=4):
    t = qsb.shape[0]
    w = heads * HEAD_DIM
    n_hp = N_SB_HEADS // heads
    return pl.pallas_call(
        functools.partial(_sb_prompt_kernel, tq=tq, heads=heads),
        grid=(n_hp, t // tq),
        in_specs=[pl.BlockSpec((tq, w), lambda h, i: (i, h)),
                  pl.BlockSpec((t, w), lambda h, i: (0, h)),
                  pl.BlockSpec((t, w), lambda h, i: (0, n_hp + h))],
        out_specs=pl.BlockSpec((tq, w), lambda h, i: (i, h)),
        out_shape=jax.ShapeDtypeStruct((t, SB_W), F32),
        compiler_params=_cparams(("arbitrary", "arbitrary")),
        name="sb_prompt",
    )(qsb, kvsb, kvsb)


def _sb_sample_kernel(pt_ref, q_ref, new_ref, *refs, n_new, ppg):
    cache_refs = refs[:ppg]
    o_ref, qbd_ref, carry_ref, acc_ref = refs[ppg:]
    p = pl.program_id(1)
    rows = N_SB_HEADS * n_new
    kk = lax.broadcasted_iota(jnp.int32, (2 * LANES, 2 * LANES), 0)
    cc = lax.broadcasted_iota(jnp.int32, (2 * LANES, 2 * LANES), 1)
    kk = jnp.where(kk >= LANES, kk - LANES, kk)
    later_and_all = jnp.where((cc >= LANES) | (kk > cc), 1.0, 0.0).astype(BF16)
    stride = 2 * N_SB_HEADS
    hsl = [slice(h * HEAD_DIM, (h + 1) * HEAD_DIM) for h in range(N_SB_HEADS)]
    rsl = [slice(h * n_new, (h + 1) * n_new) for h in range(N_SB_HEADS)]

    def scores(kmat, diagonal):
        z = _dot_nt(qbd_ref[...].astype(BF16), kmat) * ATT_SCALE
        ls, l1m = _sb_logits(z)
        causal = None
        if diagonal:
            key = lax.broadcasted_iota(jnp.int32, (rows, LANES), 1)
            qry = lax.broadcasted_iota(jnp.int32, (rows, LANES), 0) % n_new
            causal = key < qry
            l1m = jnp.where(causal, l1m, 0.0)
        hi, lo = _split_bf16(l1m)
        return ls, _dot(jnp.concatenate([hi, lo], axis=1), later_and_all), causal

    def accumulate(ls, st, causal, vmat, carry):
        a = jnp.exp(ls + st[:, :LANES] + carry)
        if causal is not None:
            a = jnp.where(causal, a, 0.0)
        o_all = _dot(a.astype(BF16), vmat)
        for h in range(N_SB_HEADS):
            acc_ref[rsl[h], :] += o_all[rsl[h], hsl[h]]
        return carry + st[:, LANES:]

    @pl.when(p == 0)
    def _():
        acc_ref[...] = jnp.zeros_like(acc_ref)
        qbd_ref[...] = jnp.zeros_like(qbd_ref)
        for h in range(N_SB_HEADS):
            qbd_ref[rsl[h], hsl[h]] = q_ref[0, :, hsl[h]].astype(F32)
        pad = jnp.zeros((LANES - n_new, SB_W), F32)
        knew = jnp.concatenate([new_ref[0, :, :SB_W], pad], axis=0).astype(BF16)
        vnew = jnp.concatenate([new_ref[0, :, SB_W:], pad], axis=0).astype(BF16)
        ls, st, causal = scores(knew, True)
        carry_ref[...] = accumulate(ls, st, causal, vnew, jnp.zeros((rows, LANES), F32))

    def page_mat(ref, c0):
        return jnp.concatenate([ref[0, pl.ds(c0 + h, LANES, stride=stride), :].astype(BF16)
                                for h in range(N_SB_HEADS)], axis=1)

    pre = [scores(page_mat(ref, 0), False) for ref in cache_refs]
    carry = carry_ref[...]
    for ref, (ls, st, _) in zip(cache_refs, pre):
        carry = accumulate(ls, st, None, page_mat(ref, N_SB_HEADS), carry)
    carry_ref[...] = carry

    @pl.when(p == pl.num_programs(1) - 1)
    def _():
        for h in range(N_SB_HEADS):
            o_ref[0, :, hsl[h]] = acc_ref[rsl[h], :]


def _sb_sample(page_table, qsb, sbrows, cache, ppg=8):
    bs, n_new, _ = qsb.shape
    n_pages = page_table.shape[1]
    page_rows = cache.shape[1]
    assert page_rows == LANES * 2 * N_SB_HEADS and n_pages % ppg == 0
    rows = N_SB_HEADS * n_new

    def page_map(q):
        return lambda b, p, pt: (pt[b * n_pages + n_pages - 1 - (p * ppg + q)], 0, 0)

    grid_spec = pltpu.PrefetchScalarGridSpec(
        num_scalar_prefetch=1,
        grid=(bs, n_pages // ppg),
        in_specs=[pl.BlockSpec((1, n_new, SB_W), lambda b, p, pt: (b, 0, 0)),
                  pl.BlockSpec((1, n_new, 2 * SB_W), lambda b, p, pt: (b, 0, 0))]
                 + [pl.BlockSpec((1, page_rows, HEAD_DIM), page_map(q)) for q in range(ppg)],
        out_specs=pl.BlockSpec((1, n_new, SB_W), lambda b, p, pt: (b, 0, 0)),
        scratch_shapes=[pltpu.VMEM((rows, SB_W), F32), pltpu.VMEM((rows, LANES), F32),
                        pltpu.VMEM((rows, HEAD_DIM), F32)],
    )
    return pl.pallas_call(
        functools.partial(_sb_sample_kernel, n_new=n_new, ppg=ppg),
        grid_spec=grid_spec,
        out_shape=jax.ShapeDtypeStruct((bs, n_new, SB_W), F32),
        compiler_params=_cparams(("arbitrary", "arbitrary")),
        name="sb_sample",
    )(page_table.reshape(-1), qsb, sbrows, *([cache] * ppg))


def _pool_chunks(x, w1, w2):
    x3 = x.reshape(x.shape[0] // CMP_STRIDE, CMP_STRIDE, x.shape[1])
    return jnp.sum(x3 * w1[None], axis=1), jnp.sum(x3 * w2[None], axis=1)


def _compress_kernel(rows_ref, wpos_ref, wc_ref, o_ref, first_ref, second_ref, *, step_rows):
    n_rows = rows_ref.shape[0]
    n_chunks = n_rows // CMP_STRIDE
    w1 = wpos_ref[0, :CMP_STRIDE, :]
    w2 = wpos_ref[0, CMP_STRIDE:, :]
    cps = step_rows // CMP_STRIDE

    def body(s, carry):
        r0 = pl.multiple_of(s * step_rows, step_rows)
        c0 = pl.multiple_of(s * cps, cps)
        f, sec = _pool_chunks(rows_ref[pl.ds(r0, step_rows), :], w1, w2)
        first_ref[pl.ds(c0, cps), :] = f
        second_ref[pl.ds(c0, cps), :] = sec
        return carry

    lax.fori_loop(0, n_rows // step_rows, body, 0)
    pre = first_ref[...] + pltpu.roll(second_ref[...], n_chunks - 1, axis=0)
    for g in range(N_NSA_KV):
        sl = slice(g * HEAD_DIM, (g + 1) * HEAD_DIM)
        o_ref[0, :, sl] = _dot(pre[:, sl].astype(BF16), wc_ref[0, g]).astype(BF16)


def _compress_prompt(nsarows, wpos, wc, step_rows=1024):
    t = nsarows.shape[0]
    n_chunks = t // CMP_STRIDE
    return pl.pallas_call(
        functools.partial(_compress_kernel, step_rows=step_rows),
        grid=(2,),
        in_specs=[pl.BlockSpec((t, KV_W), lambda c: (0, c)),
                  pl.BlockSpec((1, CMP_BLOCK, KV_W), lambda c: (c, 0, 0)),
                  pl.BlockSpec((1, N_NSA_KV, HEAD_DIM, HEAD_DIM), lambda c: (c, 0, 0, 0))],
        out_specs=pl.BlockSpec((1, n_chunks, KV_W), lambda c: (c, 0, 0)),
        out_shape=jax.ShapeDtypeStruct((2, n_chunks, KV_W), BF16),
        scratch_shapes=[pltpu.VMEM((n_chunks, KV_W), F32), pltpu.VMEM((n_chunks, KV_W), F32)],
        compiler_params=_cparams(("arbitrary",)),
        name="compress_prompt",
    )(nsarows, wpos, wc)


def _importance_matrix(n_cmp, n_slc_lanes):
    m = np.arange(n_cmp)[:, None]
    j = np.arange(n_slc_lanes)[None, :]
    d = m - (SLC_BLOCK // CMP_STRIDE) * j
    w = np.where((d >= 0) & (d <= 2), 1.0, np.where((d == -1) | (d == 3), 0.5, 0.0))
    return jnp.asarray(w, BF16)


def _block_expansion(n_keys):
    b = np.arange(LANES)[:, None]
    k = np.arange(n_keys)[None, :]
    return jnp.asarray((b == k // SLC_BLOCK).astype(np.float32), BF16)


def _nsa_prompt_kernel(q_ref, gate_ref, kc_ref, vc_ref, sk_ref, sv_ref, wk_ref, wv_ref,
                       e_ref, mimp_ref, o_ref, *, tq, tk):
    i = pl.program_id(0)
    start = i * tq
    rows = NSA_GROUP * tq
    groups = range(N_NSA_KV)
    r = lax.broadcasted_iota(jnp.int32, (rows, 1), 0)
    qpos = start + r % tq
    qpos_q = start + lax.broadcasted_iota(jnp.int32, (tq, 1), 0)
    n_cmp = kc_ref.shape[1]
    kc_end = CMP_STRIDE * lax.broadcasted_iota(jnp.int32, (1, n_cmp), 1) + (CMP_BLOCK - 1)
    gsl = [slice(g * HEAD_DIM, (g + 1) * HEAD_DIM) for g in groups]

    qg, slope, o_cmp, sel = [], [], [], []
    for g in groups:
        qg.append(jnp.concatenate(
            [q_ref[:, (g * NSA_GROUP + h) * HEAD_DIM:(g * NSA_GROUP + h + 1) * HEAD_DIM] for h in range(NSA_GROUP)],
            axis=0))
        slope.append(_head_slope(g * NSA_GROUP + r // tq))
        s = _dot_nt(qg[g], kc_ref[0, :, gsl[g]]) * ATT_SCALE - slope[g] * (qpos - kc_end).astype(F32)
        p = _masked_softmax(s, kc_end <= qpos)
        o_cmp.append(_dot(p.astype(BF16), vc_ref[0, :, gsl[g]]))
        p4 = p[0:tq] + p[tq:2 * tq] + p[2 * tq:3 * tq] + p[3 * tq:4 * tq]
        hi, lo = _split_bf16(p4)
        imp = _dot(hi, mimp_ref[...]) + _dot(lo, mimp_ref[...])
        sel.append(_top_blocks(_slc_scores(imp, qpos_q), SLC_TOP))

    lane_f = lax.broadcasted_iota(jnp.int32, (1, tk), 1).astype(F32)
    lane_bias = [slope[g] * lane_f for g in groups]
    row_bias = [slope[g] * qpos.astype(F32) for g in groups]
    drop = 2.0 * NEG
    sel_bias = [((sel[g] - 1.0) * (-drop)).astype(BF16) for g in groups]

    def attend(k_ref, v_ref, t_lo, t_hi, selected):
        def body(t, state, edge):
            off = pl.multiple_of(t * tk, tk)
            z = [_dot_nt(qg[g], k_ref[pl.ds(off, tk), gsl[g]]) for g in groups]
            s = [z[g] * ATT_SCALE + lane_bias[g] for g in groups]
            if selected:
                mb = [_dot(sel_bias[g], e_ref[t]) for g in groups]
                s = [(s[g].reshape(NSA_GROUP, tq, tk) + mb[g][None]).reshape(rows, tk) for g in groups]
            if edge:
                dist = qpos - (off + lax.broadcasted_iota(jnp.int32, (1, tk), 1))
                ok = (dist >= 0) if selected else ((dist >= 0) & (dist < WINDOW))
                s = [jnp.where(ok, s[g], drop) for g in groups]
            shift = [row_bias[g] - slope[g] * off.astype(F32) for g in groups]
            m_new = [jnp.maximum(state[g][0], jnp.max(s[g], axis=-1, keepdims=True) - shift[g]) for g in groups]
            e = [jnp.exp(s[g] - (shift[g] + m_new[g])) for g in groups]
            pv = [_dot(e[g].astype(BF16), v_ref[pl.ds(off, tk), gsl[g]]) for g in groups]
            out = []
            for g in groups:
                alpha = jnp.exp(state[g][0] - m_new[g])
                out.append((m_new[g], state[g][1] * alpha + jnp.sum(e[g], axis=-1, keepdims=True),
                            state[g][2] * alpha + pv[g]))
            return tuple(out)

        init = tuple((jnp.full((rows, 1), NEG, F32), jnp.zeros((rows, 1), F32), jnp.zeros((rows, HEAD_DIM), F32))
                     for _ in groups)
        if selected:
            state = lax.fori_loop(t_lo, t_hi - 1, lambda t, st: body(t, st, False), init)
            state = body(t_hi - 1, state, True)
        else:
            state = lax.fori_loop(t_lo, t_hi, lambda t, st: body(t, st, True), init)
        return [state[g][2] / jnp.maximum(state[g][1], 1e-30) for g in groups]

    last = (start + tq - 1) // tk
    o_slc = attend(sk_ref, sv_ref, 0, last + 1, True)
    o_win = attend(wk_ref, wv_ref, jnp.maximum(start - WINDOW, 0) // tk, last + 1, False)

    gates = gate_ref[...]
    for g in groups:
        o = (_gate_cols(gates, 0, g, tq) * o_cmp[g] + _gate_cols(gates, 1, g, tq) * o_slc[g]
             + _gate_cols(gates, 2, g, tq) * o_win[g])
        for h in range(NSA_GROUP):
            c = (g * NSA_GROUP + h) * HEAD_DIM
            o_ref[:, c:c + HEAD_DIM] = o[h * tq:(h + 1) * tq]


def _nsa_prompt(qn, gates, kvc, nsab, winb, tq=128, tk=256):
    t = qn.shape[0]
    n_cmp = kvc.shape[1]
    e3 = _block_expansion(t).reshape(LANES, t // tk, tk).transpose(1, 0, 2)
    mimp = _importance_matrix(n_cmp, LANES)
    once = pl.Buffered(1)
    col = lambda c: pl.BlockSpec((t, KV_W), lambda i: (0, c), pipeline_mode=once)
    return pl.pallas_call(
        functools.partial(_nsa_prompt_kernel, tq=tq, tk=tk),
        grid=(t // tq,),
        in_specs=[pl.BlockSpec((tq, NSA_W), lambda i: (i, 0)),
                  pl.BlockSpec((tq, LANES), lambda i: (i, 0)),
                  pl.BlockSpec((1, n_cmp, KV_W), lambda i: (0, 0, 0), pipeline_mode=once),
                  pl.BlockSpec((1, n_cmp, KV_W), lambda i: (1, 0, 0), pipeline_mode=once),
                  col(2), col(3), col(0), col(1),
                  pl.BlockSpec(e3.shape, lambda i: (0, 0, 0), pipeline_mode=once),
                  pl.BlockSpec(mimp.shape, lambda i: (0, 0), pipeline_mode=once)],
        out_specs=pl.BlockSpec((tq, NSA_W), lambda i: (i, 0)),
        out_shape=jax.ShapeDtypeStruct((t, NSA_W), F32),
        compiler_params=_cparams(("arbitrary",)),
        name="nsa_prompt",
    )(qn, gates, kvc, kvc, nsab, nsab, winb, winb, e3, mimp)


def _nsa_sample_kernel(pt_ref, q_ref, gate_ref, newrows_ref, newwin_ref, win_ref, wpos_ref, wc_ref,
                       e_ref, mimp_ref, *refs, n_new, n_pages, ppg):
    cache_refs = refs[:ppg]
    o_ref, winout_ref, first_ref, second_ref, ks_ref, vs_ref = refs[ppg:]
    p = pl.program_id(1)
    page = LANES
    past = n_pages * page
    cpp = page // CMP_STRIDE
    n_cmp = first_ref.shape[1]
    n_keys = ks_ref.shape[0]
    w_buf = win_ref.shape[1] // (2 * N_NSA_KV)
    gsl = [slice(g * HEAD_DIM, (g + 1) * HEAD_DIM) for g in range(N_NSA_KV)]
    kinds = 4 * N_NSA_KV

    @pl.when(p == 0)
    def _():
        first_ref[...] = jnp.zeros_like(first_ref)
        second_ref[...] = jnp.zeros_like(second_ref)

    for q, cache_ref in enumerate(cache_refs):
        pg = p * ppg + q
        c0 = pl.multiple_of(pg * cpp, cpp)
        r0 = pl.multiple_of(pg * page, page)
        for g in range(N_NSA_KV):
            for kv in range(2):
                x = cache_ref[0, pl.ds(kv * N_NSA_KV + g, page, stride=kinds), :]
                f, sec = _pool_chunks(x, wpos_ref[kv, :CMP_STRIDE, gsl[g]], wpos_ref[kv, CMP_STRIDE:, gsl[g]])
                first_ref[kv, pl.ds(c0, cpp), gsl[g]] = f
                second_ref[kv, pl.ds(c0, cpp), gsl[g]] = sec
            ks_ref[pl.ds(r0, page), gsl[g]] = cache_ref[0, pl.ds(2 * N_NSA_KV + g, page, stride=kinds), :].astype(BF16)
            vs_ref[pl.ds(r0, page), gsl[g]] = cache_ref[0, pl.ds(3 * N_NSA_KV + g, page, stride=kinds), :].astype(BF16)

    @pl.when(p == pl.num_programs(1) - 1)
    def _():
        new = newrows_ref[0]
        zpad = jnp.zeros((CMP_STRIDE - n_new, KV_W), F32)
        kvc = []
        for kv in range(2):
            xc = jnp.concatenate([new[:, kv * KV_W:(kv + 1) * KV_W], zpad], axis=0)
            first_ref[kv, past // CMP_STRIDE:past // CMP_STRIDE + 1, :] = jnp.sum(
                xc * wpos_ref[kv, :CMP_STRIDE, :], axis=0, keepdims=True)
            second_ref[kv, past // CMP_STRIDE:past // CMP_STRIDE + 1, :] = jnp.sum(
                xc * wpos_ref[kv, CMP_STRIDE:, :], axis=0, keepdims=True)
            pre = first_ref[kv] + pltpu.roll(second_ref[kv], n_cmp - 1, axis=0)
            kvc.append([_dot(pre[:, gsl[g]].astype(BF16), wc_ref[kv, g]).astype(BF16) for g in range(N_NSA_KV)])
        tail = jnp.zeros((n_keys - past - n_new, KV_W), F32)
        ks_ref[past:, :] = jnp.concatenate([new[:, 2 * KV_W:3 * KV_W], tail], axis=0).astype(BF16)
        vs_ref[past:, :] = jnp.concatenate([new[:, 3 * KV_W:4 * KV_W], tail], axis=0).astype(BF16)

        nwin = newwin_ref[0]
        wr = 2 * N_NSA_KV
        winout_ref[0, :(w_buf - n_new) * wr, :] = win_ref[0, n_new * wr:, :]
        for c in range(wr):
            winout_ref[0, pl.ds((w_buf - n_new) * wr + c, n_new, stride=wr), :] = nwin[:, c * HEAD_DIM:(c + 1) * HEAD_DIM]
        wtail = jnp.zeros((LANES - n_new, HEAD_DIM), F32)

        def window(c):
            return jnp.concatenate([win_ref[0, pl.ds(c, w_buf, stride=wr), :],
                                    nwin[:, c * HEAD_DIM:(c + 1) * HEAD_DIM], wtail], axis=0).astype(BF16)

        n_win = w_buf + LANES
        rows = NSA_GROUP * n_new
        r = lax.broadcasted_iota(jnp.int32, (rows, 1), 0)
        qpos = past + r % n_new
        qpos_q = past + lax.broadcasted_iota(jnp.int32, (n_new, 1), 0)
        kc_end = CMP_STRIDE * lax.broadcasted_iota(jnp.int32, (1, n_cmp), 1) + (CMP_BLOCK - 1)
        dist_s = qpos - lax.broadcasted_iota(jnp.int32, (1, n_keys), 1)
        dist_w = qpos - (past - w_buf + lax.broadcasted_iota(jnp.int32, (1, n_win), 1))
        gates = gate_ref[0]

        qgs, slopes, o_cmps, imps = [], [], [], []
        for g in range(N_NSA_KV):
            qg = jnp.concatenate(
                [q_ref[0, :, (g * NSA_GROUP + h) * HEAD_DIM:(g * NSA_GROUP + h + 1) * HEAD_DIM]
                 for h in range(NSA_GROUP)], axis=0)
            slope = _head_slope(g * NSA_GROUP + r // n_new)
            s = _dot_nt(qg, kvc[0][g]) * ATT_SCALE - slope * (qpos - kc_end).astype(F32)
            pc = _masked_softmax(s, kc_end <= qpos)
            o_cmps.append(_dot(pc.astype(BF16), kvc[1][g]))
            p4 = pc[0:n_new] + pc[n_new:2 * n_new] + pc[2 * n_new:3 * n_new] + pc[3 * n_new:4 * n_new]
            hi, lo = _split_bf16(p4)
            imps.append(_dot(hi, mimp_ref[...]) + _dot(lo, mimp_ref[...]))
            qgs.append(qg)
            slopes.append(slope)
        n_slc = -(-(past + n_new) // SLC_BLOCK)
        sel_all = _top_blocks_by_rank(
            _slc_scores(jnp.concatenate(imps, axis=0), jnp.concatenate([qpos_q] * N_NSA_KV, axis=0)), SLC_TOP, n_slc)

        for g in range(N_NSA_KV):
            qg, slope, o_cmp = qgs[g], slopes[g], o_cmps[g]
            sel = sel_all[g * n_new:(g + 1) * n_new]
            keep = _dot(jnp.concatenate([sel] * NSA_GROUP, axis=0).astype(BF16), e_ref[...]) > 0.5
            s = _dot_nt(qg, ks_ref[:, gsl[g]]) * ATT_SCALE - slope * dist_s.astype(F32)
            ps = _masked_softmax(s, (dist_s >= 0) & keep)
            o_slc = _dot(ps.astype(BF16), vs_ref[:, gsl[g]])
            s = _dot_nt(qg, window(g)) * ATT_SCALE - slope * dist_w.astype(F32)
            pw = _masked_softmax(s, (dist_w >= 0) & (dist_w < WINDOW))
            o_win = _dot(pw.astype(BF16), window(N_NSA_KV + g))

            o = (_gate_cols(gates, 0, g, n_new) * o_cmp + _gate_cols(gates, 1, g, n_new) * o_slc
                 + _gate_cols(gates, 2, g, n_new) * o_win)
            for h in range(NSA_GROUP):
                c = (g * NSA_GROUP + h) * HEAD_DIM
                o_ref[0, :, c:c + HEAD_DIM] = o[h * n_new:(h + 1) * n_new]


def _nsa_sample(page_table, qn, gates, nsarows, winrows, cache, state_win, wpos, wc, ppg=4):
    bs, n_new, _ = qn.shape
    n_pages = page_table.shape[1]
    page_rows = cache.shape[1]
    assert page_rows == LANES * 4 * N_NSA_KV and n_pages % ppg == 0
    past = n_pages * LANES
    win_rows = state_win.shape[1]
    n_keys = past + LANES
    n_cmp = 2 * (past // CMP_STRIDE)
    e = _block_expansion(n_keys)
    mimp = _importance_matrix(n_cmp, LANES)
    seq = lambda b, p, pt: (b, 0, 0)
    c3 = lambda b, p, pt: (0, 0, 0)

    def page_map(q):
        return lambda b, p, pt: (pt[b * n_pages + p * ppg + q], 0, 0)

    grid_spec = pltpu.PrefetchScalarGridSpec(
        num_scalar_prefetch=1,
        grid=(bs, n_pages // ppg),
        in_specs=[pl.BlockSpec((1, n_new, NSA_W), seq),
                  pl.BlockSpec((1, n_new, LANES), seq),
                  pl.BlockSpec((1, n_new, 4 * KV_W), seq),
                  pl.BlockSpec((1, n_new, 2 * KV_W), seq),
                  pl.BlockSpec((1, win_rows, HEAD_DIM), seq),
                  pl.BlockSpec(wpos.shape, c3),
                  pl.BlockSpec(wc.shape, lambda b, p, pt: (0, 0, 0, 0)),
                  pl.BlockSpec(e.shape, lambda b, p, pt: (0, 0)),
                  pl.BlockSpec(mimp.shape, lambda b, p, pt: (0, 0))]
                 + [pl.BlockSpec((1, page_rows, HEAD_DIM), page_map(q)) for q in range(ppg)],
        out_specs=[pl.BlockSpec((1, n_new, NSA_W), seq),
                   pl.BlockSpec((1, win_rows, HEAD_DIM), seq)],
        scratch_shapes=[pltpu.VMEM((2, n_cmp, KV_W), F32), pltpu.VMEM((2, n_cmp, KV_W), F32),
                        pltpu.VMEM((n_keys, KV_W), BF16), pltpu.VMEM((n_keys, KV_W), BF16)],
    )
    return pl.pallas_call(
        functools.partial(_nsa_sample_kernel, n_new=n_new, n_pages=n_pages, ppg=ppg),
        grid_spec=grid_spec,
        out_shape=[jax.ShapeDtypeStruct((bs, n_new, NSA_W), F32),
                   jax.ShapeDtypeStruct((bs, win_rows, HEAD_DIM), F32)],
        compiler_params=_cparams(("arbitrary", "arbitrary")),
        name="nsa_sample",
    )(page_table.reshape(-1), qn, gates, nsarows, winrows, state_win, wpos, wc, e, mimp, *([cache] * ppg))


def _mixout_kernel(osb_ref, on_ref, x_ref, gout_ref, w_ref, gate_ref, g2_ref, shift_ref, scale_ref, *rest):
    x1_ref, h2_ref, h2g_ref, nrm_ref = rest[-4:]
    tm = x_ref.shape[0]
    for h in range(N_SB_HEADS + N_NSA_HEADS):
        src = osb_ref if h < N_SB_HEADS else on_ref
        c = (h % N_SB_HEADS) * HEAD_DIM
        sl = slice(h * HEAD_DIM, (h + 1) * HEAD_DIM)
        nrm_ref[:, sl] = (_rms(src[:, c:c + HEAD_DIM]) * gout_ref[:, sl]).astype(BF16)
    x1 = x_ref[...] + gate_ref[...] * _dot(nrm_ref[...], w_ref[...])
    x1_ref[...] = x1
    h2 = _rms(x1) * g2_ref[...] * (1.0 + scale_ref[...]) + shift_ref[...]
    h2_ref[...] = h2.astype(BF16)
    for s in range(ROW_WORDS):
        lo = h2[:, (2 * s) * LANES:(2 * s + 1) * LANES].astype(BF16).astype(F32)
        hi = h2[:, (2 * s + 1) * LANES:(2 * s + 2) * LANES].astype(BF16).astype(F32)
        word = (lax.bitcast_convert_type(lo, U32) >> 16) | lax.bitcast_convert_type(hi, U32)
        h2g_ref[pl.ds(s, tm, stride=ROW_WORDS), :] = word


def _mixout(osb, on, x, gout, w_out, gate, g2, shift, scale, n_total, row0, prev=None, tm=256):
    m, d = x.shape
    blk0 = row0 // tm
    row = lambda i: (i, 0)
    out_row = lambda i: (blk0 + i, 0)
    const = lambda i: (0, 0)
    mod_spec = lambda a: pl.BlockSpec((tm, d), row) if a.shape[0] == m else pl.BlockSpec((1, d), const)
    in_specs = [pl.BlockSpec((tm, SB_W), row), pl.BlockSpec((tm, NSA_W), row), pl.BlockSpec((tm, d), row),
                pl.BlockSpec((1, d), const),
                pl.BlockSpec(w_out.shape, const, pipeline_mode=pl.Buffered(1)),
                mod_spec(gate), pl.BlockSpec((1, d), const), mod_spec(shift), mod_spec(scale)]
    args = [osb, on, x, gout, w_out, gate, g2, shift, scale]
    aliases = {}
    if prev is not None:
        in_specs += [pl.BlockSpec(memory_space=pl.ANY)] * 3
        aliases = {len(args) + j: j for j in range(3)}
        args += list(prev)
    return pl.pallas_call(
        _mixout_kernel,
        grid=(m // tm,),
        in_specs=in_specs,
        out_specs=[pl.BlockSpec((tm, d), out_row), pl.BlockSpec((tm, d), out_row),
                   pl.BlockSpec((tm * ROW_WORDS, LANES), out_row)],
        out_shape=[jax.ShapeDtypeStruct((n_total, d), F32), jax.ShapeDtypeStruct((n_total, d), BF16),
                   jax.ShapeDtypeStruct((n_total * ROW_WORDS, LANES), U32)],
        scratch_shapes=[pltpu.VMEM((tm, d), BF16)],
        input_output_aliases=aliases,
        compiler_params=_cparams(("arbitrary",)),
        name="mixout",
    )(*args)


def _first_max(v, idx, axis, n):
    m = jnp.max(v, axis=axis, keepdims=True)
    return m, jnp.min(jnp.where(v == m, idx, float(n)), axis=axis, keepdims=True)


def _router_kernel(h_ref, wrt_ref, bias_ref, eidx_ref, w_ref):
    tm = h_ref.shape[0]
    per = N_EXPERTS // N_EXPERT_GROUPS
    s = _sigmoid(_dot_nt(wrt_ref[...], h_ref[...]))
    biased = s + bias_ref[...]
    b3 = biased.reshape(N_EXPERT_GROUPS, per, tm)
    r3 = lax.broadcasted_iota(jnp.int32, b3.shape, 1).astype(F32)
    m1, i1 = _first_max(b3, r3, 1, per)
    m2 = jnp.max(jnp.where(r3 == i1, -jnp.inf, b3), axis=1, keepdims=True)
    gscore = (m1 + m2).reshape(N_EXPERT_GROUPS, tm)
    gi = lax.broadcasted_iota(jnp.int32, gscore.shape, 0).astype(F32)
    gmask = jnp.zeros(gscore.shape, F32)
    for _ in range(TOPK_GROUPS):
        _, idx = _first_max(gscore, gi, 0, N_EXPERT_GROUPS)
        pick = gi == idx
        gmask = jnp.where(pick, 1.0, gmask)
        gscore = jnp.where(pick, -jnp.inf, gscore)
    cand = jnp.where(gmask.reshape(N_EXPERT_GROUPS, 1, tm) > 0.5, b3, -jnp.inf).reshape(N_EXPERTS, tm)
    ei = lax.broadcasted_iota(jnp.int32, cand.shape, 0).astype(F32)
    ids, ws = [], []
    for _ in range(TOP_K):
        _, idx = _first_max(cand, ei, 0, N_EXPERTS)
        pick = ei == idx
        ids.append(idx)
        ws.append(jnp.sum(jnp.where(pick, s, 0.0), axis=0, keepdims=True))
        cand = jnp.where(pick, -jnp.inf, cand)
    w = jnp.concatenate(ws, axis=0)
    eidx_ref[...] = jnp.concatenate(ids, axis=0).astype(jnp.int32)
    w_ref[...] = w / jnp.sum(w, axis=0, keepdims=True) * ROUTED_SCALE


def _router(h2, w_router_t, router_bias, tm=512):
    m, d = h2.shape
    return pl.pallas_call(
        _router_kernel,
        grid=(m // tm,),
        in_specs=[pl.BlockSpec((tm, d), lambda i: (i, 0)),
                  pl.BlockSpec((N_EXPERTS, d), lambda i: (0, 0)),
                  pl.BlockSpec((N_EXPERTS, 1), lambda i: (0, 0))],
        out_specs=[pl.BlockSpec((TOP_K, tm), lambda i: (0, i)), pl.BlockSpec((TOP_K, tm), lambda i: (0, i))],
        out_shape=[jax.ShapeDtypeStruct((TOP_K, m), jnp.int32), jax.ShapeDtypeStruct((TOP_K, m), F32)],
        compiler_params=_cparams(("arbitrary",)),
        name="router",
    )(h2, w_router_t, router_bias.reshape(N_EXPERTS, 1))


def _dispatch_plan(eidx):
    t = eidx.shape[0]
    onehot = eidx[:, :, None] == jnp.arange(N_EXPERTS, dtype=jnp.int32)[None, None, :]
    routed = jnp.any(onehot, axis=1).astype(jnp.int32)
    incl = jnp.cumsum(routed, axis=0)
    counts = incl[-1]
    padded = (counts + MOE_ROWS - 1) // MOE_ROWS * MOE_ROWS
    pad_end = jnp.cumsum(padded)
    base = (pad_end - padded)[None, :] + incl - routed
    dest = jnp.sum(jnp.where(onehot, base[:, None, :], 0), axis=-1).astype(jnp.int32)
    n_blk = -(-(t * TOP_K) // MOE_ROWS) + N_EXPERTS
    blk_e = jnp.minimum(jnp.searchsorted(pad_end, jnp.arange(n_blk) * MOE_ROWS, side='right'),
                        N_EXPERTS - 1).astype(jnp.int32)
    n_active = (pad_end[-1] // MOE_ROWS).astype(jnp.int32)
    nxt_blk = (pad_end[blk_e] // MOE_ROWS).astype(jnp.int32)
    nxt_e = jnp.where(nxt_blk < n_active, blk_e[jnp.minimum(nxt_blk, n_blk - 1)], -1).astype(jnp.int32)
    return dest, blk_e, nxt_e, n_active.reshape(1), n_blk


def _dispatch_kernel(dest_ref, h2g_ref, xs_ref, sem, *, tm):
    i = pl.program_id(0)

    def body(r, carry):
        src = h2g_ref.at[pl.ds(pl.multiple_of(r * ROW_WORDS, ROW_WORDS), ROW_WORDS), :]
        for k in range(TOP_K):
            d = dest_ref[(i * tm + r) * TOP_K + k]
            pltpu.make_async_copy(src, xs_ref.at[pl.ds(pl.multiple_of(d * ROW_WORDS, ROW_WORDS), ROW_WORDS), :],
                                  sem).start()
        return carry

    lax.fori_loop(0, tm, body, 0)
    done = xs_ref.at[pl.ds(0, tm * TOP_K * ROW_WORDS), :]
    pltpu.make_async_copy(done, done, sem).wait()


def _dispatch(dest_flat, h2g, p_rows, tm=256):
    n_tok = h2g.shape[0] // ROW_WORDS
    grid_spec = pltpu.PrefetchScalarGridSpec(
        num_scalar_prefetch=1,
        grid=(n_tok // tm,),
        in_specs=[pl.BlockSpec((tm * ROW_WORDS, LANES), lambda i, d: (i, 0))],
        out_specs=pl.BlockSpec(memory_space=pl.ANY),
        scratch_shapes=[pltpu.SemaphoreType.DMA(())],
    )
    return pl.pallas_call(
        functools.partial(_dispatch_kernel, tm=tm),
        grid_spec=grid_spec,
        out_shape=jax.ShapeDtypeStruct((p_rows * ROW_WORDS, LANES), U32),
        compiler_params=_cparams(("arbitrary",)),
        name="dispatch",
    )(dest_flat, h2g)


def _experts_kernel(be_ref, nxt_ref, nact_ref, x_ref, wg_hbm, wu_hbm, wd_hbm, o_ref,
                    wgf, wuf, wdf, wgb, wub, wdb, sem):
    b = pl.program_id(0)
    active = b < nact_ref[0]
    e = be_ref[b]
    changed = e != be_ref[jnp.maximum(b - 1, 0)]

    def weight_copies(eid):
        return (pltpu.make_async_copy(wg_hbm.at[eid], wgf, sem.at[0]),
                pltpu.make_async_copy(wu_hbm.at[eid], wuf, sem.at[1]),
                pltpu.make_async_copy(wd_hbm.at[eid], wdf, sem.at[2]))

    @pl.when(active & (b == 0))
    def _():
        for cp in weight_copies(e):
            cp.start()

    @pl.when(active & ((b == 0) | changed))
    def _():
        for cp in weight_copies(e):
            cp.wait()
        ch = 256
        for c in range(0, D_MODEL, ch):
            wgb[c:c + ch, :] = wgf[c:c + ch, :].astype(BF16)
            wub[c:c + ch, :] = wuf[c:c + ch, :].astype(BF16)
        for c in range(0, D_EXPERT, ch // 4):
            wdb[c:c + ch // 4, :] = wdf[c:c + ch // 4, :].astype(BF16)
        nxt = nxt_ref[b]

        @pl.when(nxt >= 0)
        def _():
            for cp in weight_copies(nxt):
                cp.start()

    @pl.when(active)
    def _():
        chunks = []
        for s in range(ROW_WORDS):
            word = x_ref[pl.ds(s, MOE_ROWS, stride=ROW_WORDS), :]
            chunks.append(lax.bitcast_convert_type(word << 16, F32).astype(BF16))
            chunks.append(lax.bitcast_convert_type(word & jnp.uint32(0xFFFF0000), F32).astype(BF16))
        x = jnp.concatenate(chunks, axis=1)
        gt = _dot(x, wgb[...])
        a = (gt * _sigmoid(gt) * _dot(x, wub[...])).astype(BF16)
        y = _dot(a, wdb[...])
        for j in range(D_CHUNKS):
            o_ref[pl.ds(j, MOE_ROWS, stride=D_CHUNKS), :] = y[:, j * LANES:(j + 1) * LANES]


def _experts(blk_e, nxt_e, n_active, x_sorted, wg, wu, wd):
    d = wg.shape[1]
    n_blk = x_sorted.shape[0] // (MOE_ROWS * ROW_WORDS)
    hbm = pl.BlockSpec(memory_space=pl.ANY)
    grid_spec = pltpu.PrefetchScalarGridSpec(
        num_scalar_prefetch=3,
        grid=(n_blk,),
        in_specs=[pl.BlockSpec((MOE_ROWS * ROW_WORDS, LANES), lambda b, be, nx, na: (b, 0)), hbm, hbm, hbm],
        out_specs=pl.BlockSpec((MOE_ROWS * D_CHUNKS, LANES), lambda b, be, nx, na: (b, 0)),
        scratch_shapes=[pltpu.VMEM((d, D_EXPERT), F32), pltpu.VMEM((d, D_EXPERT), F32),
                        pltpu.VMEM((D_EXPERT, d), F32),
                        pltpu.VMEM((d, D_EXPERT), BF16), pltpu.VMEM((d, D_EXPERT), BF16),
                        pltpu.VMEM((D_EXPERT, d), BF16),
                        pltpu.SemaphoreType.DMA((3,))],
    )
    return pl.pallas_call(
        _experts_kernel,
        grid_spec=grid_spec,
        out_shape=jax.ShapeDtypeStruct((n_blk * MOE_ROWS * D_CHUNKS, LANES), F32),
        compiler_params=_cparams(("arbitrary",)),
        name="experts",
    )(blk_e, nxt_e, n_active, x_sorted, wg, wu, wd)


def _final_kernel(dest_ref, x1_ref, h2_ref, w8_ref, ys_ref, wg_ref, wu_ref, wd_ref, gate_ref, gf_ref,
                  shift_ref, scale_ref, y_ref, ybuf, ymoe_ref, sem, *, tm, row0):
    i = pl.program_id(0)
    n = pl.num_programs(0)
    tile_rows = tm * TOP_K * D_CHUNKS

    def issue(tile, slot):
        def body(r, carry):
            for k in range(TOP_K):
                d = dest_ref[(row0 + tile * tm + r) * TOP_K + k]
                pltpu.make_async_copy(
                    ys_ref.at[pl.ds(pl.multiple_of(d * D_CHUNKS, D_CHUNKS), D_CHUNKS), :],
                    ybuf.at[slot, pl.ds(pl.multiple_of((r * TOP_K + k) * D_CHUNKS, D_CHUNKS), D_CHUNKS), :],
                    sem.at[slot]).start()
            return carry
        lax.fori_loop(0, tm, body, 0)

    @pl.when(i == 0)
    def _():
        issue(0, 0)

    @pl.when(i + 1 < n)
    def _():
        issue(i + 1, (i + 1) % 2)

    slot = i % 2
    pltpu.make_async_copy(ys_ref.at[pl.ds(0, tile_rows), :], ybuf.at[slot], sem.at[slot]).wait()
    w8 = w8_ref[...]
    wk = [w8[:, k:k + 1] for k in range(TOP_K)]
    for j in range(D_CHUNKS):
        acc = None
        for k in range(TOP_K):
            v = ybuf[slot, pl.ds(k * D_CHUNKS + j, tm, stride=TOP_K * D_CHUNKS), :] * wk[k]
            acc = v if acc is None else acc + v
        ymoe_ref[:, j * LANES:(j + 1) * LANES] = acc

    h = h2_ref[...]
    gt = _dot(h, wg_ref[...])
    a = (gt * _sigmoid(gt) * _dot(h, wu_ref[...])).astype(BF16)
    f = ymoe_ref[...] + _dot(a, wd_ref[...])
    x2 = x1_ref[...] + gate_ref[...] * f
    y_ref[...] = _rms(x2) * gf_ref[...] * (1.0 + scale_ref[...]) + shift_ref[...]


def _final(dest_flat, x1, h2, w8, y_sorted, wg, wu, wd, gate, gf, shift, scale, m, row0, tm=64):
    d = x1.shape[1]
    blk0 = row0 // tm
    in_row = lambda i, dr: (blk0 + i, 0)
    row = lambda i, dr: (i, 0)
    const = lambda i, dr: (0, 0)
    mod_spec = lambda a: pl.BlockSpec((tm, d), row) if a.shape[0] == m else pl.BlockSpec((1, d), const)
    grid_spec = pltpu.PrefetchScalarGridSpec(
        num_scalar_prefetch=1,
        grid=(m // tm,),
        in_specs=[pl.BlockSpec((tm, d), in_row), pl.BlockSpec((tm, d), in_row), pl.BlockSpec((tm, TOP_K), in_row),
                  pl.BlockSpec(memory_space=pl.ANY),
                  pl.BlockSpec(wg.shape, const), pl.BlockSpec(wu.shape, const), pl.BlockSpec(wd.shape, const),
                  mod_spec(gate), pl.BlockSpec((1, d), const), mod_spec(shift), mod_spec(scale)],
        out_specs=pl.BlockSpec((tm, d), row),
        scratch_shapes=[pltpu.VMEM((2, tm * TOP_K * D_CHUNKS, LANES), F32), pltpu.VMEM((tm, d), F32),
                        pltpu.SemaphoreType.DMA((2,))],
    )
    return pl.pallas_call(
        functools.partial(_final_kernel, tm=tm, row0=row0),
        grid_spec=grid_spec,
        out_shape=jax.ShapeDtypeStruct((m, d), F32),
        compiler_params=_cparams(("arbitrary",)),
        name="final",
    )(dest_flat, x1, h2, w8, y_sorted, wg, wu, wd, gate, gf, shift, scale)


def kernel(x_prompt, x_sample, cache_sb, cache_nsa, state_win, page_table, c_prompt, c_sample, w_ada, b_ada, norm1_g, w_in, cmp_pos_k, cmp_pos_v, cmp_wk, cmp_wv, out_norm_g, w_out, norm2_g, w_router, router_bias, w_gate_e, w_up_e, w_down_e, w_gate_s, w_up_s, w_down_s, normf_g):
    b_p, t, d = x_prompt.shape
    bs, n_new, _ = x_sample.shape
    assert b_p == 1 and d == D_MODEL
    n_s = bs * n_new
    n_tok = t + n_s

    c_all = jnp.concatenate([c_prompt, c_sample], axis=0)
    m_pad = -(-c_all.shape[0] // 8) * 8
    c_all = jnp.pad(c_all, ((0, m_pad - c_all.shape[0]), (0, 0)))
    mod = _ada(c_all, w_ada, b_ada).reshape(m_pad, N_ADA, d)
    mod_p = [mod[0:1, i] for i in range(N_ADA)]
    mod_s = [jnp.repeat(mod[1:1 + bs, i], n_new, axis=0) for i in range(N_ADA)]

    w_main = w_in[:, :COL_GATE]
    w_gate = jnp.pad(w_in[:, COL_GATE:], ((0, 0), (0, LANES - N_GATE)))
    w_cat = jnp.concatenate([w_main, w_gate], axis=1).astype(BF16)
    row1 = lambda v: v.reshape(1, -1)

    xp = x_prompt.reshape(t, d)
    xs = x_sample.reshape(n_s, d)
    (qsb_p, sbrows_p, kvsb_p, qn_p, nsarows_p, nsab_p, winrows_p, winb_p, gates_p) = _project(
        xp, row1(norm1_g), mod_p[0], mod_p[1], w_cat)
    (qsb_s, sbrows_s, _, qn_s, nsarows_s, _, winrows_s, _, gates_s) = _project(
        xs, row1(norm1_g), mod_s[0], mod_s[1], w_cat)

    osb_p = _sb_prompt(qsb_p, kvsb_p)
    wpos = jnp.stack([jnp.repeat(cmp_pos_k, HEAD_DIM, axis=1), jnp.repeat(cmp_pos_v, HEAD_DIM, axis=1)])
    wc = jnp.stack([cmp_wk, cmp_wv]).astype(BF16)
    kvc_p = _compress_prompt(nsarows_p, wpos, wc)
    on_p = _nsa_prompt(qn_p, gates_p, kvc_p, nsab_p, winb_p)

    per_seq = lambda a: a.reshape(bs, n_new, a.shape[-1])
    n_phys = cache_sb.shape[0]
    w_buf = state_win.shape[1]
    osb_s = _sb_sample(page_table, per_seq(qsb_s), per_seq(sbrows_s), cache_sb.reshape(n_phys, -1, HEAD_DIM))
    on_s, win_s = _nsa_sample(page_table, per_seq(qn_s), per_seq(gates_s), per_seq(nsarows_s), per_seq(winrows_s),
                              cache_nsa.reshape(n_phys, -1, HEAD_DIM), state_win.reshape(bs, -1, HEAD_DIM), wpos, wc)

    w_out_b = w_out.astype(BF16)
    gout = out_norm_g.reshape(1, -1)
    bufs = _mixout(osb_p, on_p, xp, gout, w_out_b, mod_p[2], row1(norm2_g), mod_p[3], mod_p[4], n_tok, 0)
    x1, h2, h2g = _mixout(osb_s.reshape(n_s, SB_W), on_s.reshape(n_s, NSA_W), xs, gout, w_out_b,
                          mod_s[2], row1(norm2_g), mod_s[3], mod_s[4], n_tok, t, prev=bufs)

    eidx_t, w_t = _router(h2, w_router.T.astype(BF16), router_bias)
    dest, blk_e, nxt_e, n_active, n_blk = _dispatch_plan(eidx_t.T)
    dest_flat = dest.reshape(-1)
    x_sorted = _dispatch(dest_flat, h2g, n_blk * MOE_ROWS)
    y_sorted = _experts(blk_e, nxt_e, n_active, x_sorted, w_gate_e, w_up_e, w_down_e)
    w8 = w_t.T
    ws = (w_gate_s.astype(BF16), w_up_s.astype(BF16), w_down_s.astype(BF16))
    y_p = _final(dest_flat, x1, h2, w8, y_sorted, *ws, mod_p[5], row1(normf_g), mod_p[6], mod_p[7], t, 0)
    y_s = _final(dest_flat, x1, h2, w8, y_sorted, *ws, mod_s[5], row1(normf_g), mod_s[6], mod_s[7], n_s, t)

    keep = min(WINDOW, t)
    return (y_p.reshape(1, t, d), y_s.reshape(bs, n_new, d),
            sbrows_p.reshape(1, t, 2, N_SB_HEADS, HEAD_DIM), sbrows_s.reshape(bs, n_new, 2, N_SB_HEADS, HEAD_DIM),
            nsarows_p.reshape(1, t, 4, N_NSA_KV, HEAD_DIM), nsarows_s.reshape(bs, n_new, 4, N_NSA_KV, HEAD_DIM),
            winrows_p[t - keep:].reshape(1, keep, 2, N_NSA_KV, HEAD_DIM),
            win_s.reshape(bs, w_buf, 2, N_NSA_KV, HEAD_DIM))
```

```python
import functools

import numpy as np
import jax
import jax.numpy as jnp
from jax import lax
from jax.experimental import pallas as pl
from jax.experimental.pallas import tpu as pltpu

F32 = jnp.float32
BF16 = jnp.bfloat16
U32 = jnp.uint32

D_MODEL = 2048
HEAD_DIM = 128
N_SB_HEADS = 8
N_NSA_HEADS = 8
N_NSA_KV = 2
NSA_GROUP = N_NSA_HEADS // N_NSA_KV
CMP_BLOCK = 32
CMP_STRIDE = 16
SLC_BLOCK = 64
SLC_TOP = 16
WINDOW = 512
FORCE = 1e4
NEG = -1e30
N_EXPERTS = 256
TOP_K = 8
N_EXPERT_GROUPS = 8
TOPK_GROUPS = 4
D_EXPERT = D_MODEL // 4
ROUTED_SCALE = 2.5
MOE_ROWS = 128
N_ADA = 8
EPS = 1e-6
ATT_SCALE = HEAD_DIM ** -0.5

SB_W = N_SB_HEADS * HEAD_DIM
NSA_W = N_NSA_HEADS * HEAD_DIM
KV_W = N_NSA_KV * HEAD_DIM
COL_QSB = 0
COL_KVSB = SB_W
COL_QN = 3 * SB_W
COL_NSA = COL_QN + NSA_W
COL_WIN = COL_NSA + 4 * KV_W
COL_GATE = COL_WIN + 2 * KV_W
N_GATE = 3 * N_NSA_HEADS
LANES = 128
V7X_VMEM_LIMIT = 56 * 1024 * 1024
D_CHUNKS = D_MODEL // LANES
ROW_WORDS = D_CHUNKS // 2
W_DMA_SPLIT = 4

NT = (((1,), (1,)), ((), ()))


def _cparams(sem):
    return pltpu.CompilerParams(dimension_semantics=sem, vmem_limit_bytes=V7X_VMEM_LIMIT)


def _sigmoid(x):
    return 1.0 / (1.0 + jnp.exp(-x))


def _split_bf16(x):
    hi = x.astype(BF16)
    lo = (x - hi.astype(F32)).astype(BF16)
    return hi, lo


def _dot(a, b):
    return jnp.dot(a, b, preferred_element_type=F32)


def _dot_nt(a, b):
    return lax.dot_general(a, b, NT, preferred_element_type=F32)


def _rms(x):
    return x * lax.rsqrt(jnp.mean(x * x, axis=-1, keepdims=True) + EPS)


def _masked_softmax(s, mask):
    s = jnp.where(mask, s, NEG)
    m = jnp.max(s, axis=-1, keepdims=True)
    e = jnp.where(mask, jnp.exp(s - m), 0.0)
    return e / jnp.maximum(jnp.sum(e, axis=-1, keepdims=True), 1e-30)


def _head_slope(head):
    out = jnp.full(head.shape, 2.0 ** -N_NSA_HEADS, F32)
    for i in range(N_NSA_HEADS - 1):
        out = jnp.where(head == i, 2.0 ** -(i + 1), out)
    return out


def _top_blocks(score, n_top):
    blk = lax.broadcasted_iota(jnp.int32, score.shape, 1).astype(F32)
    sel = jnp.zeros(score.shape, F32)
    for _ in range(n_top):
        m = jnp.max(score, axis=-1, keepdims=True)
        idx = jnp.min(jnp.where(score == m, blk, float(score.shape[1])), axis=-1, keepdims=True)
        pick = blk == idx
        sel = jnp.where(pick, 1.0, sel)
        score = jnp.where(pick, -2.0, score)
    return sel


def _top_blocks_by_rank(score, n_top, n_cand):
    blk = lax.broadcasted_iota(jnp.int32, score.shape, 1)
    rank = jnp.zeros(score.shape, F32)
    for i in range(n_cand):
        c = score[:, i:i + 1]
        tie = jnp.where(blk > i, 1.0, 0.0)
        rank = rank + jnp.where(c > score, 1.0, jnp.where(c == score, tie, 0.0))
    return jnp.where(rank < n_top, 1.0, 0.0)


def _slc_scores(imp, qpos):
    blk = lax.broadcasted_iota(jnp.int32, imp.shape, 1)
    qb = qpos // SLC_BLOCK
    forced = (blk == 0) | ((blk <= qb) & (blk >= qb - 1))
    return jnp.where(blk > qb, -1.0, jnp.where(forced, FORCE, imp))


def _gate_cols(gates, branch, group, n_rows_per_head):
    lane = lax.broadcasted_iota(jnp.int32, gates.shape, 1)
    cols = [jnp.sum(jnp.where(lane == branch * N_NSA_HEADS + group * NSA_GROUP + h, gates, 0.0),
                    axis=-1, keepdims=True) for h in range(NSA_GROUP)]
    return jnp.concatenate(cols, axis=0)


def _ada_kernel(c_ref, w_ref, b_ref, o_ref):
    c = c_ref[...]
    a = (c * _sigmoid(c)).astype(BF16)
    o_ref[...] = _dot(a, w_ref[...].astype(BF16)) + b_ref[...]


def _ada(c, w_ada, b_ada, tn=1024):
    m, d = c.shape
    n = w_ada.shape[1]
    return pl.pallas_call(
        _ada_kernel,
        grid=(n // tn,),
        in_specs=[pl.BlockSpec((m, d), lambda j: (0, 0)),
                  pl.BlockSpec((d, tn), lambda j: (0, j)),
                  pl.BlockSpec((1, tn), lambda j: (0, j))],
        out_specs=pl.BlockSpec((m, tn), lambda j: (0, j)),
        out_shape=jax.ShapeDtypeStruct((m, n), F32),
        compiler_params=_cparams(("arbitrary",)),
        name="ada",
    )(c, w_ada, b_ada.reshape(1, n))


def _proj_kernel(x_ref, g_ref, shift_ref, scale_ref, w_ref,
                 qsb_ref, sbrows_ref, kvsb_ref, qn_ref, nsarows_ref, nsab_ref,
                 winrows_ref, winb_ref, gates_ref):
    h = (_rms(x_ref[...]) * g_ref[...] * (1.0 + scale_ref[...]) + shift_ref[...]).astype(BF16)
    ch = 512

    def mm(c0, n):
        return _dot(h, w_ref[:, c0:c0 + n])

    for c in range(0, SB_W, ch):
        qsb_ref[:, c:c + ch] = mm(COL_QSB + c, ch).astype(BF16)
    for c in range(0, 2 * SB_W, ch):
        r = mm(COL_KVSB + c, ch)
        sbrows_ref[:, c:c + ch] = r
        kvsb_ref[:, c:c + ch] = r.astype(BF16)
    for c in range(0, NSA_W, ch):
        qn_ref[:, c:c + ch] = mm(COL_QN + c, ch).astype(BF16)
    for c in range(0, 4 * KV_W, ch):
        r = mm(COL_NSA + c, ch)
        nsarows_ref[:, c:c + ch] = r
        nsab_ref[:, c:c + ch] = r.astype(BF16)
    r = mm(COL_WIN, 2 * KV_W)
    winrows_ref[...] = r
    winb_ref[...] = r.astype(BF16)
    gates_ref[...] = _sigmoid(mm(COL_GATE, LANES))


def _project(x, g, shift, scale, w_cat, tm=256):
    m, d = x.shape
    row = lambda i: (i, 0)
    const = lambda i: (0, 0)
    mod_spec = lambda a: pl.BlockSpec((tm, d), row) if a.shape[0] == m else pl.BlockSpec((1, d), const)
    widths = [(SB_W, BF16), (2 * SB_W, F32), (2 * SB_W, BF16), (NSA_W, BF16), (4 * KV_W, F32),
              (4 * KV_W, BF16), (2 * KV_W, F32), (2 * KV_W, BF16), (LANES, F32)]
    return pl.pallas_call(
        _proj_kernel,
        grid=(m // tm,),
        in_specs=[pl.BlockSpec((tm, d), row),
                  pl.BlockSpec((1, d), const),
                  mod_spec(shift), mod_spec(scale),
                  pl.BlockSpec(w_cat.shape, const, pipeline_mode=pl.Buffered(1))],
        out_specs=[pl.BlockSpec((tm, w), row) for w, _ in widths],
        out_shape=[jax.ShapeDtypeStruct((m, w), dt) for w, dt in widths],
        compiler_params=_cparams(("arbitrary",)),
        name="project",
    )(x, g, shift, scale, w_cat)


def _sb_logits(z):
    sp = jnp.log1p(jnp.exp(-jnp.abs(z)))
    return jnp.minimum(z, 0.0) - sp, jnp.minimum(-z, 0.0) - sp


def _sb_prompt_kernel(q_ref, k_ref, v_ref, o_ref, *, tq, heads):
    i = pl.program_id(1)
    row = lax.broadcasted_iota(jnp.int32, (2 * tq, tq), 0)
    col = lax.broadcasted_iota(jnp.int32, (2 * tq, tq), 1)
    later2 = jnp.where(jnp.where(row >= tq, row - tq, row) > col, 1.0, 0.0).astype(BF16)
    causal = lax.broadcasted_iota(jnp.int32, (tq, tq), 1) < lax.broadcasted_iota(jnp.int32, (tq, tq), 0)

    hsl = [slice(hh * HEAD_DIM, (hh + 1) * HEAD_DIM) for hh in range(heads)]

    def block(j, state, diagonal):
        off = pl.multiple_of(j * tq, tq)
        zs = [_dot_nt(q_ref[:, hs], k_ref[pl.ds(off, tq), hs]) * ATT_SCALE for hs in hsl]
        logits = [_sb_logits(z) for z in zs]
        if diagonal:
            logits = [(ls, jnp.where(causal, l1m, 0.0)) for ls, l1m in logits]
        split = [jnp.concatenate(_split_bf16(l1m), axis=1) for _, l1m in logits]
        suffix = [_dot(sp, later2) + state[hh][0] for hh, sp in enumerate(split)]
        a = [jnp.exp(logits[hh][0] + suffix[hh]) for hh in range(heads)]
        if diagonal:
            a = [jnp.where(causal, x, 0.0) for x in a]
        acc = [state[hh][1] + _dot(a[hh].astype(BF16), v_ref[pl.ds(off, tq), hsl[hh]]) for hh in range(heads)]
        return tuple((suffix[hh][:, :1] + logits[hh][1][:, :1], acc[hh]) for hh in range(heads))

    init = tuple((jnp.zeros((tq, 1), F32), jnp.zeros((tq, HEAD_DIM), F32)) for _ in range(heads))
    state = block(i, init, True)
    state = lax.fori_loop(0, i, lambda t, st: block(i - 1 - t, st, False), state)
    for hh in range(heads):
        o_ref[:, hh * HEAD_DIM:(hh + 1) * HEAD_DIM] = state[hh][1]


def _sb_prompt(qsb, kvsb, tq=256, heads=4):
    t = qsb.shape[0]
    w = heads * HEAD_DIM
    n_hp = N_SB_HEADS // heads
    return pl.pallas_call(
        functools.partial(_sb_prompt_kernel, tq=tq, heads=heads),
        grid=(n_hp, t // tq),
        in_specs=[pl.BlockSpec((tq, w), lambda h, i: (i, h)),
                  pl.BlockSpec((t, w), lambda h, i: (0, h)),
                  pl.BlockSpec((t, w), lambda h, i: (0, n_hp + h))],
        out_specs=pl.BlockSpec((tq, w), lambda h, i: (i, h)),
        out_shape=jax.ShapeDtypeStruct((t, SB_W), F32),
        compiler_params=_cparams(("arbitrary", "arbitrary")),
        name="sb_prompt",
    )(qsb, kvsb, kvsb)


def _sb_sample_kernel(pt_ref, q_ref, new_ref, *refs, n_new, ppg):
    cache_refs = refs[:ppg]
    o_ref, qbd_ref, carry_ref, acc_ref = refs[ppg:]
    p = pl.program_id(1)
    rows = N_SB_HEADS * n_new
    kk = lax.broadcasted_iota(jnp.int32, (2 * LANES, 2 * LANES), 0)
    cc = lax.broadcasted_iota(jnp.int32, (2 * LANES, 2 * LANES), 1)
    kk = jnp.where(kk >= LANES, kk - LANES, kk)
    later_and_all = jnp.where((cc >= LANES) | (kk > cc), 1.0, 0.0).astype(BF16)
    stride = 2 * N_SB_HEADS
    hsl = [slice(h * HEAD_DIM, (h + 1) * HEAD_DIM) for h in range(N_SB_HEADS)]
    rsl = [slice(h * n_new, (h + 1) * n_new) for h in range(N_SB_HEADS)]

    def scores(kmat, diagonal):
        z = _dot_nt(qbd_ref[...].astype(BF16), kmat) * ATT_SCALE
        ls, l1m = _sb_logits(z)
        causal = None
        if diagonal:
            key = lax.broadcasted_iota(jnp.int32, (rows, LANES), 1)
            qry = lax.broadcasted_iota(jnp.int32, (rows, LANES), 0) % n_new
            causal = key < qry
            l1m = jnp.where(causal, l1m, 0.0)
        hi, lo = _split_bf16(l1m)
        return ls, _dot(jnp.concatenate([hi, lo], axis=1), later_and_all), causal

    def accumulate(ls, st, causal, vmat, carry):
        a = jnp.exp(ls + st[:, :LANES] + carry)
        if causal is not None:
            a = jnp.where(causal, a, 0.0)
        o_all = _dot(a.astype(BF16), vmat)
        for h in range(N_SB_HEADS):
            acc_ref[rsl[h], :] += o_all[rsl[h], hsl[h]]
        return carry + st[:, LANES:]

    @pl.when(p == 0)
    def _():
        acc_ref[...] = jnp.zeros_like(acc_ref)
        qbd_ref[...] = jnp.zeros_like(qbd_ref)
        for h in range(N_SB_HEADS):
            qbd_ref[rsl[h], hsl[h]] = q_ref[0, :, hsl[h]].astype(F32)
        pad = jnp.zeros((LANES - n_new, SB_W), F32)
        knew = jnp.concatenate([new_ref[0, :, :SB_W], pad], axis=0).astype(BF16)
        vnew = jnp.concatenate([new_ref[0, :, SB_W:], pad], axis=0).astype(BF16)
        ls, st, causal = scores(knew, True)
        carry_ref[...] = accumulate(ls, st, causal, vnew, jnp.zeros((rows, LANES), F32))

    def page_mat(ref, c0):
        return jnp.concatenate([ref[0, pl.ds(c0 + h, LANES, stride=stride), :].astype(BF16)
                                for h in range(N_SB_HEADS)], axis=1)

    pre = [scores(page_mat(ref, 0), False) for ref in cache_refs]
    carry = carry_ref[...]
    for ref, (ls, st, _) in zip(cache_refs, pre):
        carry = accumulate(ls, st, None, page_mat(ref, N_SB_HEADS), carry)
    carry_ref[...] = carry

    @pl.when(p == pl.num_programs(1) - 1)
    def _():
        for h in range(N_SB_HEADS):
            o_ref[0, :, hsl[h]] = acc_ref[rsl[h], :]


def _sb_sample(page_table, qsb, sbrows, cache, ppg=8):
    bs, n_new, _ = qsb.shape
    n_pages = page_table.shape[1]
    page_rows = cache.shape[1]
    assert page_rows == LANES * 2 * N_SB_HEADS and n_pages % ppg == 0
    rows = N_SB_HEADS * n_new

    def page_map(q):
        return lambda b, p, pt: (pt[b * n_pages + n_pages - 1 - (p * ppg + q)], 0, 0)

    grid_spec = pltpu.PrefetchScalarGridSpec(
        num_scalar_prefetch=1,
        grid=(bs, n_pages // ppg),
        in_specs=[pl.BlockSpec((1, n_new, SB_W), lambda b, p, pt: (b, 0, 0)),
                  pl.BlockSpec((1, n_new, 2 * SB_W), lambda b, p, pt: (b, 0, 0))]
                 + [pl.BlockSpec((1, page_rows, HEAD_DIM), page_map(q)) for q in range(ppg)],
        out_specs=pl.BlockSpec((1, n_new, SB_W), lambda b, p, pt: (b, 0, 0)),
        scratch_shapes=[pltpu.VMEM((rows, SB_W), F32), pltpu.VMEM((rows, LANES), F32),
                        pltpu.VMEM((rows, HEAD_DIM), F32)],
    )
    return pl.pallas_call(
        functools.partial(_sb_sample_kernel, n_new=n_new, ppg=ppg),
        grid_spec=grid_spec,
        out_shape=jax.ShapeDtypeStruct((bs, n_new, SB_W), F32),
        compiler_params=_cparams(("arbitrary", "arbitrary")),
        name="sb_sample",
    )(page_table.reshape(-1), qsb, sbrows, *([cache] * ppg))


def _pool_chunks(x, w1, w2):
    x3 = x.reshape(x.shape[0] // CMP_STRIDE, CMP_STRIDE, x.shape[1])
    return jnp.sum(x3 * w1[None], axis=1), jnp.sum(x3 * w2[None], axis=1)


def _compress_kernel(rows_ref, wpos_ref, wc_ref, o_ref, first_ref, second_ref, *, step_rows):
    n_rows = rows_ref.shape[0]
    n_chunks = n_rows // CMP_STRIDE
    w1 = wpos_ref[0, :CMP_STRIDE, :]
    w2 = wpos_ref[0, CMP_STRIDE:, :]
    cps = step_rows // CMP_STRIDE

    def body(s, carry):
        r0 = pl.multiple_of(s * step_rows, step_rows)
        c0 = pl.multiple_of(s * cps, cps)
        f, sec = _pool_chunks(rows_ref[pl.ds(r0, step_rows), :], w1, w2)
        first_ref[pl.ds(c0, cps), :] = f
        second_ref[pl.ds(c0, cps), :] = sec
        return carry

    lax.fori_loop(0, n_rows // step_rows, body, 0)
    pre = first_ref[...] + pltpu.roll(second_ref[...], n_chunks - 1, axis=0)
    for g in range(N_NSA_KV):
        sl = slice(g * HEAD_DIM, (g + 1) * HEAD_DIM)
        o_ref[0, :, sl] = _dot(pre[:, sl].astype(BF16), wc_ref[0, g]).astype(BF16)


def _compress_prompt(nsarows, wpos, wc, step_rows=1024):
    t = nsarows.shape[0]
    n_chunks = t // CMP_STRIDE
    return pl.pallas_call(
        functools.partial(_compress_kernel, step_rows=step_rows),
        grid=(2,),
        in_specs=[pl.BlockSpec((t, KV_W), lambda c: (0, c)),
                  pl.BlockSpec((1, CMP_BLOCK, KV_W), lambda c: (c, 0, 0)),
                  pl.BlockSpec((1, N_NSA_KV, HEAD_DIM, HEAD_DIM), lambda c: (c, 0, 0, 0))],
        out_specs=pl.BlockSpec((1, n_chunks, KV_W), lambda c: (c, 0, 0)),
        out_shape=jax.ShapeDtypeStruct((2, n_chunks, KV_W), BF16),
        scratch_shapes=[pltpu.VMEM((n_chunks, KV_W), F32), pltpu.VMEM((n_chunks, KV_W), F32)],
        compiler_params=_cparams(("arbitrary",)),
        name="compress_prompt",
    )(nsarows, wpos, wc)


def _importance_matrix(n_cmp, n_slc_lanes):
    m = np.arange(n_cmp)[:, None]
    j = np.arange(n_slc_lanes)[None, :]
    d = m - (SLC_BLOCK // CMP_STRIDE) * j
    w = np.where((d >= 0) & (d <= 2), 1.0, np.where((d == -1) | (d == 3), 0.5, 0.0))
    return jnp.asarray(w, BF16)


def _block_expansion(n_keys):
    b = np.arange(LANES)[:, None]
    k = np.arange(n_keys)[None, :]
    return jnp.asarray((b == k // SLC_BLOCK).astype(np.float32), BF16)


def _nsa_prompt_kernel(q_ref, gate_ref, kc_ref, vc_ref, sk_ref, sv_ref, wk_ref, wv_ref,
                       e_ref, mimp_ref, o_ref, *, tq, tk):
    i = pl.program_id(0)
    start = i * tq
    rows = NSA_GROUP * tq
    groups = range(N_NSA_KV)
    r = lax.broadcasted_iota(jnp.int32, (rows, 1), 0)
    qpos = start + r % tq
    qpos_q = start + lax.broadcasted_iota(jnp.int32, (tq, 1), 0)
    n_cmp = kc_ref.shape[1]
    kc_end = CMP_STRIDE * lax.broadcasted_iota(jnp.int32, (1, n_cmp), 1) + (CMP_BLOCK - 1)
    gsl = [slice(g * HEAD_DIM, (g + 1) * HEAD_DIM) for g in groups]

    qg, slope, o_cmp, sel = [], [], [], []
    for g in groups:
        qg.append(jnp.concatenate(
            [q_ref[:, (g * NSA_GROUP + h) * HEAD_DIM:(g * NSA_GROUP + h + 1) * HEAD_DIM] for h in range(NSA_GROUP)],
            axis=0))
        slope.append(_head_slope(g * NSA_GROUP + r // tq))
        s = _dot_nt(qg[g], kc_ref[0, :, gsl[g]]) * ATT_SCALE - slope[g] * (qpos - kc_end).astype(F32)
        p = _masked_softmax(s, kc_end <= qpos)
        o_cmp.append(_dot(p.astype(BF16), vc_ref[0, :, gsl[g]]))
        p4 = p[0:tq] + p[tq:2 * tq] + p[2 * tq:3 * tq] + p[3 * tq:4 * tq]
        hi, lo = _split_bf16(p4)
        imp = _dot(hi, mimp_ref[...]) + _dot(lo, mimp_ref[...])
        sel.append(_top_blocks(_slc_scores(imp, qpos_q), SLC_TOP))

    lane_f = lax.broadcasted_iota(jnp.int32, (1, tk), 1).astype(F32)
    lane_bias = [slope[g] * lane_f for g in groups]
    row_bias = [slope[g] * qpos.astype(F32) for g in groups]
    drop = 2.0 * NEG
    sel_bias = [((sel[g] - 1.0) * (-drop)).astype(BF16) for g in groups]

    def attend(k_ref, v_ref, t_lo, t_hi, selected):
        def body(t, state, edge):
            off = pl.multiple_of(t * tk, tk)
            z = [_dot_nt(qg[g], k_ref[pl.ds(off, tk), gsl[g]]) for g in groups]
            s = [z[g] * ATT_SCALE + lane_bias[g] for g in groups]
            if selected:
                mb = [_dot(sel_bias[g], e_ref[t]) for g in groups]
                s = [(s[g].reshape(NSA_GROUP, tq, tk) + mb[g][None]).reshape(rows, tk) for g in groups]
            if edge:
                dist = qpos - (off + lax.broadcasted_iota(jnp.int32, (1, tk), 1))
                ok = (dist >= 0) if selected else ((dist >= 0) & (dist < WINDOW))
                s = [jnp.where(ok, s[g], drop) for g in groups]
            shift = [row_bias[g] - slope[g] * off.astype(F32) for g in groups]
            m_new = [jnp.maximum(state[g][0], jnp.max(s[g], axis=-1, keepdims=True) - shift[g]) for g in groups]
            e = [jnp.exp(s[g] - (shift[g] + m_new[g])) for g in groups]
            pv = [_dot(e[g].astype(BF16), v_ref[pl.ds(off, tk), gsl[g]]) for g in groups]
            out = []
            for g in groups:
                alpha = jnp.exp(state[g][0] - m_new[g])
                out.append((m_new[g], state[g][1] * alpha + jnp.sum(e[g], axis=-1, keepdims=True),
                            state[g][2] * alpha + pv[g]))
            return tuple(out)

        init = tuple((jnp.full((rows, 1), NEG, F32), jnp.zeros((rows, 1), F32), jnp.zeros((rows, HEAD_DIM), F32))
                     for _ in groups)
        if selected:
            state = lax.fori_loop(t_lo, t_hi - 1, lambda t, st: body(t, st, False), init)
            state = body(t_hi - 1, state, True)
        else:
            state = lax.fori_loop(t_lo, t_hi, lambda t, st: body(t, st, True), init)
        return [state[g][2] / jnp.maximum(state[g][1], 1e-30) for g in groups]

    last = (start + tq - 1) // tk
    o_slc = attend(sk_ref, sv_ref, 0, last + 1, True)
    o_win = attend(wk_ref, wv_ref, jnp.maximum(start - WINDOW, 0) // tk, last + 1, False)

    gates = gate_ref[...]
    for g in groups:
        o = (_gate_cols(gates, 0, g, tq) * o_cmp[g] + _gate_cols(gates, 1, g, tq) * o_slc[g]
             + _gate_cols(gates, 2, g, tq) * o_win[g])
        for h in range(NSA_GROUP):
            c = (g * NSA_GROUP + h) * HEAD_DIM
            o_ref[:, c:c + HEAD_DIM] = o[h * tq:(h + 1) * tq]


def _nsa_prompt(qn, gates, kvc, nsab, winb, tq=128, tk=256):
    t = qn.shape[0]
    n_cmp = kvc.shape[1]
    e3 = _block_expansion(t).reshape(LANES, t // tk, tk).transpose(1, 0, 2)
    mimp = _importance_matrix(n_cmp, LANES)
    once = pl.Buffered(1)
    col = lambda c: pl.BlockSpec((t, KV_W), lambda i: (0, c), pipeline_mode=once)
    return pl.pallas_call(
        functools.partial(_nsa_prompt_kernel, tq=tq, tk=tk),
        grid=(t // tq,),
        in_specs=[pl.BlockSpec((tq, NSA_W), lambda i: (i, 0)),
                  pl.BlockSpec((tq, LANES), lambda i: (i, 0)),
                  pl.BlockSpec((1, n_cmp, KV_W), lambda i: (0, 0, 0), pipeline_mode=once),
                  pl.BlockSpec((1, n_cmp, KV_W), lambda i: (1, 0, 0), pipeline_mode=once),
                  col(2), col(3), col(0), col(1),
                  pl.BlockSpec(e3.shape, lambda i: (0, 0, 0), pipeline_mode=once),
                  pl.BlockSpec(mimp.shape, lambda i: (0, 0), pipeline_mode=once)],
        out_specs=pl.BlockSpec((tq, NSA_W), lambda i: (i, 0)),
        out_shape=jax.ShapeDtypeStruct((t, NSA_W), F32),
        compiler_params=_cparams(("arbitrary",)),
        name="nsa_prompt",
    )(qn, gates, kvc, kvc, nsab, nsab, winb, winb, e3, mimp)


def _top_blocks_t(score, n_top):
    blk = lax.broadcasted_iota(jnp.int32, score.shape, 0).astype(F32)
    sel = jnp.zeros(score.shape, F32)
    for _ in range(n_top):
        m = jnp.max(score, axis=0, keepdims=True)
        idx = jnp.min(jnp.where(score == m, blk, float(score.shape[0])), axis=0, keepdims=True)
        pick = blk == idx
        sel = jnp.where(pick, 1.0, sel)
        score = jnp.where(pick, -2.0, score)
    return sel


def _nsa_prompt_t_kernel(q_ref, gate_ref, kc_ref, vct_ref, sk_ref, svt_ref, wk_ref, wvt_ref,
                         et_ref, mimpt_ref, o_ref, cb_ref, *, tq, tk):
    i = pl.program_id(0)
    start = i * tq
    cols = NSA_GROUP * tq
    groups = range(N_NSA_KV)
    gsl = [slice(g * HEAD_DIM, (g + 1) * HEAD_DIM) for g in groups]
    hsl = [slice(h * tq, (h + 1) * tq) for h in range(NSA_GROUP)]
    slope_of = lambda g, h: 2.0 ** -(g * NSA_GROUP + h + 1)
    drop = 2.0 * NEG
    n_cmp = kc_ref.shape[1]

    @pl.when(i == 0)
    def _():
        kio = lax.broadcasted_iota(jnp.int32, (tk, tq), 0).astype(F32)
        for hh in range(N_NSA_HEADS):
            cb_ref[hh] = (2.0 ** -(hh + 1)) * kio

    qpos = start + lax.broadcasted_iota(jnp.int32, (1, tq), 1)
    qpos_f = qpos.astype(F32)

    def head_row(fn):
        return jnp.concatenate([fn(h) for h in range(NSA_GROUP)], axis=1)

    qg, o_cmp, selb = [], [], []
    dist_c = qpos - (CMP_STRIDE * lax.broadcasted_iota(jnp.int32, (n_cmp, tq), 0) + (CMP_BLOCK - 1))
    seen_c = dist_c >= 0
    dist_cf = dist_c.astype(F32)
    for g in groups:
        qg.append(jnp.concatenate(
            [q_ref[:, (g * NSA_GROUP + h) * HEAD_DIM:(g * NSA_GROUP + h + 1) * HEAD_DIM] for h in range(NSA_GROUP)],
            axis=0))
        zc = _dot_nt(kc_ref[0, :, gsl[g]], qg[g]) * ATT_SCALE
        ps = []
        for h in range(NSA_GROUP):
            s = jnp.where(seen_c, zc[:, hsl[h]] - slope_of(g, h) * dist_cf, NEG)
            e = jnp.where(seen_c, jnp.exp(s - jnp.max(s, axis=0, keepdims=True)), 0.0)
            ps.append(e / jnp.maximum(jnp.sum(e, axis=0, keepdims=True), 1e-30))
        o_cmp.append(_dot(vct_ref[gsl[g], :], jnp.concatenate(ps, axis=1).astype(BF16)))
        hi, lo = _split_bf16(ps[0] + ps[1] + ps[2] + ps[3])
        imp = _dot(mimpt_ref[...], hi) + _dot(mimpt_ref[...], lo)
        blk = lax.broadcasted_iota(jnp.int32, imp.shape, 0)
        qb = qpos // SLC_BLOCK
        forced = (blk == 0) | ((blk <= qb) & (blk >= qb - 1))
        score = jnp.where(blk > qb, -1.0, jnp.where(forced, FORCE, imp))
        selb.append(((_top_blocks_t(score, SLC_TOP) - 1.0) * (-drop)).astype(BF16))

    def attend(k_ref, vt_ref, t_lo, t_hi, selected):
        def tile(t, state, edge):
            off = pl.multiple_of(t * tk, tk)
            off_f = off.astype(F32)
            zs = [_dot_nt(k_ref[pl.ds(off, tk), gsl[g]], qg[g]) for g in groups]
            mbs = [_dot(et_ref[t], selb[g]) for g in groups] if selected else None
            if edge:
                dist = qpos - (off + lax.broadcasted_iota(jnp.int32, (tk, tq), 0))
                ok = (dist >= 0) if selected else ((dist >= 0) & (dist < WINDOW))
            out = []
            for g in groups:
                parts = []
                for h in range(NSA_GROUP):
                    sh = zs[g][:, hsl[h]] * ATT_SCALE + cb_ref[g * NSA_GROUP + h]
                    if selected:
                        sh = sh + mbs[g]
                    if edge:
                        sh = jnp.where(ok, sh, drop)
                    parts.append(sh)
                s = jnp.concatenate(parts, axis=1)
                shift = head_row(lambda h: slope_of(g, h) * (qpos_f - off_f))
                m_old, l_old, acc_old = state[g]
                m_new = jnp.maximum(m_old, jnp.max(s, axis=0, keepdims=True) - shift)
                e = jnp.exp(s - (shift + m_new))
                alpha = jnp.exp(m_old - m_new)
                pv = _dot(vt_ref[t, gsl[g], :], e.astype(BF16))
                out.append((m_new, l_old * alpha + jnp.sum(e, axis=0, keepdims=True), acc_old * alpha + pv))
            return tuple(out)

        init = tuple((jnp.full((1, cols), NEG, F32), jnp.zeros((1, cols), F32), jnp.zeros((HEAD_DIM, cols), F32))
                     for _ in groups)
        if selected:
            state = lax.fori_loop(t_lo, t_hi - 1, lambda t, st: tile(t, st, False), init)
            state = tile(t_hi - 1, state, True)
        else:
            state = lax.fori_loop(t_lo, t_hi, lambda t, st: tile(t, st, True), init)
        return [state[g][2] / jnp.maximum(state[g][1], 1e-30) for g in groups]

    last = (start + tq - 1) // tk
    o_slc = attend(sk_ref, svt_ref, 0, last + 1, True)
    o_win = attend(wk_ref, wvt_ref, jnp.maximum(start - WINDOW, 0) // tk, last + 1, False)

    gates_t = gate_ref[...].T
    for g in groups:
        def gate_row(branch):
            return head_row(lambda h: gates_t[branch * N_NSA_HEADS + g * NSA_GROUP + h:
                                              branch * N_NSA_HEADS + g * NSA_GROUP + h + 1, :])
        o = gate_row(0) * o_cmp[g] + gate_row(1) * o_slc[g] + gate_row(2) * o_win[g]
        for h in range(NSA_GROUP):
            c = (g * NSA_GROUP + h) * HEAD_DIM
            o_ref[:, c:c + HEAD_DIM] = o[:, hsl[h]].T


def _nsa_prompt_t(qn, gates, kvc, nsab, winb, tq=128, tk=256):
    t = qn.shape[0]
    n_cmp = kvc.shape[1]
    n_tiles = t // tk
    tiles_t = lambda v: v.T.reshape(KV_W, n_tiles, tk).transpose(1, 0, 2)
    svt = tiles_t(nsab[:, 3 * KV_W:])
    wvt = tiles_t(winb[:, KV_W:])
    vct = kvc[1].T
    et = jnp.transpose(_block_expansion(t)).reshape(n_tiles, tk, LANES)
    mimpt = jnp.transpose(_importance_matrix(n_cmp, LANES))
    once = pl.Buffered(1)
    full = lambda a: pl.BlockSpec(a.shape, lambda i: (0,) * a.ndim, pipeline_mode=once)
    col = lambda c: pl.BlockSpec((t, KV_W), lambda i: (0, c), pipeline_mode=once)
    return pl.pallas_call(
        functools.partial(_nsa_prompt_t_kernel, tq=tq, tk=tk),
        grid=(t // tq,),
        in_specs=[pl.BlockSpec((tq, NSA_W), lambda i: (i, 0)),
                  pl.BlockSpec((tq, LANES), lambda i: (i, 0)),
                  pl.BlockSpec((1, n_cmp, KV_W), lambda i: (0, 0, 0), pipeline_mode=once),
                  full(vct), col(2), full(svt), col(0), full(wvt), full(et), full(mimpt)],
        out_specs=pl.BlockSpec((tq, NSA_W), lambda i: (i, 0)),
        out_shape=jax.ShapeDtypeStruct((t, NSA_W), F32),
        scratch_shapes=[pltpu.VMEM((N_NSA_HEADS, tk, tq), F32)],
        compiler_params=_cparams(("arbitrary",)),
        name="nsa_prompt",
    )(qn, gates, kvc, vct, nsab, svt, winb, wvt, et, mimpt)


def _nsa_sample_kernel(pt_ref, q_ref, gate_ref, newrows_ref, newwin_ref, win_ref, wpos_ref, wc_ref,
                       e_ref, mimp_ref, *refs, n_new, n_pages, ppg):
    cache_refs = refs[:ppg]
    o_ref, winout_ref, first_ref, second_ref, ks_ref, vs_ref = refs[ppg:]
    p = pl.program_id(1)
    page = LANES
    past = n_pages * page
    cpp = page // CMP_STRIDE
    n_cmp = first_ref.shape[1]
    n_keys = ks_ref.shape[0]
    w_buf = win_ref.shape[1] // (2 * N_NSA_KV)
    gsl = [slice(g * HEAD_DIM, (g + 1) * HEAD_DIM) for g in range(N_NSA_KV)]
    kinds = 4 * N_NSA_KV

    @pl.when(p == 0)
    def _():
        first_ref[...] = jnp.zeros_like(first_ref)
        second_ref[...] = jnp.zeros_like(second_ref)

    for q, cache_ref in enumerate(cache_refs):
        pg = p * ppg + q
        c0 = pl.multiple_of(pg * cpp, cpp)
        r0 = pl.multiple_of(pg * page, page)
        for g in range(N_NSA_KV):
            for kv in range(2):
                x = cache_ref[0, pl.ds(kv * N_NSA_KV + g, page, stride=kinds), :]
                f, sec = _pool_chunks(x, wpos_ref[kv, :CMP_STRIDE, gsl[g]], wpos_ref[kv, CMP_STRIDE:, gsl[g]])
                first_ref[kv, pl.ds(c0, cpp), gsl[g]] = f
                second_ref[kv, pl.ds(c0, cpp), gsl[g]] = sec
            ks_ref[pl.ds(r0, page), gsl[g]] = cache_ref[0, pl.ds(2 * N_NSA_KV + g, page, stride=kinds), :].astype(BF16)
            vs_ref[pl.ds(r0, page), gsl[g]] = cache_ref[0, pl.ds(3 * N_NSA_KV + g, page, stride=kinds), :].astype(BF16)

    @pl.when(p == pl.num_programs(1) - 1)
    def _():
        new = newrows_ref[0]
        zpad = jnp.zeros((CMP_STRIDE - n_new, KV_W), F32)
        kvc = []
        for kv in range(2):
            xc = jnp.concatenate([new[:, kv * KV_W:(kv + 1) * KV_W], zpad], axis=0)
            first_ref[kv, past // CMP_STRIDE:past // CMP_STRIDE + 1, :] = jnp.sum(
                xc * wpos_ref[kv, :CMP_STRIDE, :], axis=0, keepdims=True)
            second_ref[kv, past // CMP_STRIDE:past // CMP_STRIDE + 1, :] = jnp.sum(
                xc * wpos_ref[kv, CMP_STRIDE:, :], axis=0, keepdims=True)
            pre = first_ref[kv] + pltpu.roll(second_ref[kv], n_cmp - 1, axis=0)
            kvc.append([_dot(pre[:, gsl[g]].astype(BF16), wc_ref[kv, g]).astype(BF16) for g in range(N_NSA_KV)])
        tail = jnp.zeros((n_keys - past - n_new, KV_W), F32)
        ks_ref[past:, :] = jnp.concatenate([new[:, 2 * KV_W:3 * KV_W], tail], axis=0).astype(BF16)
        vs_ref[past:, :] = jnp.concatenate([new[:, 3 * KV_W:4 * KV_W], tail], axis=0).astype(BF16)

        nwin = newwin_ref[0]
        wr = 2 * N_NSA_KV
        winout_ref[0, :(w_buf - n_new) * wr, :] = win_ref[0, n_new * wr:, :]
        for c in range(wr):
            winout_ref[0, pl.ds((w_buf - n_new) * wr + c, n_new, stride=wr), :] = nwin[:, c * HEAD_DIM:(c + 1) * HEAD_DIM]
        wtail = jnp.zeros((LANES - n_new, HEAD_DIM), F32)

        def window(c):
            return jnp.concatenate([win_ref[0, pl.ds(c, w_buf, stride=wr), :],
                                    nwin[:, c * HEAD_DIM:(c + 1) * HEAD_DIM], wtail], axis=0).astype(BF16)

        n_win = w_buf + LANES
        rows = NSA_GROUP * n_new
        r = lax.broadcasted_iota(jnp.int32, (rows, 1), 0)
        qpos = past + r % n_new
        qpos_q = past + lax.broadcasted_iota(jnp.int32, (n_new, 1), 0)
        kc_end = CMP_STRIDE * lax.broadcasted_iota(jnp.int32, (1, n_cmp), 1) + (CMP_BLOCK - 1)
        dist_s = qpos - lax.broadcasted_iota(jnp.int32, (1, n_keys), 1)
        dist_w = qpos - (past - w_buf + lax.broadcasted_iota(jnp.int32, (1, n_win), 1))
        gates = gate_ref[0]

        qgs, slopes, o_cmps, imps = [], [], [], []
        for g in range(N_NSA_KV):
            qg = jnp.concatenate(
                [q_ref[0, :, (g * NSA_GROUP + h) * HEAD_DIM:(g * NSA_GROUP + h + 1) * HEAD_DIM]
                 for h in range(NSA_GROUP)], axis=0)
            slope = _head_slope(g * NSA_GROUP + r // n_new)
            s = _dot_nt(qg, kvc[0][g]) * ATT_SCALE - slope * (qpos - kc_end).astype(F32)
            pc = _masked_softmax(s, kc_end <= qpos)
            o_cmps.append(_dot(pc.astype(BF16), kvc[1][g]))
            p4 = pc[0:n_new] + pc[n_new:2 * n_new] + pc[2 * n_new:3 * n_new] + pc[3 * n_new:4 * n_new]
            hi, lo = _split_bf16(p4)
            imps.append(_dot(hi, mimp_ref[...]) + _dot(lo, mimp_ref[...]))
            qgs.append(qg)
            slopes.append(slope)
        n_slc = -(-(past + n_new) // SLC_BLOCK)
        sel_all = _top_blocks_by_rank(
            _slc_scores(jnp.concatenate(imps, axis=0), jnp.concatenate([qpos_q] * N_NSA_KV, axis=0)), SLC_TOP, n_slc)

        for g in range(N_NSA_KV):
            qg, slope, o_cmp = qgs[g], slopes[g], o_cmps[g]
            sel = sel_all[g * n_new:(g + 1) * n_new]
            keep = _dot(jnp.concatenate([sel] * NSA_GROUP, axis=0).astype(BF16), e_ref[...]) > 0.5
            s = _dot_nt(qg, ks_ref[:, gsl[g]]) * ATT_SCALE - slope * dist_s.astype(F32)
            ps = _masked_softmax(s, (dist_s >= 0) & keep)
            o_slc = _dot(ps.astype(BF16), vs_ref[:, gsl[g]])
            s = _dot_nt(qg, window(g)) * ATT_SCALE - slope * dist_w.astype(F32)
            pw = _masked_softmax(s, (dist_w >= 0) & (dist_w < WINDOW))
            o_win = _dot(pw.astype(BF16), window(N_NSA_KV + g))

            o = (_gate_cols(gates, 0, g, n_new) * o_cmp + _gate_cols(gates, 1, g, n_new) * o_slc
                 + _gate_cols(gates, 2, g, n_new) * o_win)
            for h in range(NSA_GROUP):
                c = (g * NSA_GROUP + h) * HEAD_DIM
                o_ref[0, :, c:c + HEAD_DIM] = o[h * n_new:(h + 1) * n_new]


def _nsa_sample(page_table, qn, gates, nsarows, winrows, cache, state_win, wpos, wc, ppg=4):
    bs, n_new, _ = qn.shape
    n_pages = page_table.shape[1]
    page_rows = cache.shape[1]
    assert page_rows == LANES * 4 * N_NSA_KV and n_pages % ppg == 0
    past = n_pages * LANES
    win_rows = state_win.shape[1]
    n_keys = past + LANES
    n_cmp = 2 * (past // CMP_STRIDE)
    e = _block_expansion(n_keys)
    mimp = _importance_matrix(n_cmp, LANES)
    seq = lambda b, p, pt: (b, 0, 0)
    c3 = lambda b, p, pt: (0, 0, 0)

    def page_map(q):
        return lambda b, p, pt: (pt[b * n_pages + p * ppg + q], 0, 0)

    grid_spec = pltpu.PrefetchScalarGridSpec(
        num_scalar_prefetch=1,
        grid=(bs, n_pages // ppg),
        in_specs=[pl.BlockSpec((1, n_new, NSA_W), seq),
                  pl.BlockSpec((1, n_new, LANES), seq),
                  pl.BlockSpec((1, n_new, 4 * KV_W), seq),
                  pl.BlockSpec((1, n_new, 2 * KV_W), seq),
                  pl.BlockSpec((1, win_rows, HEAD_DIM), seq),
                  pl.BlockSpec(wpos.shape, c3),
                  pl.BlockSpec(wc.shape, lambda b, p, pt: (0, 0, 0, 0)),
                  pl.BlockSpec(e.shape, lambda b, p, pt: (0, 0)),
                  pl.BlockSpec(mimp.shape, lambda b, p, pt: (0, 0))]
                 + [pl.BlockSpec((1, page_rows, HEAD_DIM), page_map(q)) for q in range(ppg)],
        out_specs=[pl.BlockSpec((1, n_new, NSA_W), seq),
                   pl.BlockSpec((1, win_rows, HEAD_DIM), seq)],
        scratch_shapes=[pltpu.VMEM((2, n_cmp, KV_W), F32), pltpu.VMEM((2, n_cmp, KV_W), F32),
                        pltpu.VMEM((n_keys, KV_W), BF16), pltpu.VMEM((n_keys, KV_W), BF16)],
    )
    return pl.pallas_call(
        functools.partial(_nsa_sample_kernel, n_new=n_new, n_pages=n_pages, ppg=ppg),
        grid_spec=grid_spec,
        out_shape=[jax.ShapeDtypeStruct((bs, n_new, NSA_W), F32),
                   jax.ShapeDtypeStruct((bs, win_rows, HEAD_DIM), F32)],
        compiler_params=_cparams(("arbitrary", "arbitrary")),
        name="nsa_sample",
    )(page_table.reshape(-1), qn, gates, nsarows, winrows, state_win, wpos, wc, e, mimp, *([cache] * ppg))


def _mixout_kernel(osb_ref, on_ref, x_ref, gout_ref, w_ref, gate_ref, g2_ref, shift_ref, scale_ref, *rest):
    x1_ref, h2_ref, h2g_ref, nrm_ref = rest[-4:]
    tm = x_ref.shape[0]
    for h in range(N_SB_HEADS + N_NSA_HEADS):
        src = osb_ref if h < N_SB_HEADS else on_ref
        c = (h % N_SB_HEADS) * HEAD_DIM
        sl = slice(h * HEAD_DIM, (h + 1) * HEAD_DIM)
        nrm_ref[:, sl] = (_rms(src[:, c:c + HEAD_DIM]) * gout_ref[:, sl]).astype(BF16)
    x1 = x_ref[...] + gate_ref[...] * _dot(nrm_ref[...], w_ref[...])
    x1_ref[...] = x1
    h2 = _rms(x1) * g2_ref[...] * (1.0 + scale_ref[...]) + shift_ref[...]
    h2_ref[...] = h2.astype(BF16)
    for s in range(ROW_WORDS):
        lo = h2[:, (2 * s) * LANES:(2 * s + 1) * LANES].astype(BF16).astype(F32)
        hi = h2[:, (2 * s + 1) * LANES:(2 * s + 2) * LANES].astype(BF16).astype(F32)
        word = (lax.bitcast_convert_type(lo, U32) >> 16) | lax.bitcast_convert_type(hi, U32)
        h2g_ref[pl.ds(s, tm, stride=ROW_WORDS), :] = word


def _mixout(osb, on, x, gout, w_out, gate, g2, shift, scale, n_total, row0, prev=None, tm=256):
    m, d = x.shape
    blk0 = row0 // tm
    row = lambda i: (i, 0)
    out_row = lambda i: (blk0 + i, 0)
    const = lambda i: (0, 0)
    mod_spec = lambda a: pl.BlockSpec((tm, d), row) if a.shape[0] == m else pl.BlockSpec((1, d), const)
    in_specs = [pl.BlockSpec((tm, SB_W), row), pl.BlockSpec((tm, NSA_W), row), pl.BlockSpec((tm, d), row),
                pl.BlockSpec((1, d), const),
                pl.BlockSpec(w_out.shape, const, pipeline_mode=pl.Buffered(1)),
                mod_spec(gate), pl.BlockSpec((1, d), const), mod_spec(shift), mod_spec(scale)]
    args = [osb, on, x, gout, w_out, gate, g2, shift, scale]
    aliases = {}
    if prev is not None:
        in_specs += [pl.BlockSpec(memory_space=pl.ANY)] * 3
        aliases = {len(args) + j: j for j in range(3)}
        args += list(prev)
    return pl.pallas_call(
        _mixout_kernel,
        grid=(m // tm,),
        in_specs=in_specs,
        out_specs=[pl.BlockSpec((tm, d), out_row), pl.BlockSpec((tm, d), out_row),
                   pl.BlockSpec((tm * ROW_WORDS, LANES), out_row)],
        out_shape=[jax.ShapeDtypeStruct((n_total, d), F32), jax.ShapeDtypeStruct((n_total, d), BF16),
                   jax.ShapeDtypeStruct((n_total * ROW_WORDS, LANES), U32)],
        scratch_shapes=[pltpu.VMEM((tm, d), BF16)],
        input_output_aliases=aliases,
        compiler_params=_cparams(("arbitrary",)),
        name="mixout",
    )(*args)


def _first_max(v, idx, axis, n):
    m = jnp.max(v, axis=axis, keepdims=True)
    return m, jnp.min(jnp.where(v == m, idx, float(n)), axis=axis, keepdims=True)


def _router_kernel(h_ref, wrt_ref, bias_ref, eidx_ref, w_ref):
    tm = h_ref.shape[0]
    per = N_EXPERTS // N_EXPERT_GROUPS
    s = _sigmoid(_dot_nt(wrt_ref[...], h_ref[...]))
    biased = s + bias_ref[...]
    b3 = biased.reshape(N_EXPERT_GROUPS, per, tm)
    r3 = lax.broadcasted_iota(jnp.int32, b3.shape, 1).astype(F32)
    m1, i1 = _first_max(b3, r3, 1, per)
    m2 = jnp.max(jnp.where(r3 == i1, -jnp.inf, b3), axis=1, keepdims=True)
    gscore = (m1 + m2).reshape(N_EXPERT_GROUPS, tm)
    gi = lax.broadcasted_iota(jnp.int32, gscore.shape, 0).astype(F32)
    gmask = jnp.zeros(gscore.shape, F32)
    for _ in range(TOPK_GROUPS):
        _, idx = _first_max(gscore, gi, 0, N_EXPERT_GROUPS)
        pick = gi == idx
        gmask = jnp.where(pick, 1.0, gmask)
        gscore = jnp.where(pick, -jnp.inf, gscore)
    cand = jnp.where(gmask.reshape(N_EXPERT_GROUPS, 1, tm) > 0.5, b3, -jnp.inf).reshape(N_EXPERTS, tm)
    ei = lax.broadcasted_iota(jnp.int32, cand.shape, 0).astype(F32)
    ids, ws = [], []
    for _ in range(TOP_K):
        _, idx = _first_max(cand, ei, 0, N_EXPERTS)
        pick = ei == idx
        ids.append(idx)
        ws.append(jnp.sum(jnp.where(pick, s, 0.0), axis=0, keepdims=True))
        cand = jnp.where(pick, -jnp.inf, cand)
    w = jnp.concatenate(ws, axis=0)
    eidx_ref[...] = jnp.concatenate(ids, axis=0).astype(jnp.int32)
    w_ref[...] = w / jnp.sum(w, axis=0, keepdims=True) * ROUTED_SCALE


def _router(h2, w_router_t, router_bias, tm=512):
    m, d = h2.shape
    return pl.pallas_call(
        _router_kernel,
        grid=(m // tm,),
        in_specs=[pl.BlockSpec((tm, d), lambda i: (i, 0)),
                  pl.BlockSpec((N_EXPERTS, d), lambda i: (0, 0)),
                  pl.BlockSpec((N_EXPERTS, 1), lambda i: (0, 0))],
        out_specs=[pl.BlockSpec((TOP_K, tm), lambda i: (0, i)), pl.BlockSpec((TOP_K, tm), lambda i: (0, i))],
        out_shape=[jax.ShapeDtypeStruct((TOP_K, m), jnp.int32), jax.ShapeDtypeStruct((TOP_K, m), F32)],
        compiler_params=_cparams(("arbitrary",)),
        name="router",
    )(h2, w_router_t, router_bias.reshape(N_EXPERTS, 1))


def _dispatch_plan(eidx):
    t = eidx.shape[0]
    onehot = eidx[:, :, None] == jnp.arange(N_EXPERTS, dtype=jnp.int32)[None, None, :]
    routed = jnp.any(onehot, axis=1).astype(jnp.int32)
    incl = jnp.cumsum(routed, axis=0)
    counts = incl[-1]
    padded = (counts + MOE_ROWS - 1) // MOE_ROWS * MOE_ROWS
    pad_end = jnp.cumsum(padded)
    base = (pad_end - padded)[None, :] + incl - routed
    dest = jnp.sum(jnp.where(onehot, base[:, None, :], 0), axis=-1).astype(jnp.int32)
    n_blk = -(-(t * TOP_K) // MOE_ROWS) + N_EXPERTS
    blk_e = jnp.minimum(jnp.searchsorted(pad_end, jnp.arange(n_blk) * MOE_ROWS, side='right'),
                        N_EXPERTS - 1).astype(jnp.int32)
    n_active = (pad_end[-1] // MOE_ROWS).astype(jnp.int32)
    nxt_blk = (pad_end[blk_e] // MOE_ROWS).astype(jnp.int32)
    nxt_e = jnp.where(nxt_blk < n_active, blk_e[jnp.minimum(nxt_blk, n_blk - 1)], -1).astype(jnp.int32)
    return dest, blk_e, nxt_e, n_active.reshape(1), n_blk


def _dispatch_kernel(dest_ref, h2g_ref, xs_ref, sem, *, tm):
    i = pl.program_id(0)

    def body(r, carry):
        src = h2g_ref.at[pl.ds(pl.multiple_of(r * ROW_WORDS, ROW_WORDS), ROW_WORDS), :]
        for k in range(TOP_K):
            d = dest_ref[(i * tm + r) * TOP_K + k]
            pltpu.make_async_copy(src, xs_ref.at[pl.ds(pl.multiple_of(d * ROW_WORDS, ROW_WORDS), ROW_WORDS), :],
                                  sem).start()
        return carry

    lax.fori_loop(0, tm, body, 0)
    done = xs_ref.at[pl.ds(0, tm * TOP_K * ROW_WORDS), :]
    pltpu.make_async_copy(done, done, sem).wait()


def _dispatch(dest_flat, h2g, p_rows, tm=256):
    n_tok = h2g.shape[0] // ROW_WORDS
    grid_spec = pltpu.PrefetchScalarGridSpec(
        num_scalar_prefetch=1,
        grid=(n_tok // tm,),
        in_specs=[pl.BlockSpec((tm * ROW_WORDS, LANES), lambda i, d: (i, 0))],
        out_specs=pl.BlockSpec(memory_space=pl.ANY),
        scratch_shapes=[pltpu.SemaphoreType.DMA(())],
    )
    return pl.pallas_call(
        functools.partial(_dispatch_kernel, tm=tm),
        grid_spec=grid_spec,
        out_shape=jax.ShapeDtypeStruct((p_rows * ROW_WORDS, LANES), U32),
        compiler_params=_cparams(("arbitrary",)),
        name="dispatch",
    )(dest_flat, h2g)


def _experts_kernel(be_ref, nxt_ref, nact_ref, x_ref, wg_hbm, wu_hbm, wd_hbm, o_ref,
                    wgf, wuf, wdf, wgb, wub, wdb, sem):
    b = pl.program_id(0)
    active = b < nact_ref[0]
    e = be_ref[b]
    changed = e != be_ref[jnp.maximum(b - 1, 0)]

    def weight_copies(eid):
        cps = []
        for j, (src, dst) in enumerate(((wg_hbm, wgf), (wu_hbm, wuf), (wd_hbm, wdf))):
            step = dst.shape[0] // W_DMA_SPLIT
            for c in range(W_DMA_SPLIT):
                rs = pl.ds(c * step, step)
                cps.append(pltpu.make_async_copy(src.at[eid, rs, :], dst.at[rs, :], sem.at[j]))
        return cps

    @pl.when(active & (b == 0))
    def _():
        for cp in weight_copies(e):
            cp.start()

    @pl.when(active & ((b == 0) | changed))
    def _():
        for cp in weight_copies(e):
            cp.wait()
        ch = 256
        for c in range(0, D_MODEL, ch):
            wgb[c:c + ch, :] = wgf[c:c + ch, :].astype(BF16)
            wub[c:c + ch, :] = wuf[c:c + ch, :].astype(BF16)
        for c in range(0, D_EXPERT, ch // 4):
            wdb[c:c + ch // 4, :] = wdf[c:c + ch // 4, :].astype(BF16)
        nxt = nxt_ref[b]

        @pl.when(nxt >= 0)
        def _():
            for cp in weight_copies(nxt):
                cp.start()

    @pl.when(active)
    def _():
        chunks = []
        for s in range(ROW_WORDS):
            word = x_ref[pl.ds(s, MOE_ROWS, stride=ROW_WORDS), :]
            chunks.append(lax.bitcast_convert_type(word << 16, F32).astype(BF16))
            chunks.append(lax.bitcast_convert_type(word & jnp.uint32(0xFFFF0000), F32).astype(BF16))
        x = jnp.concatenate(chunks, axis=1)
        gt = _dot(x, wgb[...])
        a = (gt * _sigmoid(gt) * _dot(x, wub[...])).astype(BF16)
        y = _dot(a, wdb[...])
        for j in range(D_CHUNKS):
            o_ref[pl.ds(j, MOE_ROWS, stride=D_CHUNKS), :] = y[:, j * LANES:(j + 1) * LANES]


def _experts(blk_e, nxt_e, n_active, x_sorted, wg, wu, wd):
    d = wg.shape[1]
    n_blk = x_sorted.shape[0] // (MOE_ROWS * ROW_WORDS)
    hbm = pl.BlockSpec(memory_space=pl.ANY)
    grid_spec = pltpu.PrefetchScalarGridSpec(
        num_scalar_prefetch=3,
        grid=(n_blk,),
        in_specs=[pl.BlockSpec((MOE_ROWS * ROW_WORDS, LANES), lambda b, be, nx, na: (b, 0)), hbm, hbm, hbm],
        out_specs=pl.BlockSpec((MOE_ROWS * D_CHUNKS, LANES), lambda b, be, nx, na: (b, 0)),
        scratch_shapes=[pltpu.VMEM((d, D_EXPERT), F32), pltpu.VMEM((d, D_EXPERT), F32),
                        pltpu.VMEM((D_EXPERT, d), F32),
                        pltpu.VMEM((d, D_EXPERT), BF16), pltpu.VMEM((d, D_EXPERT), BF16),
                        pltpu.VMEM((D_EXPERT, d), BF16),
                        pltpu.SemaphoreType.DMA((3,))],
    )
    return pl.pallas_call(
        _experts_kernel,
        grid_spec=grid_spec,
        out_shape=jax.ShapeDtypeStruct((n_blk * MOE_ROWS * D_CHUNKS, LANES), F32),
        compiler_params=_cparams(("arbitrary",)),
        name="experts",
    )(blk_e, nxt_e, n_active, x_sorted, wg, wu, wd)


def _final_kernel(dest_ref, x1_ref, h2_ref, w8_ref, ys_ref, wg_ref, wu_ref, wd_ref, gate_ref, gf_ref,
                  shift_ref, scale_ref, y_ref, ybuf, ymoe_ref, sem, *, tm, row0):
    i = pl.program_id(0)
    n = pl.num_programs(0)
    tile_rows = tm * TOP_K * D_CHUNKS

    def issue(tile, slot):
        def body(r, carry):
            for k in range(TOP_K):
                d = dest_ref[(row0 + tile * tm + r) * TOP_K + k]
                pltpu.make_async_copy(
                    ys_ref.at[pl.ds(pl.multiple_of(d * D_CHUNKS, D_CHUNKS), D_CHUNKS), :],
                    ybuf.at[slot, pl.ds(pl.multiple_of((r * TOP_K + k) * D_CHUNKS, D_CHUNKS), D_CHUNKS), :],
                    sem.at[slot]).start()
            return carry
        lax.fori_loop(0, tm, body, 0)

    @pl.when(i == 0)
    def _():
        issue(0, 0)

    @pl.when(i + 1 < n)
    def _():
        issue(i + 1, (i + 1) % 2)

    slot = i % 2
    pltpu.make_async_copy(ys_ref.at[pl.ds(0, tile_rows), :], ybuf.at[slot], sem.at[slot]).wait()
    w8 = w8_ref[...]
    wk = [w8[:, k:k + 1] for k in range(TOP_K)]
    for j in range(D_CHUNKS):
        acc = None
        for k in range(TOP_K):
            v = ybuf[slot, pl.ds(k * D_CHUNKS + j, tm, stride=TOP_K * D_CHUNKS), :] * wk[k]
            acc = v if acc is None else acc + v
        ymoe_ref[:, j * LANES:(j + 1) * LANES] = acc

    h = h2_ref[...]
    gt = _dot(h, wg_ref[...])
    a = (gt * _sigmoid(gt) * _dot(h, wu_ref[...])).astype(BF16)
    f = ymoe_ref[...] + _dot(a, wd_ref[...])
    x2 = x1_ref[...] + gate_ref[...] * f
    y_ref[...] = _rms(x2) * gf_ref[...] * (1.0 + scale_ref[...]) + shift_ref[...]


def _final(dest_flat, x1, h2, w8, y_sorted, wg, wu, wd, gate, gf, shift, scale, m, row0, tm=64):
    d = x1.shape[1]
    blk0 = row0 // tm
    in_row = lambda i, dr: (blk0 + i, 0)
    row = lambda i, dr: (i, 0)
    const = lambda i, dr: (0, 0)
    mod_spec = lambda a: pl.BlockSpec((tm, d), row) if a.shape[0] == m else pl.BlockSpec((1, d), const)
    grid_spec = pltpu.PrefetchScalarGridSpec(
        num_scalar_prefetch=1,
        grid=(m // tm,),
        in_specs=[pl.BlockSpec((tm, d), in_row), pl.BlockSpec((tm, d), in_row), pl.BlockSpec((tm, TOP_K), in_row),
                  pl.BlockSpec(memory_space=pl.ANY),
                  pl.BlockSpec(wg.shape, const), pl.BlockSpec(wu.shape, const), pl.BlockSpec(wd.shape, const),
                  mod_spec(gate), pl.BlockSpec((1, d), const), mod_spec(shift), mod_spec(scale)],
        out_specs=pl.BlockSpec((tm, d), row),
        scratch_shapes=[pltpu.VMEM((2, tm * TOP_K * D_CHUNKS, LANES), F32), pltpu.VMEM((tm, d), F32),
                        pltpu.SemaphoreType.DMA((2,))],
    )
    return pl.pallas_call(
        functools.partial(_final_kernel, tm=tm, row0=row0),
        grid_spec=grid_spec,
        out_shape=jax.ShapeDtypeStruct((m, d), F32),
        compiler_params=_cparams(("arbitrary",)),
        name="final",
    )(dest_flat, x1, h2, w8, y_sorted, wg, wu, wd, gate, gf, shift, scale)


def kernel(x_prompt, x_sample, cache_sb, cache_nsa, state_win, page_table, c_prompt, c_sample, w_ada, b_ada, norm1_g, w_in, cmp_pos_k, cmp_pos_v, cmp_wk, cmp_wv, out_norm_g, w_out, norm2_g, w_router, router_bias, w_gate_e, w_up_e, w_down_e, w_gate_s, w_up_s, w_down_s, normf_g):
    b_p, t, d = x_prompt.shape
    bs, n_new, _ = x_sample.shape
    assert b_p == 1 and d == D_MODEL
    n_s = bs * n_new
    n_tok = t + n_s

    c_all = jnp.concatenate([c_prompt, c_sample], axis=0)
    m_pad = -(-c_all.shape[0] // 8) * 8
    c_all = jnp.pad(c_all, ((0, m_pad - c_all.shape[0]), (0, 0)))
    mod = _ada(c_all, w_ada, b_ada).reshape(m_pad, N_ADA, d)
    mod_p = [mod[0:1, i] for i in range(N_ADA)]
    mod_s = [jnp.repeat(mod[1:1 + bs, i], n_new, axis=0) for i in range(N_ADA)]

    w_main = w_in[:, :COL_GATE]
    w_gate = jnp.pad(w_in[:, COL_GATE:], ((0, 0), (0, LANES - N_GATE)))
    w_cat = jnp.concatenate([w_main, w_gate], axis=1).astype(BF16)
    row1 = lambda v: v.reshape(1, -1)

    xp = x_prompt.reshape(t, d)
    xs = x_sample.reshape(n_s, d)
    (qsb_p, sbrows_p, kvsb_p, qn_p, nsarows_p, nsab_p, winrows_p, winb_p, gates_p) = _project(
        xp, row1(norm1_g), mod_p[0], mod_p[1], w_cat)
    (qsb_s, sbrows_s, _, qn_s, nsarows_s, _, winrows_s, _, gates_s) = _project(
        xs, row1(norm1_g), mod_s[0], mod_s[1], w_cat)

    osb_p = _sb_prompt(qsb_p, kvsb_p)
    wpos = jnp.stack([jnp.repeat(cmp_pos_k, HEAD_DIM, axis=1), jnp.repeat(cmp_pos_v, HEAD_DIM, axis=1)])
    wc = jnp.stack([cmp_wk, cmp_wv]).astype(BF16)
    kvc_p = _compress_prompt(nsarows_p, wpos, wc)
    on_p = _nsa_prompt_t(qn_p, gates_p, kvc_p, nsab_p, winb_p)

    per_seq = lambda a: a.reshape(bs, n_new, a.shape[-1])
    n_phys = cache_sb.shape[0]
    w_buf = state_win.shape[1]
    osb_s = _sb_sample(page_table, per_seq(qsb_s), per_seq(sbrows_s), cache_sb.reshape(n_phys, -1, HEAD_DIM))
    on_s, win_s = _nsa_sample(page_table, per_seq(qn_s), per_seq(gates_s), per_seq(nsarows_s), per_seq(winrows_s),
                              cache_nsa.reshape(n_phys, -1, HEAD_DIM), state_win.reshape(bs, -1, HEAD_DIM), wpos, wc)

    w_out_b = w_out.astype(BF16)
    gout = out_norm_g.reshape(1, -1)
    bufs = _mixout(osb_p, on_p, xp, gout, w_out_b, mod_p[2], row1(norm2_g), mod_p[3], mod_p[4], n_tok, 0)
    x1, h2, h2g = _mixout(osb_s.reshape(n_s, SB_W), on_s.reshape(n_s, NSA_W), xs, gout, w_out_b,
                          mod_s[2], row1(norm2_g), mod_s[3], mod_s[4], n_tok, t, prev=bufs)

    eidx_t, w_t = _router(h2, w_router.T.astype(BF16), router_bias)
    dest, blk_e, nxt_e, n_active, n_blk = _dispatch_plan(eidx_t.T)
    dest_flat = dest.reshape(-1)
    x_sorted = _dispatch(dest_flat, h2g, n_blk * MOE_ROWS)
    y_sorted = _experts(blk_e, nxt_e, n_active, x_sorted, w_gate_e, w_up_e, w_down_e)
    w8 = w_t.T
    ws = (w_gate_s.astype(BF16), w_up_s.astype(BF16), w_down_s.astype(BF16))
    y_p = _final(dest_flat, x1, h2, w8, y_sorted, *ws, mod_p[5], row1(normf_g), mod_p[6], mod_p[7], t, 0)
    y_s = _final(dest_flat, x1, h2, w8, y_sorted, *ws, mod_s[5], row1(normf_g), mod_s[6], mod_s[7], n_s, t)

    keep = min(WINDOW, t)
    return (y_p.reshape(1, t, d), y_s.reshape(bs, n_new, d),
            sbrows_p.reshape(1, t, 2, N_SB_HEADS, HEAD_DIM), sbrows_s.reshape(bs, n_new, 2, N_SB_HEADS, HEAD_DIM),
            nsarows_p.reshape(1, t, 4, N_NSA_KV, HEAD_DIM), nsarows_s.reshape(bs, n_new, 4, N_NSA_KV, HEAD_DIM),
            winrows_p[t - keep:].reshape(1, keep, 2, N_NSA_KV, HEAD_DIM),
            win_s.reshape(bs, w_buf, 2, N_NSA_KV, HEAD_DIM))
```

```python
import functools

import numpy as np
import jax
import jax.numpy as jnp
from jax import lax
from jax.experimental import pallas as pl
from jax.experimental.pallas import tpu as pltpu

F32 = jnp.float32
BF16 = jnp.bfloat16
U32 = jnp.uint32

D_MODEL = 2048
HEAD_DIM = 128
N_SB_HEADS = 8
N_NSA_HEADS = 8
N_NSA_KV = 2
NSA_GROUP = N_NSA_HEADS // N_NSA_KV
CMP_BLOCK = 32
CMP_STRIDE = 16
SLC_BLOCK = 64
SLC_TOP = 16
WINDOW = 512
FORCE = 1e4
NEG = -1e30
N_EXPERTS = 256
TOP_K = 8
N_EXPERT_GROUPS = 8
TOPK_GROUPS = 4
D_EXPERT = D_MODEL // 4
ROUTED_SCALE = 2.5
MOE_ROWS = 128
N_ADA = 8
EPS = 1e-6
ATT_SCALE = HEAD_DIM ** -0.5

SB_W = N_SB_HEADS * HEAD_DIM
NSA_W = N_NSA_HEADS * HEAD_DIM
KV_W = N_NSA_KV * HEAD_DIM
COL_QSB = 0
COL_KVSB = SB_W
COL_QN = 3 * SB_W
COL_NSA = COL_QN + NSA_W
COL_WIN = COL_NSA + 4 * KV_W
COL_GATE = COL_WIN + 2 * KV_W
N_GATE = 3 * N_NSA_HEADS
LANES = 128
V7X_VMEM_LIMIT = 56 * 1024 * 1024
D_CHUNKS = D_MODEL // LANES
ROW_WORDS = D_CHUNKS // 2

NT = (((1,), (1,)), ((), ()))


def _cparams(sem):
    return pltpu.CompilerParams(dimension_semantics=sem, vmem_limit_bytes=V7X_VMEM_LIMIT)


def _sigmoid(x):
    return 1.0 / (1.0 + jnp.exp(-x))


def _split_bf16(x):
    hi = x.astype(BF16)
    lo = (x - hi.astype(F32)).astype(BF16)
    return hi, lo


def _dot(a, b):
    return jnp.dot(a, b, preferred_element_type=F32)


def _dot_nt(a, b):
    return lax.dot_general(a, b, NT, preferred_element_type=F32)


def _rms(x):
    return x * lax.rsqrt(jnp.mean(x * x, axis=-1, keepdims=True) + EPS)


def _masked_softmax(s, mask):
    s = jnp.where(mask, s, NEG)
    m = jnp.max(s, axis=-1, keepdims=True)
    e = jnp.where(mask, jnp.exp(s - m), 0.0)
    return e / jnp.maximum(jnp.sum(e, axis=-1, keepdims=True), 1e-30)


def _head_slope(head):
    out = jnp.full(head.shape, 2.0 ** -N_NSA_HEADS, F32)
    for i in range(N_NSA_HEADS - 1):
        out = jnp.where(head == i, 2.0 ** -(i + 1), out)
    return out


def _top_blocks_by_rank(score, n_top, n_cand):
    blk = lax.broadcasted_iota(jnp.int32, score.shape, 1)
    rank = jnp.zeros(score.shape, F32)
    for i in range(n_cand):
        c = score[:, i:i + 1]
        tie = jnp.where(blk > i, 1.0, 0.0)
        rank = rank + jnp.where(c > score, 1.0, jnp.where(c == score, tie, 0.0))
    return jnp.where(rank < n_top, 1.0, 0.0)


def _slc_scores(imp, qpos):
    blk = lax.broadcasted_iota(jnp.int32, imp.shape, 1)
    qb = qpos // SLC_BLOCK
    forced = (blk == 0) | ((blk <= qb) & (blk >= qb - 1))
    return jnp.where(blk > qb, -1.0, jnp.where(forced, FORCE, imp))


def _gate_cols(gates, branch, group, n_rows_per_head):
    lane = lax.broadcasted_iota(jnp.int32, gates.shape, 1)
    cols = [jnp.sum(jnp.where(lane == branch * N_NSA_HEADS + group * NSA_GROUP + h, gates, 0.0),
                    axis=-1, keepdims=True) for h in range(NSA_GROUP)]
    return jnp.concatenate(cols, axis=0)


def _ada_kernel(c_ref, w_ref, b_ref, o_ref):
    c = c_ref[...]
    a = (c * _sigmoid(c)).astype(BF16)
    o_ref[...] = _dot(a, w_ref[...].astype(BF16)) + b_ref[...]


def _ada(c, w_ada, b_ada, tn=1024):
    m, d = c.shape
    n = w_ada.shape[1]
    return pl.pallas_call(
        _ada_kernel,
        grid=(n // tn,),
        in_specs=[pl.BlockSpec((m, d), lambda j: (0, 0)),
                  pl.BlockSpec((d, tn), lambda j: (0, j)),
                  pl.BlockSpec((1, tn), lambda j: (0, j))],
        out_specs=pl.BlockSpec((m, tn), lambda j: (0, j)),
        out_shape=jax.ShapeDtypeStruct((m, n), F32),
        compiler_params=_cparams(("arbitrary",)),
        name="ada",
    )(c, w_ada, b_ada.reshape(1, n))


def _proj_kernel(x_ref, g_ref, shift_ref, scale_ref, w_ref,
                 qsb_ref, sbrows_ref, kvsb_ref, qn_ref, nsarows_ref, nsab_ref,
                 winrows_ref, winb_ref, gates_ref):
    h = (_rms(x_ref[...]) * g_ref[...] * (1.0 + scale_ref[...]) + shift_ref[...]).astype(BF16)
    ch = 512

    def mm(c0, n):
        return _dot(h, w_ref[:, c0:c0 + n])

    for c in range(0, SB_W, ch):
        qsb_ref[:, c:c + ch] = mm(COL_QSB + c, ch).astype(BF16)
    for c in range(0, 2 * SB_W, ch):
        r = mm(COL_KVSB + c, ch)
        sbrows_ref[:, c:c + ch] = r
        kvsb_ref[:, c:c + ch] = r.astype(BF16)
    for c in range(0, NSA_W, ch):
        qn_ref[:, c:c + ch] = mm(COL_QN + c, ch).astype(BF16)
    for c in range(0, 4 * KV_W, ch):
        r = mm(COL_NSA + c, ch)
        nsarows_ref[:, c:c + ch] = r
        nsab_ref[:, c:c + ch] = r.astype(BF16)
    r = mm(COL_WIN, 2 * KV_W)
    winrows_ref[...] = r
    winb_ref[...] = r.astype(BF16)
    gates_ref[...] = _sigmoid(mm(COL_GATE, LANES))


def _project(x, g, shift, scale, w_cat, tm=256):
    m, d = x.shape
    row = lambda i: (i, 0)
    const = lambda i: (0, 0)
    mod_spec = lambda a: pl.BlockSpec((tm, d), row) if a.shape[0] == m else pl.BlockSpec((1, d), const)
    widths = [(SB_W, BF16), (2 * SB_W, F32), (2 * SB_W, BF16), (NSA_W, BF16), (4 * KV_W, F32),
              (4 * KV_W, BF16), (2 * KV_W, F32), (2 * KV_W, BF16), (LANES, F32)]
    return pl.pallas_call(
        _proj_kernel,
        grid=(m // tm,),
        in_specs=[pl.BlockSpec((tm, d), row),
                  pl.BlockSpec((1, d), const),
                  mod_spec(shift), mod_spec(scale),
                  pl.BlockSpec(w_cat.shape, const, pipeline_mode=pl.Buffered(1))],
        out_specs=[pl.BlockSpec((tm, w), row) for w, _ in widths],
        out_shape=[jax.ShapeDtypeStruct((m, w), dt) for w, dt in widths],
        compiler_params=_cparams(("arbitrary",)),
        name="project",
    )(x, g, shift, scale, w_cat)


def _sb_logits(z):
    l1m = jnp.minimum(-z, 0.0) - jnp.log(1.0 + jnp.exp(-jnp.abs(z)))
    return l1m + z, l1m


def _sb_prompt_kernel(q_ref, k_ref, v_ref, o_ref, *, tq, heads):
    i = pl.program_id(1)
    row = lax.broadcasted_iota(jnp.int32, (2 * tq, tq), 0)
    col = lax.broadcasted_iota(jnp.int32, (2 * tq, tq), 1)
    later2 = jnp.where(jnp.where(row >= tq, row - tq, row) > col, 1.0, 0.0).astype(BF16)
    causal = lax.broadcasted_iota(jnp.int32, (tq, tq), 1) < lax.broadcasted_iota(jnp.int32, (tq, tq), 0)

    hsl = [slice(hh * HEAD_DIM, (hh + 1) * HEAD_DIM) for hh in range(heads)]

    def block(j, state, diagonal):
        off = pl.multiple_of(j * tq, tq)
        zs = [_dot_nt(q_ref[:, hs], k_ref[pl.ds(off, tq), hs]) * ATT_SCALE for hs in hsl]
        logits = [_sb_logits(z) for z in zs]
        if diagonal:
            logits = [(ls, jnp.where(causal, l1m, 0.0)) for ls, l1m in logits]
        split = [jnp.concatenate(_split_bf16(l1m), axis=1) for _, l1m in logits]
        suffix = [_dot(sp, later2) + state[hh][0] for hh, sp in enumerate(split)]
        a = [jnp.exp(logits[hh][0] + suffix[hh]) for hh in range(heads)]
        if diagonal:
            a = [jnp.where(causal, x, 0.0) for x in a]
        acc = [state[hh][1] + _dot(a[hh].astype(BF16), v_ref[pl.ds(off, tq), hsl[hh]]) for hh in range(heads)]
        return tuple((suffix[hh][:, :1] + logits[hh][1][:, :1], acc[hh]) for hh in range(heads))

    init = tuple((jnp.zeros((tq, 1), F32), jnp.zeros((tq, HEAD_DIM), F32)) for _ in range(heads))
    state = block(i, init, True)
    state = lax.fori_loop(0, i, lambda t, st: block(i - 1 - t, st, False), state)
    for hh in range(heads):
        o_ref[:, hh * HEAD_DIM:(hh + 1) * HEAD_DIM] = state[hh][1]


def _sb_prompt(qsb, kvsb, tq=256, heads=4):
    t = qsb.shape[0]
    w = heads * HEAD_DIM
    n_hp = N_SB_HEADS // heads
    return pl.pallas_call(
        functools.partial(_sb_prompt_kernel, tq=tq, heads=heads),
        grid=(n_hp, t // tq),
        in_specs=[pl.BlockSpec((tq, w), lambda h, i: (i, h)),
                  pl.BlockSpec((t, w), lambda h, i: (0, h)),
                  pl.BlockSpec((t, w), lambda h, i: (0, n_hp + h))],
        out_specs=pl.BlockSpec((tq, w), lambda h, i: (i, h)),
        out_shape=jax.ShapeDtypeStruct((t, SB_W), F32),
        compiler_params=_cparams(("arbitrary", "arbitrary")),
        name="sb_prompt",
    )(qsb, kvsb, kvsb)


def _sb_sample_kernel(pt_ref, q_ref, new_ref, *refs, n_new, ppg):
    cache_refs = refs[:ppg]
    o_ref, qbd_ref, carry_ref, acc_ref = refs[ppg:]
    p = pl.program_id(1)
    rows = N_SB_HEADS * n_new
    kk = lax.broadcasted_iota(jnp.int32, (2 * LANES, 2 * LANES), 0)
    cc = lax.broadcasted_iota(jnp.int32, (2 * LANES, 2 * LANES), 1)
    kk = jnp.where(kk >= LANES, kk - LANES, kk)
    later_and_all = jnp.where((cc >= LANES) | (kk > cc), 1.0, 0.0).astype(BF16)
    stride = 2 * N_SB_HEADS
    hsl = [slice(h * HEAD_DIM, (h + 1) * HEAD_DIM) for h in range(N_SB_HEADS)]
    rsl = [slice(h * n_new, (h + 1) * n_new) for h in range(N_SB_HEADS)]

    def scores(kmat, diagonal):
        z = _dot_nt(qbd_ref[...].astype(BF16), kmat) * ATT_SCALE
        ls, l1m = _sb_logits(z)
        causal = None
        if diagonal:
            key = lax.broadcasted_iota(jnp.int32, (rows, LANES), 1)
            qry = lax.broadcasted_iota(jnp.int32, (rows, LANES), 0) % n_new
            causal = key < qry
            l1m = jnp.where(causal, l1m, 0.0)
        hi, lo = _split_bf16(l1m)
        return ls, _dot(jnp.concatenate([hi, lo], axis=1), later_and_all), causal

    def accumulate(ls, st, causal, vmat, carry):
        a = jnp.exp(ls + st[:, :LANES] + carry)
        if causal is not None:
            a = jnp.where(causal, a, 0.0)
        o_all = _dot(a.astype(BF16), vmat)
        for h in range(N_SB_HEADS):
            acc_ref[rsl[h], :] += o_all[rsl[h], hsl[h]]
        return carry + st[:, LANES:]

    @pl.when(p == 0)
    def _():
        acc_ref[...] = jnp.zeros_like(acc_ref)
        qbd_ref[...] = jnp.zeros_like(qbd_ref)
        for h in range(N_SB_HEADS):
            qbd_ref[rsl[h], hsl[h]] = q_ref[0, :, hsl[h]].astype(F32)
        pad = jnp.zeros((LANES - n_new, SB_W), F32)
        knew = jnp.concatenate([new_ref[0, :, :SB_W], pad], axis=0).astype(BF16)
        vnew = jnp.concatenate([new_ref[0, :, SB_W:], pad], axis=0).astype(BF16)
        ls, st, causal = scores(knew, True)
        carry_ref[...] = accumulate(ls, st, causal, vnew, jnp.zeros((rows, LANES), F32))

    def page_mat(ref, c0):
        return jnp.concatenate([ref[0, pl.ds(c0 + h, LANES, stride=stride), :].astype(BF16)
                                for h in range(N_SB_HEADS)], axis=1)

    pre = [scores(page_mat(ref, 0), False) for ref in cache_refs]
    carry = carry_ref[...]
    for ref, (ls, st, _) in zip(cache_refs, pre):
        carry = accumulate(ls, st, None, page_mat(ref, N_SB_HEADS), carry)
    carry_ref[...] = carry

    @pl.when(p == pl.num_programs(1) - 1)
    def _():
        for h in range(N_SB_HEADS):
            o_ref[0, :, hsl[h]] = acc_ref[rsl[h], :]


def _sb_sample(page_table, qsb, sbrows, cache, ppg=8):
    bs, n_new, _ = qsb.shape
    n_pages = page_table.shape[1]
    page_rows = cache.shape[1]
    assert page_rows == LANES * 2 * N_SB_HEADS and n_pages % ppg == 0
    rows = N_SB_HEADS * n_new

    def page_map(q):
        return lambda b, p, pt: (pt[b * n_pages + n_pages - 1 - (p * ppg + q)], 0, 0)

    grid_spec = pltpu.PrefetchScalarGridSpec(
        num_scalar_prefetch=1,
        grid=(bs, n_pages // ppg),
        in_specs=[pl.BlockSpec((1, n_new, SB_W), lambda b, p, pt: (b, 0, 0)),
                  pl.BlockSpec((1, n_new, 2 * SB_W), lambda b, p, pt: (b, 0, 0))]
                 + [pl.BlockSpec((1, page_rows, HEAD_DIM), page_map(q)) for q in range(ppg)],
        out_specs=pl.BlockSpec((1, n_new, SB_W), lambda b, p, pt: (b, 0, 0)),
        scratch_shapes=[pltpu.VMEM((rows, SB_W), F32), pltpu.VMEM((rows, LANES), F32),
                        pltpu.VMEM((rows, HEAD_DIM), F32)],
    )
    return pl.pallas_call(
        functools.partial(_sb_sample_kernel, n_new=n_new, ppg=ppg),
        grid_spec=grid_spec,
        out_shape=jax.ShapeDtypeStruct((bs, n_new, SB_W), F32),
        compiler_params=_cparams(("arbitrary", "arbitrary")),
        name="sb_sample",
    )(page_table.reshape(-1), qsb, sbrows, *([cache] * ppg))


def _pool_chunks(x, w1, w2):
    x3 = x.reshape(x.shape[0] // CMP_STRIDE, CMP_STRIDE, x.shape[1])
    return jnp.sum(x3 * w1[None], axis=1), jnp.sum(x3 * w2[None], axis=1)


def _compress_kernel(rows_ref, wpos_ref, wc_ref, o_ref, first_ref, second_ref, *, step_rows):
    n_rows = rows_ref.shape[0]
    n_chunks = n_rows // CMP_STRIDE
    w1 = wpos_ref[0, :CMP_STRIDE, :]
    w2 = wpos_ref[0, CMP_STRIDE:, :]
    cps = step_rows // CMP_STRIDE

    def body(s, carry):
        r0 = pl.multiple_of(s * step_rows, step_rows)
        c0 = pl.multiple_of(s * cps, cps)
        f, sec = _pool_chunks(rows_ref[pl.ds(r0, step_rows), :], w1, w2)
        first_ref[pl.ds(c0, cps), :] = f
        second_ref[pl.ds(c0, cps), :] = sec
        return carry

    lax.fori_loop(0, n_rows // step_rows, body, 0)
    pre = first_ref[...] + pltpu.roll(second_ref[...], n_chunks - 1, axis=0)
    for g in range(N_NSA_KV):
        sl = slice(g * HEAD_DIM, (g + 1) * HEAD_DIM)
        o_ref[0, :, sl] = _dot(pre[:, sl].astype(BF16), wc_ref[0, g]).astype(BF16)


def _compress_prompt(nsarows, wpos, wc, step_rows=1024):
    t = nsarows.shape[0]
    n_chunks = t // CMP_STRIDE
    return pl.pallas_call(
        functools.partial(_compress_kernel, step_rows=step_rows),
        grid=(2,),
        in_specs=[pl.BlockSpec((t, KV_W), lambda c: (0, c)),
                  pl.BlockSpec((1, CMP_BLOCK, KV_W), lambda c: (c, 0, 0)),
                  pl.BlockSpec((1, N_NSA_KV, HEAD_DIM, HEAD_DIM), lambda c: (c, 0, 0, 0))],
        out_specs=pl.BlockSpec((1, n_chunks, KV_W), lambda c: (c, 0, 0)),
        out_shape=jax.ShapeDtypeStruct((2, n_chunks, KV_W), BF16),
        scratch_shapes=[pltpu.VMEM((n_chunks, KV_W), F32), pltpu.VMEM((n_chunks, KV_W), F32)],
        compiler_params=_cparams(("arbitrary",)),
        name="compress_prompt",
    )(nsarows, wpos, wc)


def _importance_matrix(n_cmp, n_slc_lanes):
    m = np.arange(n_cmp)[:, None]
    j = np.arange(n_slc_lanes)[None, :]
    d = m - (SLC_BLOCK // CMP_STRIDE) * j
    w = np.where((d >= 0) & (d <= 2), 1.0, np.where((d == -1) | (d == 3), 0.5, 0.0))
    return jnp.asarray(w, BF16)


def _block_expansion(n_keys):
    b = np.arange(LANES)[:, None]
    k = np.arange(n_keys)[None, :]
    return jnp.asarray((b == k // SLC_BLOCK).astype(np.float32), BF16)


def _top_blocks_t(score, n_top):
    blk = lax.broadcasted_iota(jnp.int32, score.shape, 0).astype(F32)
    sel = jnp.zeros(score.shape, F32)
    for _ in range(n_top):
        m = jnp.max(score, axis=0, keepdims=True)
        idx = jnp.min(jnp.where(score == m, blk, float(score.shape[0])), axis=0, keepdims=True)
        pick = blk == idx
        sel = jnp.where(pick, 1.0, sel)
        score = jnp.where(pick, -2.0, score)
    return sel


def _nsa_prompt_t_kernel(q_ref, gate_ref, kc_ref, vct_ref, sk_ref, svt_ref, wk_ref, wvt_ref,
                         et_ref, mimpt_ref, o_ref, cb_ref, *, tq, tk):
    i = pl.program_id(0)
    start = i * tq
    cols = NSA_GROUP * tq
    groups = range(N_NSA_KV)
    gsl = [slice(g * HEAD_DIM, (g + 1) * HEAD_DIM) for g in groups]
    hsl = [slice(h * tq, (h + 1) * tq) for h in range(NSA_GROUP)]
    slope_of = lambda g, h: 2.0 ** -(g * NSA_GROUP + h + 1)
    drop = 2.0 * NEG
    n_cmp = kc_ref.shape[1]

    @pl.when(i == 0)
    def _():
        kio = lax.broadcasted_iota(jnp.int32, (tk, tq), 0).astype(F32)
        for hh in range(N_NSA_HEADS):
            cb_ref[hh] = (2.0 ** -(hh + 1)) * kio

    qpos = start + lax.broadcasted_iota(jnp.int32, (1, tq), 1)
    qpos_f = qpos.astype(F32)

    def head_row(fn):
        return jnp.concatenate([fn(h) for h in range(NSA_GROUP)], axis=1)

    qg, o_cmp, selb = [], [], []
    dist_c = qpos - (CMP_STRIDE * lax.broadcasted_iota(jnp.int32, (n_cmp, tq), 0) + (CMP_BLOCK - 1))
    seen_c = dist_c >= 0
    dist_cf = dist_c.astype(F32)
    for g in groups:
        qg.append(jnp.concatenate(
            [q_ref[:, (g * NSA_GROUP + h) * HEAD_DIM:(g * NSA_GROUP + h + 1) * HEAD_DIM] for h in range(NSA_GROUP)],
            axis=0))
        zc = _dot_nt(kc_ref[0, :, gsl[g]], qg[g]) * ATT_SCALE
        ps = []
        for h in range(NSA_GROUP):
            s = jnp.where(seen_c, zc[:, hsl[h]] - slope_of(g, h) * dist_cf, NEG)
            e = jnp.where(seen_c, jnp.exp(s - jnp.max(s, axis=0, keepdims=True)), 0.0)
            ps.append(e / jnp.maximum(jnp.sum(e, axis=0, keepdims=True), 1e-30))
        o_cmp.append(_dot(vct_ref[gsl[g], :], jnp.concatenate(ps, axis=1).astype(BF16)))
        hi, lo = _split_bf16(ps[0] + ps[1] + ps[2] + ps[3])
        imp = _dot(mimpt_ref[...], hi) + _dot(mimpt_ref[...], lo)
        blk = lax.broadcasted_iota(jnp.int32, imp.shape, 0)
        qb = qpos // SLC_BLOCK
        forced = (blk == 0) | ((blk <= qb) & (blk >= qb - 1))
        score = jnp.where(blk > qb, -1.0, jnp.where(forced, FORCE, imp))
        selb.append(((_top_blocks_t(score, SLC_TOP) - 1.0) * (-drop)).astype(BF16))

    def attend(k_ref, vt_ref, t_lo, t_hi, selected):
        def tile(t, state, edge):
            off = pl.multiple_of(t * tk, tk)
            off_f = off.astype(F32)
            zs = [_dot_nt(k_ref[pl.ds(off, tk), gsl[g]], qg[g]) for g in groups]
            mbs = [_dot(et_ref[t], selb[g]) for g in groups] if selected else None
            if edge:
                dist = qpos - (off + lax.broadcasted_iota(jnp.int32, (tk, tq), 0))
                ok = (dist >= 0) if selected else ((dist >= 0) & (dist < WINDOW))
            out = []
            for g in groups:
                parts = []
                for h in range(NSA_GROUP):
                    sh = zs[g][:, hsl[h]] * ATT_SCALE + cb_ref[g * NSA_GROUP + h]
                    if selected:
                        sh = sh + mbs[g]
                    if edge:
                        sh = jnp.where(ok, sh, drop)
                    parts.append(sh)
                s = jnp.concatenate(parts, axis=1)
                shift = head_row(lambda h: slope_of(g, h) * (qpos_f - off_f))
                m_old, l_old, acc_old = state[g]
                m_new = jnp.maximum(m_old, jnp.max(s, axis=0, keepdims=True) - shift)
                e = jnp.exp(s - (shift + m_new))
                alpha = jnp.exp(m_old - m_new)
                pv = _dot(vt_ref[t, gsl[g], :], e.astype(BF16))
                out.append((m_new, l_old * alpha + jnp.sum(e, axis=0, keepdims=True), acc_old * alpha + pv))
            return tuple(out)

        init = tuple((jnp.full((1, cols), NEG, F32), jnp.zeros((1, cols), F32), jnp.zeros((HEAD_DIM, cols), F32))
                     for _ in groups)
        if selected:
            state = lax.fori_loop(t_lo, t_hi - 1, lambda t, st: tile(t, st, False), init)
            state = tile(t_hi - 1, state, True)
        else:
            state = lax.fori_loop(t_lo, t_hi, lambda t, st: tile(t, st, True), init)
        return [state[g][2] / jnp.maximum(state[g][1], 1e-30) for g in groups]

    last = (start + tq - 1) // tk
    o_slc = attend(sk_ref, svt_ref, 0, last + 1, True)
    o_win = attend(wk_ref, wvt_ref, jnp.maximum(start - WINDOW, 0) // tk, last + 1, False)

    gates_t = gate_ref[...].T
    for g in groups:
        def gate_row(branch):
            return head_row(lambda h: gates_t[branch * N_NSA_HEADS + g * NSA_GROUP + h:
                                              branch * N_NSA_HEADS + g * NSA_GROUP + h + 1, :])
        o = gate_row(0) * o_cmp[g] + gate_row(1) * o_slc[g] + gate_row(2) * o_win[g]
        for h in range(NSA_GROUP):
            c = (g * NSA_GROUP + h) * HEAD_DIM
            o_ref[:, c:c + HEAD_DIM] = o[:, hsl[h]].T


def _nsa_prompt_t(qn, gates, kvc, nsab, winb, tq=128, tk=256):
    t = qn.shape[0]
    n_cmp = kvc.shape[1]
    n_tiles = t // tk
    tiles_t = lambda v: v.T.reshape(KV_W, n_tiles, tk).transpose(1, 0, 2)
    svt = tiles_t(nsab[:, 3 * KV_W:])
    wvt = tiles_t(winb[:, KV_W:])
    vct = kvc[1].T
    et = jnp.transpose(_block_expansion(t)).reshape(n_tiles, tk, LANES)
    mimpt = jnp.transpose(_importance_matrix(n_cmp, LANES))
    once = pl.Buffered(1)
    full = lambda a: pl.BlockSpec(a.shape, lambda i: (0,) * a.ndim, pipeline_mode=once)
    col = lambda c: pl.BlockSpec((t, KV_W), lambda i: (0, c), pipeline_mode=once)
    return pl.pallas_call(
        functools.partial(_nsa_prompt_t_kernel, tq=tq, tk=tk),
        grid=(t // tq,),
        in_specs=[pl.BlockSpec((tq, NSA_W), lambda i: (i, 0)),
                  pl.BlockSpec((tq, LANES), lambda i: (i, 0)),
                  pl.BlockSpec((1, n_cmp, KV_W), lambda i: (0, 0, 0), pipeline_mode=once),
                  full(vct), col(2), full(svt), col(0), full(wvt), full(et), full(mimpt)],
        out_specs=pl.BlockSpec((tq, NSA_W), lambda i: (i, 0)),
        out_shape=jax.ShapeDtypeStruct((t, NSA_W), F32),
        scratch_shapes=[pltpu.VMEM((N_NSA_HEADS, tk, tq), F32)],
        compiler_params=_cparams(("arbitrary",)),
        name="nsa_prompt",
    )(qn, gates, kvc, vct, nsab, svt, winb, wvt, et, mimpt)


def _nsa_sample_kernel(pt_ref, q_ref, gate_ref, newrows_ref, newwin_ref, win_ref, wpos_ref, wc_ref,
                       e_ref, mimp_ref, *refs, n_new, n_pages, ppg):
    cache_refs = refs[:ppg]
    o_ref, winout_ref, first_ref, second_ref, ks_ref, vs_ref = refs[ppg:]
    p = pl.program_id(1)
    page = LANES
    past = n_pages * page
    cpp = page // CMP_STRIDE
    n_cmp = first_ref.shape[1]
    n_keys = ks_ref.shape[0]
    w_buf = win_ref.shape[1] // (2 * N_NSA_KV)
    gsl = [slice(g * HEAD_DIM, (g + 1) * HEAD_DIM) for g in range(N_NSA_KV)]
    kinds = 4 * N_NSA_KV

    @pl.when(p == 0)
    def _():
        first_ref[...] = jnp.zeros_like(first_ref)
        second_ref[...] = jnp.zeros_like(second_ref)

    for q, cache_ref in enumerate(cache_refs):
        pg = p * ppg + q
        c0 = pl.multiple_of(pg * cpp, cpp)
        r0 = pl.multiple_of(pg * page, page)
        for g in range(N_NSA_KV):
            for kv in range(2):
                x = cache_ref[0, pl.ds(kv * N_NSA_KV + g, page, stride=kinds), :]
                f, sec = _pool_chunks(x, wpos_ref[kv, :CMP_STRIDE, gsl[g]], wpos_ref[kv, CMP_STRIDE:, gsl[g]])
                first_ref[kv, pl.ds(c0, cpp), gsl[g]] = f
                second_ref[kv, pl.ds(c0, cpp), gsl[g]] = sec
            ks_ref[pl.ds(r0, page), gsl[g]] = cache_ref[0, pl.ds(2 * N_NSA_KV + g, page, stride=kinds), :].astype(BF16)
            vs_ref[pl.ds(r0, page), gsl[g]] = cache_ref[0, pl.ds(3 * N_NSA_KV + g, page, stride=kinds), :].astype(BF16)

    @pl.when(p == pl.num_programs(1) - 1)
    def _():
        new = newrows_ref[0]
        zpad = jnp.zeros((CMP_STRIDE - n_new, KV_W), F32)
        kvc = []
        for kv in range(2):
            xc = jnp.concatenate([new[:, kv * KV_W:(kv + 1) * KV_W], zpad], axis=0)
            first_ref[kv, past // CMP_STRIDE:past // CMP_STRIDE + 1, :] = jnp.sum(
                xc * wpos_ref[kv, :CMP_STRIDE, :], axis=0, keepdims=True)
            second_ref[kv, past // CMP_STRIDE:past // CMP_STRIDE + 1, :] = jnp.sum(
                xc * wpos_ref[kv, CMP_STRIDE:, :], axis=0, keepdims=True)
            pre = first_ref[kv] + pltpu.roll(second_ref[kv], n_cmp - 1, axis=0)
            kvc.append([_dot(pre[:, gsl[g]].astype(BF16), wc_ref[kv, g]).astype(BF16) for g in range(N_NSA_KV)])
        tail = jnp.zeros((n_keys - past - n_new, KV_W), F32)
        ks_ref[past:, :] = jnp.concatenate([new[:, 2 * KV_W:3 * KV_W], tail], axis=0).astype(BF16)
        vs_ref[past:, :] = jnp.concatenate([new[:, 3 * KV_W:4 * KV_W], tail], axis=0).astype(BF16)

        nwin = newwin_ref[0]
        wr = 2 * N_NSA_KV
        winout_ref[0, :(w_buf - n_new) * wr, :] = win_ref[0, n_new * wr:, :]
        for c in range(wr):
            winout_ref[0, pl.ds((w_buf - n_new) * wr + c, n_new, stride=wr), :] = nwin[:, c * HEAD_DIM:(c + 1) * HEAD_DIM]
        wtail = jnp.zeros((LANES - n_new, HEAD_DIM), F32)

        def window(c):
            return jnp.concatenate([win_ref[0, pl.ds(c, w_buf, stride=wr), :],
                                    nwin[:, c * HEAD_DIM:(c + 1) * HEAD_DIM], wtail], axis=0).astype(BF16)

        n_win = w_buf + LANES
        rows = NSA_GROUP * n_new
        r = lax.broadcasted_iota(jnp.int32, (rows, 1), 0)
        qpos = past + r % n_new
        qpos_q = past + lax.broadcasted_iota(jnp.int32, (n_new, 1), 0)
        kc_end = CMP_STRIDE * lax.broadcasted_iota(jnp.int32, (1, n_cmp), 1) + (CMP_BLOCK - 1)
        dist_s = qpos - lax.broadcasted_iota(jnp.int32, (1, n_keys), 1)
        dist_w = qpos - (past - w_buf + lax.broadcasted_iota(jnp.int32, (1, n_win), 1))
        gates = gate_ref[0]

        qgs, slopes, o_cmps, imps = [], [], [], []
        for g in range(N_NSA_KV):
            qg = jnp.concatenate(
                [q_ref[0, :, (g * NSA_GROUP + h) * HEAD_DIM:(g * NSA_GROUP + h + 1) * HEAD_DIM]
                 for h in range(NSA_GROUP)], axis=0)
            slope = _head_slope(g * NSA_GROUP + r // n_new)
            s = _dot_nt(qg, kvc[0][g]) * ATT_SCALE - slope * (qpos - kc_end).astype(F32)
            pc = _masked_softmax(s, kc_end <= qpos)
            o_cmps.append(_dot(pc.astype(BF16), kvc[1][g]))
            p4 = pc[0:n_new] + pc[n_new:2 * n_new] + pc[2 * n_new:3 * n_new] + pc[3 * n_new:4 * n_new]
            hi, lo = _split_bf16(p4)
            imps.append(_dot(hi, mimp_ref[...]) + _dot(lo, mimp_ref[...]))
            qgs.append(qg)
            slopes.append(slope)
        n_slc = -(-(past + n_new) // SLC_BLOCK)
        sel_all = _top_blocks_by_rank(
            _slc_scores(jnp.concatenate(imps, axis=0), jnp.concatenate([qpos_q] * N_NSA_KV, axis=0)), SLC_TOP, n_slc)

        for g in range(N_NSA_KV):
            qg, slope, o_cmp = qgs[g], slopes[g], o_cmps[g]
            sel = sel_all[g * n_new:(g + 1) * n_new]
            keep = _dot(jnp.concatenate([sel] * NSA_GROUP, axis=0).astype(BF16), e_ref[...]) > 0.5
            s = _dot_nt(qg, ks_ref[:, gsl[g]]) * ATT_SCALE - slope * dist_s.astype(F32)
            ps = _masked_softmax(s, (dist_s >= 0) & keep)
            o_slc = _dot(ps.astype(BF16), vs_ref[:, gsl[g]])
            s = _dot_nt(qg, window(g)) * ATT_SCALE - slope * dist_w.astype(F32)
            pw = _masked_softmax(s, (dist_w >= 0) & (dist_w < WINDOW))
            o_win = _dot(pw.astype(BF16), window(N_NSA_KV + g))

            o = (_gate_cols(gates, 0, g, n_new) * o_cmp + _gate_cols(gates, 1, g, n_new) * o_slc
                 + _gate_cols(gates, 2, g, n_new) * o_win)
            for h in range(NSA_GROUP):
                c = (g * NSA_GROUP + h) * HEAD_DIM
                o_ref[0, :, c:c + HEAD_DIM] = o[h * n_new:(h + 1) * n_new]


def _nsa_sample(page_table, qn, gates, nsarows, winrows, cache, state_win, wpos, wc, ppg=4):
    bs, n_new, _ = qn.shape
    n_pages = page_table.shape[1]
    page_rows = cache.shape[1]
    assert page_rows == LANES * 4 * N_NSA_KV and n_pages % ppg == 0
    past = n_pages * LANES
    win_rows = state_win.shape[1]
    n_keys = past + LANES
    n_cmp = 2 * (past // CMP_STRIDE)
    e = _block_expansion(n_keys)
    mimp = _importance_matrix(n_cmp, LANES)
    seq = lambda b, p, pt: (b, 0, 0)
    c3 = lambda b, p, pt: (0, 0, 0)

    def page_map(q):
        return lambda b, p, pt: (pt[b * n_pages + p * ppg + q], 0, 0)

    grid_spec = pltpu.PrefetchScalarGridSpec(
        num_scalar_prefetch=1,
        grid=(bs, n_pages // ppg),
        in_specs=[pl.BlockSpec((1, n_new, NSA_W), seq),
                  pl.BlockSpec((1, n_new, LANES), seq),
                  pl.BlockSpec((1, n_new, 4 * KV_W), seq),
                  pl.BlockSpec((1, n_new, 2 * KV_W), seq),
                  pl.BlockSpec((1, win_rows, HEAD_DIM), seq),
                  pl.BlockSpec(wpos.shape, c3),
                  pl.BlockSpec(wc.shape, lambda b, p, pt: (0, 0, 0, 0)),
                  pl.BlockSpec(e.shape, lambda b, p, pt: (0, 0)),
                  pl.BlockSpec(mimp.shape, lambda b, p, pt: (0, 0))]
                 + [pl.BlockSpec((1, page_rows, HEAD_DIM), page_map(q)) for q in range(ppg)],
        out_specs=[pl.BlockSpec((1, n_new, NSA_W), seq),
                   pl.BlockSpec((1, win_rows, HEAD_DIM), seq)],
        scratch_shapes=[pltpu.VMEM((2, n_cmp, KV_W), F32), pltpu.VMEM((2, n_cmp, KV_W), F32),
                        pltpu.VMEM((n_keys, KV_W), BF16), pltpu.VMEM((n_keys, KV_W), BF16)],
    )
    return pl.pallas_call(
        functools.partial(_nsa_sample_kernel, n_new=n_new, n_pages=n_pages, ppg=ppg),
        grid_spec=grid_spec,
        out_shape=[jax.ShapeDtypeStruct((bs, n_new, NSA_W), F32),
                   jax.ShapeDtypeStruct((bs, win_rows, HEAD_DIM), F32)],
        compiler_params=_cparams(("arbitrary", "arbitrary")),
        name="nsa_sample",
    )(page_table.reshape(-1), qn, gates, nsarows, winrows, state_win, wpos, wc, e, mimp, *([cache] * ppg))


def _mixout_kernel(osb_ref, on_ref, x_ref, gout_ref, w_ref, gate_ref, g2_ref, shift_ref, scale_ref, *rest):
    x1_ref, h2_ref, h2g_ref, nrm_ref = rest[-4:]
    tm = x_ref.shape[0]
    for h in range(N_SB_HEADS + N_NSA_HEADS):
        src = osb_ref if h < N_SB_HEADS else on_ref
        c = (h % N_SB_HEADS) * HEAD_DIM
        sl = slice(h * HEAD_DIM, (h + 1) * HEAD_DIM)
        nrm_ref[:, sl] = (_rms(src[:, c:c + HEAD_DIM]) * gout_ref[:, sl]).astype(BF16)
    x1 = x_ref[...] + gate_ref[...] * _dot(nrm_ref[...], w_ref[...])
    x1_ref[...] = x1
    h2 = _rms(x1) * g2_ref[...] * (1.0 + scale_ref[...]) + shift_ref[...]
    h2_ref[...] = h2.astype(BF16)
    for s in range(ROW_WORDS):
        lo = h2[:, (2 * s) * LANES:(2 * s + 1) * LANES].astype(BF16).astype(F32)
        hi = h2[:, (2 * s + 1) * LANES:(2 * s + 2) * LANES].astype(BF16).astype(F32)
        word = (lax.bitcast_convert_type(lo, U32) >> 16) | lax.bitcast_convert_type(hi, U32)
        h2g_ref[pl.ds(s, tm, stride=ROW_WORDS), :] = word


def _mixout(osb, on, x, gout, w_out, gate, g2, shift, scale, n_total, row0, prev=None, tm=256):
    m, d = x.shape
    blk0 = row0 // tm
    row = lambda i: (i, 0)
    out_row = lambda i: (blk0 + i, 0)
    const = lambda i: (0, 0)
    mod_spec = lambda a: pl.BlockSpec((tm, d), row) if a.shape[0] == m else pl.BlockSpec((1, d), const)
    in_specs = [pl.BlockSpec((tm, SB_W), row), pl.BlockSpec((tm, NSA_W), row), pl.BlockSpec((tm, d), row),
                pl.BlockSpec((1, d), const),
                pl.BlockSpec(w_out.shape, const, pipeline_mode=pl.Buffered(1)),
                mod_spec(gate), pl.BlockSpec((1, d), const), mod_spec(shift), mod_spec(scale)]
    args = [osb, on, x, gout, w_out, gate, g2, shift, scale]
    aliases = {}
    if prev is not None:
        in_specs += [pl.BlockSpec(memory_space=pl.ANY)] * 3
        aliases = {len(args) + j: j for j in range(3)}
        args += list(prev)
    return pl.pallas_call(
        _mixout_kernel,
        grid=(m // tm,),
        in_specs=in_specs,
        out_specs=[pl.BlockSpec((tm, d), out_row), pl.BlockSpec((tm, d), out_row),
                   pl.BlockSpec((tm * ROW_WORDS, LANES), out_row)],
        out_shape=[jax.ShapeDtypeStruct((n_total, d), F32), jax.ShapeDtypeStruct((n_total, d), BF16),
                   jax.ShapeDtypeStruct((n_total * ROW_WORDS, LANES), U32)],
        scratch_shapes=[pltpu.VMEM((tm, d), BF16)],
        input_output_aliases=aliases,
        compiler_params=_cparams(("arbitrary",)),
        name="mixout",
    )(*args)


def _first_max(v, idx, axis, n):
    m = jnp.max(v, axis=axis, keepdims=True)
    return m, jnp.min(jnp.where(v == m, idx, float(n)), axis=axis, keepdims=True)


def _router_kernel(h_ref, wrt_ref, bias_ref, eidx_ref, w_ref):
    tm = h_ref.shape[0]
    per = N_EXPERTS // N_EXPERT_GROUPS
    s = _sigmoid(_dot_nt(wrt_ref[...], h_ref[...]))
    biased = s + bias_ref[...]
    b3 = biased.reshape(N_EXPERT_GROUPS, per, tm)
    r3 = lax.broadcasted_iota(jnp.int32, b3.shape, 1).astype(F32)
    m1, i1 = _first_max(b3, r3, 1, per)
    m2 = jnp.max(jnp.where(r3 == i1, -jnp.inf, b3), axis=1, keepdims=True)
    gscore = (m1 + m2).reshape(N_EXPERT_GROUPS, tm)
    gi = lax.broadcasted_iota(jnp.int32, gscore.shape, 0).astype(F32)
    gmask = jnp.zeros(gscore.shape, F32)
    for _ in range(TOPK_GROUPS):
        _, idx = _first_max(gscore, gi, 0, N_EXPERT_GROUPS)
        pick = gi == idx
        gmask = jnp.where(pick, 1.0, gmask)
        gscore = jnp.where(pick, -jnp.inf, gscore)
    cand = jnp.where(gmask.reshape(N_EXPERT_GROUPS, 1, tm) > 0.5, b3, -jnp.inf).reshape(N_EXPERTS, tm)
    ei = lax.broadcasted_iota(jnp.int32, cand.shape, 0).astype(F32)
    ids, ws = [], []
    for _ in range(TOP_K):
        _, idx = _first_max(cand, ei, 0, N_EXPERTS)
        pick = ei == idx
        ids.append(idx)
        ws.append(jnp.sum(jnp.where(pick, s, 0.0), axis=0, keepdims=True))
        cand = jnp.where(pick, -jnp.inf, cand)
    w = jnp.concatenate(ws, axis=0)
    eidx_ref[...] = jnp.concatenate(ids, axis=0).astype(jnp.int32)
    w_ref[...] = w / jnp.sum(w, axis=0, keepdims=True) * ROUTED_SCALE


def _router(h2, w_router_t, router_bias, tm=512):
    m, d = h2.shape
    return pl.pallas_call(
        _router_kernel,
        grid=(m // tm,),
        in_specs=[pl.BlockSpec((tm, d), lambda i: (i, 0)),
                  pl.BlockSpec((N_EXPERTS, d), lambda i: (0, 0)),
                  pl.BlockSpec((N_EXPERTS, 1), lambda i: (0, 0))],
        out_specs=[pl.BlockSpec((TOP_K, tm), lambda i: (0, i)), pl.BlockSpec((TOP_K, tm), lambda i: (0, i))],
        out_shape=[jax.ShapeDtypeStruct((TOP_K, m), jnp.int32), jax.ShapeDtypeStruct((TOP_K, m), F32)],
        compiler_params=_cparams(("arbitrary",)),
        name="router",
    )(h2, w_router_t, router_bias.reshape(N_EXPERTS, 1))


def _dispatch_plan(eidx):
    t = eidx.shape[0]
    onehot = eidx[:, :, None] == jnp.arange(N_EXPERTS, dtype=jnp.int32)[None, None, :]
    routed = jnp.any(onehot, axis=1)
    ck = 128
    r3 = routed.reshape(t // ck, ck, N_EXPERTS).astype(BF16)
    tri = jnp.asarray(np.tril(np.ones((ck, ck), np.float32)), BF16)
    within = jnp.einsum('ij,cje->cie', tri, r3, preferred_element_type=F32)
    tot = within[:, -1, :]
    incl = (within + (jnp.cumsum(tot, axis=0) - tot)[:, None, :]).reshape(t, N_EXPERTS).astype(jnp.int32)
    counts = incl[-1]
    padded = (counts + MOE_ROWS - 1) // MOE_ROWS * MOE_ROWS
    pad_end = jnp.cumsum(padded)
    base = (pad_end - padded)[None, :] + incl - routed.astype(jnp.int32)
    dest = jnp.sum(jnp.where(onehot, base[:, None, :], 0), axis=-1).astype(jnp.int32)
    n_blk = -(-(t * TOP_K) // MOE_ROWS) + N_EXPERTS
    blk_start = (jnp.arange(n_blk) * MOE_ROWS)[:, None]
    blk_e = jnp.minimum(jnp.sum(pad_end[None, :] <= blk_start, axis=1), N_EXPERTS - 1).astype(jnp.int32)
    n_active = (pad_end[-1] // MOE_ROWS).astype(jnp.int32)
    nxt_blk = (pad_end[blk_e] // MOE_ROWS).astype(jnp.int32)
    nxt_e = jnp.where(nxt_blk < n_active, blk_e[jnp.minimum(nxt_blk, n_blk - 1)], -1).astype(jnp.int32)
    first = jnp.concatenate([jnp.ones((1,), jnp.int32), (blk_e[1:] != blk_e[:-1]).astype(jnp.int32)])
    par = ((jnp.cumsum(first) - 1) % 2).astype(jnp.int32)
    return dest, blk_e, nxt_e, par, n_active.reshape(1), n_blk


def _dispatch_kernel(dest_ref, h2g_ref, xs_ref, sem, *, tm):
    i = pl.program_id(0)

    def body(r, carry):
        src = h2g_ref.at[pl.ds(pl.multiple_of(r * ROW_WORDS, ROW_WORDS), ROW_WORDS), :]
        for k in range(TOP_K):
            d = dest_ref[(i * tm + r) * TOP_K + k]
            pltpu.make_async_copy(src, xs_ref.at[pl.ds(pl.multiple_of(d * ROW_WORDS, ROW_WORDS), ROW_WORDS), :],
                                  sem).start()
        return carry

    lax.fori_loop(0, tm, body, 0)
    done = xs_ref.at[pl.ds(0, tm * TOP_K * ROW_WORDS), :]
    pltpu.make_async_copy(done, done, sem).wait()


def _dispatch(dest_flat, h2g, p_rows, tm=256):
    n_tok = h2g.shape[0] // ROW_WORDS
    grid_spec = pltpu.PrefetchScalarGridSpec(
        num_scalar_prefetch=1,
        grid=(n_tok // tm,),
        in_specs=[pl.BlockSpec((tm * ROW_WORDS, LANES), lambda i, d: (i, 0))],
        out_specs=pl.BlockSpec(memory_space=pl.ANY),
        scratch_shapes=[pltpu.SemaphoreType.DMA(())],
    )
    return pl.pallas_call(
        functools.partial(_dispatch_kernel, tm=tm),
        grid_spec=grid_spec,
        out_shape=jax.ShapeDtypeStruct((p_rows * ROW_WORDS, LANES), U32),
        compiler_params=_cparams(("arbitrary",)),
        name="dispatch",
    )(dest_flat, h2g)


def _experts_kernel(be_ref, nxt_ref, par_ref, nact_ref, x_ref, wg_hbm, wu_hbm, wd_hbm, o_ref,
                    wgf, wuf, wdf, wgb, wub, wdb, sem):
    b = pl.program_id(0)
    active = b < nact_ref[0]
    e = be_ref[b]
    slot = par_ref[b]
    changed = e != be_ref[jnp.maximum(b - 1, 0)]

    def weight_copies(eid, s):
        return [pltpu.make_async_copy(src.at[eid], dst.at[s], sem.at[s, j])
                for j, (src, dst) in enumerate(((wg_hbm, wgf), (wu_hbm, wuf), (wd_hbm, wdf)))]

    @pl.when(active & (b == 0))
    def _():
        for cp in weight_copies(e, slot):
            cp.start()

    @pl.when(active & ((b == 0) | changed))
    def _():
        for cp in weight_copies(e, slot):
            cp.wait()
        nxt = nxt_ref[b]

        @pl.when(nxt >= 0)
        def _():
            for cp in weight_copies(nxt, 1 - slot):
                cp.start()

        ch = 256
        for c in range(0, D_MODEL, ch):
            wgb[c:c + ch, :] = wgf[slot, c:c + ch, :].astype(BF16)
            wub[c:c + ch, :] = wuf[slot, c:c + ch, :].astype(BF16)
        for c in range(0, D_EXPERT, ch // 4):
            wdb[c:c + ch // 4, :] = wdf[slot, c:c + ch // 4, :].astype(BF16)

    @pl.when(active)
    def _():
        chunks = []
        for s in range(ROW_WORDS):
            word = x_ref[pl.ds(s, MOE_ROWS, stride=ROW_WORDS), :]
            chunks.append(lax.bitcast_convert_type(word << 16, F32).astype(BF16))
            chunks.append(lax.bitcast_convert_type(word & jnp.uint32(0xFFFF0000), F32).astype(BF16))
        x = jnp.concatenate(chunks, axis=1)
        gt = _dot(x, wgb[...])
        a = (gt * _sigmoid(gt) * _dot(x, wub[...])).astype(BF16)
        y = _dot(a, wdb[...])
        for j in range(D_CHUNKS):
            o_ref[pl.ds(j, MOE_ROWS, stride=D_CHUNKS), :] = y[:, j * LANES:(j + 1) * LANES]


def _experts(blk_e, nxt_e, par, n_active, x_sorted, wg, wu, wd):
    d = wg.shape[1]
    n_blk = x_sorted.shape[0] // (MOE_ROWS * ROW_WORDS)
    hbm = pl.BlockSpec(memory_space=pl.ANY)
    grid_spec = pltpu.PrefetchScalarGridSpec(
        num_scalar_prefetch=4,
        grid=(n_blk,),
        in_specs=[pl.BlockSpec((MOE_ROWS * ROW_WORDS, LANES), lambda b, be, nx, pa, na: (b, 0)), hbm, hbm, hbm],
        out_specs=pl.BlockSpec((MOE_ROWS * D_CHUNKS, LANES), lambda b, be, nx, pa, na: (b, 0)),
        scratch_shapes=[pltpu.VMEM((2, d, D_EXPERT), F32), pltpu.VMEM((2, d, D_EXPERT), F32),
                        pltpu.VMEM((2, D_EXPERT, d), F32),
                        pltpu.VMEM((d, D_EXPERT), BF16), pltpu.VMEM((d, D_EXPERT), BF16),
                        pltpu.VMEM((D_EXPERT, d), BF16),
                        pltpu.SemaphoreType.DMA((2, 3))],
    )
    return pl.pallas_call(
        _experts_kernel,
        grid_spec=grid_spec,
        out_shape=jax.ShapeDtypeStruct((n_blk * MOE_ROWS * D_CHUNKS, LANES), F32),
        compiler_params=_cparams(("arbitrary",)),
        name="experts",
    )(blk_e, nxt_e, par, n_active, x_sorted, wg, wu, wd)


def _final_kernel(dest_ref, x1_ref, h2_ref, w8_ref, ys_ref, wg_ref, wu_ref, wd_ref, gate_ref, gf_ref,
                  shift_ref, scale_ref, y_ref, ybuf, ymoe_ref, sem, *, tm, row0):
    i = pl.program_id(0)
    n = pl.num_programs(0)
    tile_rows = tm * TOP_K * D_CHUNKS

    def issue(tile, slot):
        def body(r, carry):
            for k in range(TOP_K):
                d = dest_ref[(row0 + tile * tm + r) * TOP_K + k]
                pltpu.make_async_copy(
                    ys_ref.at[pl.ds(pl.multiple_of(d * D_CHUNKS, D_CHUNKS), D_CHUNKS), :],
                    ybuf.at[slot, pl.ds(pl.multiple_of((k * tm + r) * D_CHUNKS, D_CHUNKS), D_CHUNKS), :],
                    sem.at[slot]).start()
            return carry
        lax.fori_loop(0, tm, body, 0)

    @pl.when(i == 0)
    def _():
        issue(0, 0)

    @pl.when(i + 1 < n)
    def _():
        issue(i + 1, (i + 1) % 2)

    slot = i % 2
    pltpu.make_async_copy(ys_ref.at[pl.ds(0, tile_rows), :], ybuf.at[slot], sem.at[slot]).wait()
    w8 = w8_ref[...]
    wk = [w8[:, k:k + 1] for k in range(TOP_K)]
    for j in range(D_CHUNKS):
        acc = None
        for k in range(TOP_K):
            v = ybuf[slot, pl.ds(k * tm * D_CHUNKS + j, tm, stride=D_CHUNKS), :] * wk[k]
            acc = v if acc is None else acc + v
        ymoe_ref[:, j * LANES:(j + 1) * LANES] = acc

    h = h2_ref[...]
    gt = _dot(h, wg_ref[...])
    a = (gt * _sigmoid(gt) * _dot(h, wu_ref[...])).astype(BF16)
    f = ymoe_ref[...] + _dot(a, wd_ref[...])
    x2 = x1_ref[...] + gate_ref[...] * f
    y_ref[...] = _rms(x2) * gf_ref[...] * (1.0 + scale_ref[...]) + shift_ref[...]


def _final(dest_flat, x1, h2, w8, y_sorted, wg, wu, wd, gate, gf, shift, scale, m, row0, tm=64):
    d = x1.shape[1]
    blk0 = row0 // tm
    in_row = lambda i, dr: (blk0 + i, 0)
    row = lambda i, dr: (i, 0)
    const = lambda i, dr: (0, 0)
    mod_spec = lambda a: pl.BlockSpec((tm, d), row) if a.shape[0] == m else pl.BlockSpec((1, d), const)
    grid_spec = pltpu.PrefetchScalarGridSpec(
        num_scalar_prefetch=1,
        grid=(m // tm,),
        in_specs=[pl.BlockSpec((tm, d), in_row), pl.BlockSpec((tm, d), in_row), pl.BlockSpec((tm, TOP_K), in_row),
                  pl.BlockSpec(memory_space=pl.ANY),
                  pl.BlockSpec(wg.shape, const), pl.BlockSpec(wu.shape, const), pl.BlockSpec(wd.shape, const),
                  mod_spec(gate), pl.BlockSpec((1, d), const), mod_spec(shift), mod_spec(scale)],
        out_specs=pl.BlockSpec((tm, d), row),
        scratch_shapes=[pltpu.VMEM((2, tm * TOP_K * D_CHUNKS, LANES), F32), pltpu.VMEM((tm, d), F32),
                        pltpu.SemaphoreType.DMA((2,))],
    )
    return pl.pallas_call(
        functools.partial(_final_kernel, tm=tm, row0=row0),
        grid_spec=grid_spec,
        out_shape=jax.ShapeDtypeStruct((m, d), F32),
        compiler_params=_cparams(("arbitrary",)),
        name="final",
    )(dest_flat, x1, h2, w8, y_sorted, wg, wu, wd, gate, gf, shift, scale)


def kernel(x_prompt, x_sample, cache_sb, cache_nsa, state_win, page_table, c_prompt, c_sample, w_ada, b_ada, norm1_g, w_in, cmp_pos_k, cmp_pos_v, cmp_wk, cmp_wv, out_norm_g, w_out, norm2_g, w_router, router_bias, w_gate_e, w_up_e, w_down_e, w_gate_s, w_up_s, w_down_s, normf_g):
    b_p, t, d = x_prompt.shape
    bs, n_new, _ = x_sample.shape
    assert b_p == 1 and d == D_MODEL
    n_s = bs * n_new
    n_tok = t + n_s

    c_all = jnp.concatenate([c_prompt, c_sample], axis=0)
    m_pad = -(-c_all.shape[0] // 8) * 8
    c_all = jnp.pad(c_all, ((0, m_pad - c_all.shape[0]), (0, 0)))
    mod = _ada(c_all, w_ada, b_ada).reshape(m_pad, N_ADA, d)
    mod_p = [mod[0:1, i] for i in range(N_ADA)]
    mod_s = [jnp.repeat(mod[1:1 + bs, i], n_new, axis=0) for i in range(N_ADA)]

    w_main = w_in[:, :COL_GATE]
    w_gate = jnp.pad(w_in[:, COL_GATE:], ((0, 0), (0, LANES - N_GATE)))
    w_cat = jnp.concatenate([w_main, w_gate], axis=1).astype(BF16)
    row1 = lambda v: v.reshape(1, -1)

    xp = x_prompt.reshape(t, d)
    xs = x_sample.reshape(n_s, d)
    (qsb_p, sbrows_p, kvsb_p, qn_p, nsarows_p, nsab_p, winrows_p, winb_p, gates_p) = _project(
        xp, row1(norm1_g), mod_p[0], mod_p[1], w_cat)
    (qsb_s, sbrows_s, _, qn_s, nsarows_s, _, winrows_s, _, gates_s) = _project(
        xs, row1(norm1_g), mod_s[0], mod_s[1], w_cat)

    osb_p = _sb_prompt(qsb_p, kvsb_p)
    wpos = jnp.stack([jnp.repeat(cmp_pos_k, HEAD_DIM, axis=1), jnp.repeat(cmp_pos_v, HEAD_DIM, axis=1)])
    wc = jnp.stack([cmp_wk, cmp_wv]).astype(BF16)
    kvc_p = _compress_prompt(nsarows_p, wpos, wc)
    on_p = _nsa_prompt_t(qn_p, gates_p, kvc_p, nsab_p, winb_p)

    per_seq = lambda a: a.reshape(bs, n_new, a.shape[-1])
    n_phys = cache_sb.shape[0]
    w_buf = state_win.shape[1]
    osb_s = _sb_sample(page_table, per_seq(qsb_s), per_seq(sbrows_s), cache_sb.reshape(n_phys, -1, HEAD_DIM))
    on_s, win_s = _nsa_sample(page_table, per_seq(qn_s), per_seq(gates_s), per_seq(nsarows_s), per_seq(winrows_s),
                              cache_nsa.reshape(n_phys, -1, HEAD_DIM), state_win.reshape(bs, -1, HEAD_DIM), wpos, wc)

    w_out_b = w_out.astype(BF16)
    gout = out_norm_g.reshape(1, -1)
    bufs = _mixout(osb_p, on_p, xp, gout, w_out_b, mod_p[2], row1(norm2_g), mod_p[3], mod_p[4], n_tok, 0)
    x1, h2, h2g = _mixout(osb_s.reshape(n_s, SB_W), on_s.reshape(n_s, NSA_W), xs, gout, w_out_b,
                          mod_s[2], row1(norm2_g), mod_s[3], mod_s[4], n_tok, t, prev=bufs)

    eidx_t, w_t = _router(h2, w_router.T.astype(BF16), router_bias)
    dest, blk_e, nxt_e, par, n_active, n_blk = _dispatch_plan(eidx_t.T)
    dest_flat = dest.reshape(-1)
    x_sorted = _dispatch(dest_flat, h2g, n_blk * MOE_ROWS)
    y_sorted = _experts(blk_e, nxt_e, par, n_active, x_sorted, w_gate_e, w_up_e, w_down_e)
    w8 = w_t.T
    ws = (w_gate_s.astype(BF16), w_up_s.astype(BF16), w_down_s.astype(BF16))
    y_p = _final(dest_flat, x1, h2, w8, y_sorted, *ws, mod_p[5], row1(normf_g), mod_p[6], mod_p[7], t, 0)
    y_s = _final(dest_flat, x1, h2, w8, y_sorted, *ws, mod_s[5], row1(normf_g), mod_s[6], mod_s[7], n_s, t)

    keep = min(WINDOW, t)
    return (y_p.reshape(1, t, d), y_s.reshape(bs, n_new, d),
            sbrows_p.reshape(1, t, 2, N_SB_HEADS, HEAD_DIM), sbrows_s.reshape(bs, n_new, 2, N_SB_HEADS, HEAD_DIM),
            nsarows_p.reshape(1, t, 4, N_NSA_KV, HEAD_DIM), nsarows_s.reshape(bs, n_new, 4, N_NSA_KV, HEAD_DIM),
            winrows_p[t - keep:].reshape(1, keep, 2, N_NSA_KV, HEAD_DIM),
            win_s.reshape(bs, w_buf, 2, N_NSA_KV, HEAD_DIM))
```

```python
import functools

import numpy as np
import jax
import jax.numpy as jnp
from jax import lax
from jax.experimental import pallas as pl
from jax.experimental.pallas import tpu as pltpu

F32 = jnp.float32
BF16 = jnp.bfloat16
U32 = jnp.uint32

D_MODEL = 2048
HEAD_DIM = 128
N_SB_HEADS = 8
N_NSA_HEADS = 8
N_NSA_KV = 2
NSA_GROUP = N_NSA_HEADS // N_NSA_KV
CMP_BLOCK = 32
CMP_STRIDE = 16
SLC_BLOCK = 64
SLC_TOP = 16
WINDOW = 512
FORCE = 1e4
NEG = -1e30
N_EXPERTS = 256
TOP_K = 8
N_EXPERT_GROUPS = 8
TOPK_GROUPS = 4
D_EXPERT = D_MODEL // 4
ROUTED_SCALE = 2.5
MOE_ROWS = 128
N_ADA = 8
EPS = 1e-6
ATT_SCALE = HEAD_DIM ** -0.5

SB_W = N_SB_HEADS * HEAD_DIM
NSA_W = N_NSA_HEADS * HEAD_DIM
KV_W = N_NSA_KV * HEAD_DIM
COL_QSB = 0
COL_KVSB = SB_W
COL_QN = 3 * SB_W
COL_NSA = COL_QN + NSA_W
COL_WIN = COL_NSA + 4 * KV_W
COL_GATE = COL_WIN + 2 * KV_W
N_GATE = 3 * N_NSA_HEADS
LANES = 128
V7X_VMEM_LIMIT = 56 * 1024 * 1024
D_CHUNKS = D_MODEL // LANES
ROW_WORDS = D_CHUNKS // 2

NT = (((1,), (1,)), ((), ()))


def _cparams(sem):
    return pltpu.CompilerParams(dimension_semantics=sem, vmem_limit_bytes=V7X_VMEM_LIMIT)


def _sigmoid(x):
    return 1.0 / (1.0 + jnp.exp(-x))


def _split_bf16(x):
    hi = x.astype(BF16)
    lo = (x - hi.astype(F32)).astype(BF16)
    return hi, lo


def _dot(a, b):
    return jnp.dot(a, b, preferred_element_type=F32)


def _dot_nt(a, b):
    return lax.dot_general(a, b, NT, preferred_element_type=F32)


def _rms(x):
    return x * lax.rsqrt(jnp.mean(x * x, axis=-1, keepdims=True) + EPS)


def _masked_softmax(s, mask):
    s = jnp.where(mask, s, NEG)
    m = jnp.max(s, axis=-1, keepdims=True)
    e = jnp.where(mask, jnp.exp(s - m), 0.0)
    return e / jnp.maximum(jnp.sum(e, axis=-1, keepdims=True), 1e-30)


def _head_slope(head):
    out = jnp.full(head.shape, 2.0 ** -N_NSA_HEADS, F32)
    for i in range(N_NSA_HEADS - 1):
        out = jnp.where(head == i, 2.0 ** -(i + 1), out)
    return out


def _top_blocks_by_rank(score, n_top, n_cand):
    blk = lax.broadcasted_iota(jnp.int32, score.shape, 1)
    rank = jnp.zeros(score.shape, F32)
    for i in range(n_cand):
        c = score[:, i:i + 1]
        tie = jnp.where(blk > i, 1.0, 0.0)
        rank = rank + jnp.where(c > score, 1.0, jnp.where(c == score, tie, 0.0))
    return jnp.where(rank < n_top, 1.0, 0.0)


def _slc_scores(imp, qpos):
    blk = lax.broadcasted_iota(jnp.int32, imp.shape, 1)
    qb = qpos // SLC_BLOCK
    forced = (blk == 0) | ((blk <= qb) & (blk >= qb - 1))
    return jnp.where(blk > qb, -1.0, jnp.where(forced, FORCE, imp))


def _gate_cols(gates, branch, group, n_rows_per_head):
    lane = lax.broadcasted_iota(jnp.int32, gates.shape, 1)
    cols = [jnp.sum(jnp.where(lane == branch * N_NSA_HEADS + group * NSA_GROUP + h, gates, 0.0),
                    axis=-1, keepdims=True) for h in range(NSA_GROUP)]
    return jnp.concatenate(cols, axis=0)


def _ada_kernel(c_ref, w_ref, b_ref, o_ref):
    c = c_ref[...]
    a = (c * _sigmoid(c)).astype(BF16)
    o_ref[...] = _dot(a, w_ref[...].astype(BF16)) + b_ref[...]


def _ada(c, w_ada, b_ada, tn=1024):
    m, d = c.shape
    n = w_ada.shape[1]
    return pl.pallas_call(
        _ada_kernel,
        grid=(n // tn,),
        in_specs=[pl.BlockSpec((m, d), lambda j: (0, 0)),
                  pl.BlockSpec((d, tn), lambda j: (0, j)),
                  pl.BlockSpec((1, tn), lambda j: (0, j))],
        out_specs=pl.BlockSpec((m, tn), lambda j: (0, j)),
        out_shape=jax.ShapeDtypeStruct((m, n), F32),
        compiler_params=_cparams(("arbitrary",)),
        name="ada",
    )(c, w_ada, b_ada.reshape(1, n))


def _proj_kernel(x_ref, g_ref, shift_ref, scale_ref, w_ref,
                 qsb_ref, sbrows_ref, kvsb_ref, qn_ref, nsarows_ref, nsab_ref,
                 winrows_ref, winb_ref, gates_ref):
    h = (_rms(x_ref[...]) * g_ref[...] * (1.0 + scale_ref[...]) + shift_ref[...]).astype(BF16)
    ch = 512

    def mm(c0, n):
        return _dot(h, w_ref[:, c0:c0 + n])

    for c in range(0, SB_W, ch):
        qsb_ref[:, c:c + ch] = mm(COL_QSB + c, ch).astype(BF16)
    for c in range(0, 2 * SB_W, ch):
        r = mm(COL_KVSB + c, ch)
        sbrows_ref[:, c:c + ch] = r
        kvsb_ref[:, c:c + ch] = r.astype(BF16)
    for c in range(0, NSA_W, ch):
        qn_ref[:, c:c + ch] = mm(COL_QN + c, ch).astype(BF16)
    for c in range(0, 4 * KV_W, ch):
        r = mm(COL_NSA + c, ch)
        nsarows_ref[:, c:c + ch] = r
        nsab_ref[:, c:c + ch] = r.astype(BF16)
    r = mm(COL_WIN, 2 * KV_W)
    winrows_ref[...] = r
    winb_ref[...] = r.astype(BF16)
    gates_ref[...] = _sigmoid(mm(COL_GATE, LANES))


def _project(x, g, shift, scale, w_cat, tm=256):
    m, d = x.shape
    row = lambda i: (i, 0)
    const = lambda i: (0, 0)
    mod_spec = lambda a: pl.BlockSpec((tm, d), row) if a.shape[0] == m else pl.BlockSpec((1, d), const)
    widths = [(SB_W, BF16), (2 * SB_W, F32), (2 * SB_W, BF16), (NSA_W, BF16), (4 * KV_W, F32),
              (4 * KV_W, BF16), (2 * KV_W, F32), (2 * KV_W, BF16), (LANES, F32)]
    return pl.pallas_call(
        _proj_kernel,
        grid=(m // tm,),
        in_specs=[pl.BlockSpec((tm, d), row),
                  pl.BlockSpec((1, d), const),
                  mod_spec(shift), mod_spec(scale),
                  pl.BlockSpec(w_cat.shape, const, pipeline_mode=pl.Buffered(1))],
        out_specs=[pl.BlockSpec((tm, w), row) for w, _ in widths],
        out_shape=[jax.ShapeDtypeStruct((m, w), dt) for w, dt in widths],
        compiler_params=_cparams(("arbitrary",)),
        name="project",
    )(x, g, shift, scale, w_cat)


def _sb_logits(z):
    l1m = jnp.minimum(-z, 0.0) - jnp.log(1.0 + jnp.exp(-jnp.abs(z)))
    return l1m + z, l1m


def _sb_prompt_kernel(q_ref, k_ref, v_ref, o_ref, *, tq, heads):
    i = pl.program_id(1)
    row = lax.broadcasted_iota(jnp.int32, (2 * tq, tq), 0)
    col = lax.broadcasted_iota(jnp.int32, (2 * tq, tq), 1)
    later2 = jnp.where(jnp.where(row >= tq, row - tq, row) > col, 1.0, 0.0).astype(BF16)
    causal = lax.broadcasted_iota(jnp.int32, (tq, tq), 1) < lax.broadcasted_iota(jnp.int32, (tq, tq), 0)

    hsl = [slice(hh * HEAD_DIM, (hh + 1) * HEAD_DIM) for hh in range(heads)]

    def block(j, state, diagonal):
        off = pl.multiple_of(j * tq, tq)
        zs = [_dot_nt(q_ref[:, hs], k_ref[pl.ds(off, tq), hs]) * ATT_SCALE for hs in hsl]
        logits = [_sb_logits(z) for z in zs]
        if diagonal:
            logits = [(ls, jnp.where(causal, l1m, 0.0)) for ls, l1m in logits]
        split = [jnp.concatenate(_split_bf16(l1m), axis=1) for _, l1m in logits]
        suffix = [_dot(sp, later2) + state[hh][0] for hh, sp in enumerate(split)]
        a = [jnp.exp(logits[hh][0] + suffix[hh]) for hh in range(heads)]
        if diagonal:
            a = [jnp.where(causal, x, 0.0) for x in a]
        acc = [state[hh][1] + _dot(a[hh].astype(BF16), v_ref[pl.ds(off, tq), hsl[hh]]) for hh in range(heads)]
        return tuple((suffix[hh][:, :1] + logits[hh][1][:, :1], acc[hh]) for hh in range(heads))

    init = tuple((jnp.zeros((tq, 1), F32), jnp.zeros((tq, HEAD_DIM), F32)) for _ in range(heads))
    state = block(i, init, True)
    state = lax.fori_loop(0, i, lambda t, st: block(i - 1 - t, st, False), state)
    for hh in range(heads):
        o_ref[:, hh * HEAD_DIM:(hh + 1) * HEAD_DIM] = state[hh][1]


def _sb_prompt(qsb, kvsb, tq=256, heads=4):
    t = qsb.shape[0]
    w = heads * HEAD_DIM
    n_hp = N_SB_HEADS // heads
    return pl.pallas_call(
        functools.partial(_sb_prompt_kernel, tq=tq, heads=heads),
        grid=(n_hp, t // tq),
        in_specs=[pl.BlockSpec((tq, w), lambda h, i: (i, h)),
                  pl.BlockSpec((t, w), lambda h, i: (0, h)),
                  pl.BlockSpec((t, w), lambda h, i: (0, n_hp + h))],
        out_specs=pl.BlockSpec((tq, w), lambda h, i: (i, h)),
        out_shape=jax.ShapeDtypeStruct((t, SB_W), F32),
        compiler_params=_cparams(("arbitrary", "arbitrary")),
        name="sb_prompt",
    )(qsb, kvsb, kvsb)


def _sb_sample_kernel(pt_ref, q_ref, new_ref, *refs, n_new, ppg):
    cache_refs = refs[:ppg]
    o_ref, qbd_ref, carry_ref, acc_ref = refs[ppg:]
    p = pl.program_id(1)
    rows = N_SB_HEADS * n_new
    kk = lax.broadcasted_iota(jnp.int32, (2 * LANES, 2 * LANES), 0)
    cc = lax.broadcasted_iota(jnp.int32, (2 * LANES, 2 * LANES), 1)
    kk = jnp.where(kk >= LANES, kk - LANES, kk)
    later_and_all = jnp.where((cc >= LANES) | (kk > cc), 1.0, 0.0).astype(BF16)
    stride = 2 * N_SB_HEADS
    hsl = [slice(h * HEAD_DIM, (h + 1) * HEAD_DIM) for h in range(N_SB_HEADS)]
    rsl = [slice(h * n_new, (h + 1) * n_new) for h in range(N_SB_HEADS)]

    def scores(kmat, diagonal):
        z = _dot_nt(qbd_ref[...].astype(BF16), kmat) * ATT_SCALE
        ls, l1m = _sb_logits(z)
        causal = None
        if diagonal:
            key = lax.broadcasted_iota(jnp.int32, (rows, LANES), 1)
            qry = lax.broadcasted_iota(jnp.int32, (rows, LANES), 0) % n_new
            causal = key < qry
            l1m = jnp.where(causal, l1m, 0.0)
        hi, lo = _split_bf16(l1m)
        return ls, _dot(jnp.concatenate([hi, lo], axis=1), later_and_all), causal

    def accumulate(ls, st, causal, vmat, carry):
        a = jnp.exp(ls + st[:, :LANES] + carry)
        if causal is not None:
            a = jnp.where(causal, a, 0.0)
        o_all = _dot(a.astype(BF16), vmat)
        for h in range(N_SB_HEADS):
            acc_ref[rsl[h], :] += o_all[rsl[h], hsl[h]]
        return carry + st[:, LANES:]

    @pl.when(p == 0)
    def _():
        acc_ref[...] = jnp.zeros_like(acc_ref)
        qbd_ref[...] = jnp.zeros_like(qbd_ref)
        for h in range(N_SB_HEADS):
            qbd_ref[rsl[h], hsl[h]] = q_ref[0, :, hsl[h]].astype(F32)
        pad = jnp.zeros((LANES - n_new, SB_W), F32)
        knew = jnp.concatenate([new_ref[0, :, :SB_W], pad], axis=0).astype(BF16)
        vnew = jnp.concatenate([new_ref[0, :, SB_W:], pad], axis=0).astype(BF16)
        ls, st, causal = scores(knew, True)
        carry_ref[...] = accumulate(ls, st, causal, vnew, jnp.zeros((rows, LANES), F32))

    def page_mat(ref, c0):
        return jnp.concatenate([ref[0, pl.ds(c0 + h, LANES, stride=stride), :].astype(BF16)
                                for h in range(N_SB_HEADS)], axis=1)

    qbd = qbd_ref[...].astype(BF16)
    zs = [_dot_nt(qbd, page_mat(ref, 0)) * ATT_SCALE for ref in cache_refs]
    logits = [_sb_logits(z) for z in zs]
    sts = [_dot(jnp.concatenate(_split_bf16(l1m), axis=1), later_and_all) for _, l1m in logits]
    carry = carry_ref[...]
    probs = []
    for (ls, _), st in zip(logits, sts):
        probs.append(jnp.exp(ls + st[:, :LANES] + carry).astype(BF16))
        carry = carry + st[:, LANES:]
    carry_ref[...] = carry
    outs = [_dot(a, page_mat(ref, N_SB_HEADS)) for a, ref in zip(probs, cache_refs)]
    for h in range(N_SB_HEADS):
        acc = acc_ref[rsl[h], :]
        for o_all in outs:
            acc = acc + o_all[rsl[h], hsl[h]]
        acc_ref[rsl[h], :] = acc

    @pl.when(p == pl.num_programs(1) - 1)
    def _():
        for h in range(N_SB_HEADS):
            o_ref[0, :, hsl[h]] = acc_ref[rsl[h], :]


def _sb_sample(page_table, qsb, sbrows, cache, ppg=8):
    bs, n_new, _ = qsb.shape
    n_pages = page_table.shape[1]
    page_rows = cache.shape[1]
    assert page_rows == LANES * 2 * N_SB_HEADS and n_pages % ppg == 0
    rows = N_SB_HEADS * n_new

    def page_map(q):
        return lambda b, p, pt: (pt[b * n_pages + n_pages - 1 - (p * ppg + q)], 0, 0)

    grid_spec = pltpu.PrefetchScalarGridSpec(
        num_scalar_prefetch=1,
        grid=(bs, n_pages // ppg),
        in_specs=[pl.BlockSpec((1, n_new, SB_W), lambda b, p, pt: (b, 0, 0)),
                  pl.BlockSpec((1, n_new, 2 * SB_W), lambda b, p, pt: (b, 0, 0))]
                 + [pl.BlockSpec((1, page_rows, HEAD_DIM), page_map(q)) for q in range(ppg)],
        out_specs=pl.BlockSpec((1, n_new, SB_W), lambda b, p, pt: (b, 0, 0)),
        scratch_shapes=[pltpu.VMEM((rows, SB_W), F32), pltpu.VMEM((rows, LANES), F32),
                        pltpu.VMEM((rows, HEAD_DIM), F32)],
    )
    return pl.pallas_call(
        functools.partial(_sb_sample_kernel, n_new=n_new, ppg=ppg),
        grid_spec=grid_spec,
        out_shape=jax.ShapeDtypeStruct((bs, n_new, SB_W), F32),
        compiler_params=_cparams(("arbitrary", "arbitrary")),
        name="sb_sample",
    )(page_table.reshape(-1), qsb, sbrows, *([cache] * ppg))


def _pool_chunks(x, w1, w2):
    x3 = x.reshape(x.shape[0] // CMP_STRIDE, CMP_STRIDE, x.shape[1])
    return jnp.sum(x3 * w1[None], axis=1), jnp.sum(x3 * w2[None], axis=1)


def _compress_kernel(rows_ref, wpos_ref, wc_ref, o_ref, first_ref, second_ref, *, step_rows):
    n_rows = rows_ref.shape[0]
    n_chunks = n_rows // CMP_STRIDE
    w1 = wpos_ref[0, :CMP_STRIDE, :]
    w2 = wpos_ref[0, CMP_STRIDE:, :]
    cps = step_rows // CMP_STRIDE

    def body(s, carry):
        r0 = pl.multiple_of(s * step_rows, step_rows)
        c0 = pl.multiple_of(s * cps, cps)
        f, sec = _pool_chunks(rows_ref[pl.ds(r0, step_rows), :], w1, w2)
        first_ref[pl.ds(c0, cps), :] = f
        second_ref[pl.ds(c0, cps), :] = sec
        return carry

    lax.fori_loop(0, n_rows // step_rows, body, 0)
    pre = first_ref[...] + pltpu.roll(second_ref[...], n_chunks - 1, axis=0)
    for g in range(N_NSA_KV):
        sl = slice(g * HEAD_DIM, (g + 1) * HEAD_DIM)
        o_ref[0, :, sl] = _dot(pre[:, sl].astype(BF16), wc_ref[0, g]).astype(BF16)


def _compress_prompt(nsarows, wpos, wc, step_rows=1024):
    t = nsarows.shape[0]
    n_chunks = t // CMP_STRIDE
    return pl.pallas_call(
        functools.partial(_compress_kernel, step_rows=step_rows),
        grid=(2,),
        in_specs=[pl.BlockSpec((t, KV_W), lambda c: (0, c)),
                  pl.BlockSpec((1, CMP_BLOCK, KV_W), lambda c: (c, 0, 0)),
                  pl.BlockSpec((1, N_NSA_KV, HEAD_DIM, HEAD_DIM), lambda c: (c, 0, 0, 0))],
        out_specs=pl.BlockSpec((1, n_chunks, KV_W), lambda c: (c, 0, 0)),
        out_shape=jax.ShapeDtypeStruct((2, n_chunks, KV_W), BF16),
        scratch_shapes=[pltpu.VMEM((n_chunks, KV_W), F32), pltpu.VMEM((n_chunks, KV_W), F32)],
        compiler_params=_cparams(("arbitrary",)),
        name="compress_prompt",
    )(nsarows, wpos, wc)


def _importance_matrix(n_cmp, n_slc_lanes):
    m = np.arange(n_cmp)[:, None]
    j = np.arange(n_slc_lanes)[None, :]
    d = m - (SLC_BLOCK // CMP_STRIDE) * j
    w = np.where((d >= 0) & (d <= 2), 1.0, np.where((d == -1) | (d == 3), 0.5, 0.0))
    return jnp.asarray(w, BF16)


def _block_expansion(n_keys):
    b = np.arange(LANES)[:, None]
    k = np.arange(n_keys)[None, :]
    return jnp.asarray((b == k // SLC_BLOCK).astype(np.float32), BF16)


def _top_blocks_t(score, n_top):
    blk = lax.broadcasted_iota(jnp.int32, score.shape, 0).astype(F32)
    sel = jnp.zeros(score.shape, F32)
    for _ in range(n_top):
        m = jnp.max(score, axis=0, keepdims=True)
        idx = jnp.min(jnp.where(score == m, blk, float(score.shape[0])), axis=0, keepdims=True)
        pick = blk == idx
        sel = jnp.where(pick, 1.0, sel)
        score = jnp.where(pick, -2.0, score)
    return sel


def _nsa_prompt_t_kernel(q_ref, gate_ref, kc_ref, vct_ref, sk_ref, svt_ref, wk_ref, wvt_ref,
                         et_ref, mimpt_ref, o_ref, cb_ref, *, tq, tk):
    i = pl.program_id(0)
    start = i * tq
    cols = NSA_GROUP * tq
    groups = range(N_NSA_KV)
    gsl = [slice(g * HEAD_DIM, (g + 1) * HEAD_DIM) for g in groups]
    hsl = [slice(h * tq, (h + 1) * tq) for h in range(NSA_GROUP)]
    slope_of = lambda g, h: 2.0 ** -(g * NSA_GROUP + h + 1)
    drop = 2.0 * NEG
    n_cmp = kc_ref.shape[1]

    @pl.when(i == 0)
    def _():
        kio = lax.broadcasted_iota(jnp.int32, (tk, tq), 0).astype(F32)
        for hh in range(N_NSA_HEADS):
            cb_ref[hh] = (2.0 ** -(hh + 1)) * kio

    qpos = start + lax.broadcasted_iota(jnp.int32, (1, tq), 1)
    qpos_f = qpos.astype(F32)

    def head_row(fn):
        return jnp.concatenate([fn(h) for h in range(NSA_GROUP)], axis=1)

    qg, o_cmp, selb = [], [], []
    dist_c = qpos - (CMP_STRIDE * lax.broadcasted_iota(jnp.int32, (n_cmp, tq), 0) + (CMP_BLOCK - 1))
    seen_c = dist_c >= 0
    dist_cf = dist_c.astype(F32)
    for g in groups:
        qg.append(jnp.concatenate(
            [q_ref[:, (g * NSA_GROUP + h) * HEAD_DIM:(g * NSA_GROUP + h + 1) * HEAD_DIM] for h in range(NSA_GROUP)],
            axis=0))
        zc = _dot_nt(kc_ref[0, :, gsl[g]], qg[g]) * ATT_SCALE
        ps = []
        for h in range(NSA_GROUP):
            s = jnp.where(seen_c, zc[:, hsl[h]] - slope_of(g, h) * dist_cf, NEG)
            e = jnp.where(seen_c, jnp.exp(s - jnp.max(s, axis=0, keepdims=True)), 0.0)
            ps.append(e / jnp.maximum(jnp.sum(e, axis=0, keepdims=True), 1e-30))
        o_cmp.append(_dot(vct_ref[gsl[g], :], jnp.concatenate(ps, axis=1).astype(BF16)))
        hi, lo = _split_bf16(ps[0] + ps[1] + ps[2] + ps[3])
        imp = _dot(mimpt_ref[...], hi) + _dot(mimpt_ref[...], lo)
        blk = lax.broadcasted_iota(jnp.int32, imp.shape, 0)
        qb = qpos // SLC_BLOCK
        forced = (blk == 0) | ((blk <= qb) & (blk >= qb - 1))
        score = jnp.where(blk > qb, -1.0, jnp.where(forced, FORCE, imp))
        selb.append(((_top_blocks_t(score, SLC_TOP) - 1.0) * (-drop)).astype(BF16))

    def attend(k_ref, vt_ref, t_lo, t_hi, selected):
        def tile(t, state, edge):
            off = pl.multiple_of(t * tk, tk)
            off_f = off.astype(F32)
            zs = [_dot_nt(k_ref[pl.ds(off, tk), gsl[g]], qg[g]) for g in groups]
            mbs = [_dot(et_ref[t], selb[g]) for g in groups] if selected else None
            if edge:
                dist = qpos - (off + lax.broadcasted_iota(jnp.int32, (tk, tq), 0))
                ok = (dist >= 0) if selected else ((dist >= 0) & (dist < WINDOW))
            out = []
            for g in groups:
                parts = []
                for h in range(NSA_GROUP):
                    sh = zs[g][:, hsl[h]] * ATT_SCALE + cb_ref[g * NSA_GROUP + h]
                    if selected:
                        sh = sh + mbs[g]
                    if edge:
                        sh = jnp.where(ok, sh, drop)
                    parts.append(sh)
                s = jnp.concatenate(parts, axis=1)
                shift = head_row(lambda h: slope_of(g, h) * (qpos_f - off_f))
                m_old, l_old, acc_old = state[g]
                m_new = jnp.maximum(m_old, jnp.max(s, axis=0, keepdims=True) - shift)
                e = jnp.exp(s - (shift + m_new))
                alpha = jnp.exp(m_old - m_new)
                pv = _dot(vt_ref[t, gsl[g], :], e.astype(BF16))
                out.append((m_new, l_old * alpha + jnp.sum(e, axis=0, keepdims=True), acc_old * alpha + pv))
            return tuple(out)

        init = tuple((jnp.full((1, cols), NEG, F32), jnp.zeros((1, cols), F32), jnp.zeros((HEAD_DIM, cols), F32))
                     for _ in groups)
        if selected:
            state = lax.fori_loop(t_lo, t_hi - 1, lambda t, st: tile(t, st, False), init)
            state = tile(t_hi - 1, state, True)
        else:
            state = lax.fori_loop(t_lo, t_hi, lambda t, st: tile(t, st, True), init)
        return [state[g][2] / jnp.maximum(state[g][1], 1e-30) for g in groups]

    last = (start + tq - 1) // tk
    o_slc = attend(sk_ref, svt_ref, 0, last + 1, True)
    o_win = attend(wk_ref, wvt_ref, jnp.maximum(start - WINDOW, 0) // tk, last + 1, False)

    gates_t = gate_ref[...].T
    for g in groups:
        def gate_row(branch):
            return head_row(lambda h: gates_t[branch * N_NSA_HEADS + g * NSA_GROUP + h:
                                              branch * N_NSA_HEADS + g * NSA_GROUP + h + 1, :])
        o = gate_row(0) * o_cmp[g] + gate_row(1) * o_slc[g] + gate_row(2) * o_win[g]
        for h in range(NSA_GROUP):
            c = (g * NSA_GROUP + h) * HEAD_DIM
            o_ref[:, c:c + HEAD_DIM] = o[:, hsl[h]].T


def _nsa_prompt_t(qn, gates, kvc, nsab, winb, tq=128, tk=256):
    t = qn.shape[0]
    n_cmp = kvc.shape[1]
    n_tiles = t // tk
    tiles_t = lambda v: v.T.reshape(KV_W, n_tiles, tk).transpose(1, 0, 2)
    svt = tiles_t(nsab[:, 3 * KV_W:])
    wvt = tiles_t(winb[:, KV_W:])
    vct = kvc[1].T
    et = jnp.transpose(_block_expansion(t)).reshape(n_tiles, tk, LANES)
    mimpt = jnp.transpose(_importance_matrix(n_cmp, LANES))
    once = pl.Buffered(1)
    full = lambda a: pl.BlockSpec(a.shape, lambda i: (0,) * a.ndim, pipeline_mode=once)
    col = lambda c: pl.BlockSpec((t, KV_W), lambda i: (0, c), pipeline_mode=once)
    return pl.pallas_call(
        functools.partial(_nsa_prompt_t_kernel, tq=tq, tk=tk),
        grid=(t // tq,),
        in_specs=[pl.BlockSpec((tq, NSA_W), lambda i: (i, 0)),
                  pl.BlockSpec((tq, LANES), lambda i: (i, 0)),
                  pl.BlockSpec((1, n_cmp, KV_W), lambda i: (0, 0, 0), pipeline_mode=once),
                  full(vct), col(2), full(svt), col(0), full(wvt), full(et), full(mimpt)],
        out_specs=pl.BlockSpec((tq, NSA_W), lambda i: (i, 0)),
        out_shape=jax.ShapeDtypeStruct((t, NSA_W), F32),
        scratch_shapes=[pltpu.VMEM((N_NSA_HEADS, tk, tq), F32)],
        compiler_params=_cparams(("arbitrary",)),
        name="nsa_prompt",
    )(qn, gates, kvc, vct, nsab, svt, winb, wvt, et, mimpt)


def _nsa_sample_kernel(pt_ref, q_ref, gate_ref, newrows_ref, newwin_ref, win_ref, wpos_ref, wc_ref,
                       e_ref, mimp_ref, *refs, n_new, n_pages, ppg):
    cache_refs = refs[:ppg]
    o_ref, winout_ref, first_ref, second_ref, ks_ref, vs_ref = refs[ppg:]
    p = pl.program_id(1)
    page = LANES
    past = n_pages * page
    cpp = page // CMP_STRIDE
    n_cmp = first_ref.shape[1]
    n_keys = ks_ref.shape[0]
    w_buf = win_ref.shape[1] // (2 * N_NSA_KV)
    gsl = [slice(g * HEAD_DIM, (g + 1) * HEAD_DIM) for g in range(N_NSA_KV)]
    kinds = 4 * N_NSA_KV

    @pl.when(p == 0)
    def _():
        first_ref[...] = jnp.zeros_like(first_ref)
        second_ref[...] = jnp.zeros_like(second_ref)

    for q, cache_ref in enumerate(cache_refs):
        pg = p * ppg + q
        c0 = pl.multiple_of(pg * cpp, cpp)
        r0 = pl.multiple_of(pg * page, page)
        for g in range(N_NSA_KV):
            for kv in range(2):
                x = cache_ref[0, pl.ds(kv * N_NSA_KV + g, page, stride=kinds), :]
                f, sec = _pool_chunks(x, wpos_ref[kv, :CMP_STRIDE, gsl[g]], wpos_ref[kv, CMP_STRIDE:, gsl[g]])
                first_ref[kv, pl.ds(c0, cpp), gsl[g]] = f
                second_ref[kv, pl.ds(c0, cpp), gsl[g]] = sec
            ks_ref[pl.ds(r0, page), gsl[g]] = cache_ref[0, pl.ds(2 * N_NSA_KV + g, page, stride=kinds), :].astype(BF16)
            vs_ref[pl.ds(r0, page), gsl[g]] = cache_ref[0, pl.ds(3 * N_NSA_KV + g, page, stride=kinds), :].astype(BF16)

    @pl.when(p == pl.num_programs(1) - 1)
    def _():
        new = newrows_ref[0]
        zpad = jnp.zeros((CMP_STRIDE - n_new, KV_W), F32)
        kvc = []
        for kv in range(2):
            xc = jnp.concatenate([new[:, kv * KV_W:(kv + 1) * KV_W], zpad], axis=0)
            first_ref[kv, past // CMP_STRIDE:past // CMP_STRIDE + 1, :] = jnp.sum(
                xc * wpos_ref[kv, :CMP_STRIDE, :], axis=0, keepdims=True)
            second_ref[kv, past // CMP_STRIDE:past // CMP_STRIDE + 1, :] = jnp.sum(
                xc * wpos_ref[kv, CMP_STRIDE:, :], axis=0, keepdims=True)
            pre = first_ref[kv] + pltpu.roll(second_ref[kv], n_cmp - 1, axis=0)
            kvc.append([_dot(pre[:, gsl[g]].astype(BF16), wc_ref[kv, g]).astype(BF16) for g in range(N_NSA_KV)])
        tail = jnp.zeros((n_keys - past - n_new, KV_W), F32)
        ks_ref[past:, :] = jnp.concatenate([new[:, 2 * KV_W:3 * KV_W], tail], axis=0).astype(BF16)
        vs_ref[past:, :] = jnp.concatenate([new[:, 3 * KV_W:4 * KV_W], tail], axis=0).astype(BF16)

        nwin = newwin_ref[0]
        wr = 2 * N_NSA_KV
        winout_ref[0, :(w_buf - n_new) * wr, :] = win_ref[0, n_new * wr:, :]
        for c in range(wr):
            winout_ref[0, pl.ds((w_buf - n_new) * wr + c, n_new, stride=wr), :] = nwin[:, c * HEAD_DIM:(c + 1) * HEAD_DIM]
        wtail = jnp.zeros((LANES - n_new, HEAD_DIM), F32)

        def window(c):
            return jnp.concatenate([win_ref[0, pl.ds(c, w_buf, stride=wr), :],
                                    nwin[:, c * HEAD_DIM:(c + 1) * HEAD_DIM], wtail], axis=0).astype(BF16)

        n_win = w_buf + LANES
        rows = NSA_GROUP * n_new
        r = lax.broadcasted_iota(jnp.int32, (rows, 1), 0)
        qpos = past + r % n_new
        qpos_q = past + lax.broadcasted_iota(jnp.int32, (n_new, 1), 0)
        kc_end = CMP_STRIDE * lax.broadcasted_iota(jnp.int32, (1, n_cmp), 1) + (CMP_BLOCK - 1)
        dist_s = qpos - lax.broadcasted_iota(jnp.int32, (1, n_keys), 1)
        dist_w = qpos - (past - w_buf + lax.broadcasted_iota(jnp.int32, (1, n_win), 1))
        gates = gate_ref[0]

        groups = range(N_NSA_KV)
        qgs = [jnp.concatenate(
            [q_ref[0, :, (g * NSA_GROUP + h) * HEAD_DIM:(g * NSA_GROUP + h + 1) * HEAD_DIM]
             for h in range(NSA_GROUP)], axis=0) for g in groups]
        slopes = [_head_slope(g * NSA_GROUP + r // n_new) for g in groups]
        z_cmp = [_dot_nt(qgs[g], kvc[0][g]) for g in groups]
        z_slc = [_dot_nt(qgs[g], ks_ref[:, gsl[g]]) for g in groups]
        z_win = [_dot_nt(qgs[g], window(g)) for g in groups]
        pcs = [_masked_softmax(z_cmp[g] * ATT_SCALE - slopes[g] * (qpos - kc_end).astype(F32), kc_end <= qpos)
               for g in groups]
        o_cmps = [_dot(pcs[g].astype(BF16), kvc[1][g]) for g in groups]
        imps = []
        for pc in pcs:
            p4 = pc[0:n_new] + pc[n_new:2 * n_new] + pc[2 * n_new:3 * n_new] + pc[3 * n_new:4 * n_new]
            hi, lo = _split_bf16(p4)
            imps.append(_dot(hi, mimp_ref[...]) + _dot(lo, mimp_ref[...]))
        n_slc = -(-(past + n_new) // SLC_BLOCK)
        sel_all = _top_blocks_by_rank(
            _slc_scores(jnp.concatenate(imps, axis=0), jnp.concatenate([qpos_q] * N_NSA_KV, axis=0)), SLC_TOP, n_slc)
        keeps = [_dot(jnp.concatenate([sel_all[g * n_new:(g + 1) * n_new]] * NSA_GROUP, axis=0).astype(BF16),
                      e_ref[...]) > 0.5 for g in groups]
        pss = [_masked_softmax(z_slc[g] * ATT_SCALE - slopes[g] * dist_s.astype(F32), (dist_s >= 0) & keeps[g])
               for g in groups]
        pws = [_masked_softmax(z_win[g] * ATT_SCALE - slopes[g] * dist_w.astype(F32),
                               (dist_w >= 0) & (dist_w < WINDOW)) for g in groups]
        o_slcs = [_dot(pss[g].astype(BF16), vs_ref[:, gsl[g]]) for g in groups]
        o_wins = [_dot(pws[g].astype(BF16), window(N_NSA_KV + g)) for g in groups]

        for g in groups:
            o = (_gate_cols(gates, 0, g, n_new) * o_cmps[g] + _gate_cols(gates, 1, g, n_new) * o_slcs[g]
                 + _gate_cols(gates, 2, g, n_new) * o_wins[g])
            for h in range(NSA_GROUP):
                c = (g * NSA_GROUP + h) * HEAD_DIM
                o_ref[0, :, c:c + HEAD_DIM] = o[h * n_new:(h + 1) * n_new]


def _nsa_sample(page_table, qn, gates, nsarows, winrows, cache, state_win, wpos, wc, ppg=8):
    bs, n_new, _ = qn.shape
    n_pages = page_table.shape[1]
    page_rows = cache.shape[1]
    assert page_rows == LANES * 4 * N_NSA_KV and n_pages % ppg == 0
    past = n_pages * LANES
    win_rows = state_win.shape[1]
    n_keys = past + LANES
    n_cmp = 2 * (past // CMP_STRIDE)
    e = _block_expansion(n_keys)
    mimp = _importance_matrix(n_cmp, LANES)
    seq = lambda b, p, pt: (b, 0, 0)
    c3 = lambda b, p, pt: (0, 0, 0)

    def page_map(q):
        return lambda b, p, pt: (pt[b * n_pages + p * ppg + q], 0, 0)

    grid_spec = pltpu.PrefetchScalarGridSpec(
        num_scalar_prefetch=1,
        grid=(bs, n_pages // ppg),
        in_specs=[pl.BlockSpec((1, n_new, NSA_W), seq),
                  pl.BlockSpec((1, n_new, LANES), seq),
                  pl.BlockSpec((1, n_new, 4 * KV_W), seq),
                  pl.BlockSpec((1, n_new, 2 * KV_W), seq),
                  pl.BlockSpec((1, win_rows, HEAD_DIM), seq),
                  pl.BlockSpec(wpos.shape, c3),
                  pl.BlockSpec(wc.shape, lambda b, p, pt: (0, 0, 0, 0)),
                  pl.BlockSpec(e.shape, lambda b, p, pt: (0, 0)),
                  pl.BlockSpec(mimp.shape, lambda b, p, pt: (0, 0))]
                 + [pl.BlockSpec((1, page_rows, HEAD_DIM), page_map(q)) for q in range(ppg)],
        out_specs=[pl.BlockSpec((1, n_new, NSA_W), seq),
                   pl.BlockSpec((1, win_rows, HEAD_DIM), seq)],
        scratch_shapes=[pltpu.VMEM((2, n_cmp, KV_W), F32), pltpu.VMEM((2, n_cmp, KV_W), F32),
                        pltpu.VMEM((n_keys, KV_W), BF16), pltpu.VMEM((n_keys, KV_W), BF16)],
    )
    return pl.pallas_call(
        functools.partial(_nsa_sample_kernel, n_new=n_new, n_pages=n_pages, ppg=ppg),
        grid_spec=grid_spec,
        out_shape=[jax.ShapeDtypeStruct((bs, n_new, NSA_W), F32),
                   jax.ShapeDtypeStruct((bs, win_rows, HEAD_DIM), F32)],
        compiler_params=_cparams(("arbitrary", "arbitrary")),
        name="nsa_sample",
    )(page_table.reshape(-1), qn, gates, nsarows, winrows, state_win, wpos, wc, e, mimp, *([cache] * ppg))


def _mixout_kernel(osb_ref, on_ref, x_ref, gout_ref, w_ref, gate_ref, g2_ref, shift_ref, scale_ref, *rest):
    x1_ref, h2_ref, h2g_ref, nrm_ref = rest[-4:]
    tm = x_ref.shape[0]
    for h in range(N_SB_HEADS + N_NSA_HEADS):
        src = osb_ref if h < N_SB_HEADS else on_ref
        c = (h % N_SB_HEADS) * HEAD_DIM
        sl = slice(h * HEAD_DIM, (h + 1) * HEAD_DIM)
        nrm_ref[:, sl] = (_rms(src[:, c:c + HEAD_DIM]) * gout_ref[:, sl]).astype(BF16)
    x1 = x_ref[...] + gate_ref[...] * _dot(nrm_ref[...], w_ref[...])
    x1_ref[...] = x1
    h2 = _rms(x1) * g2_ref[...] * (1.0 + scale_ref[...]) + shift_ref[...]
    h2_ref[...] = h2.astype(BF16)
    for s in range(ROW_WORDS):
        lo = h2[:, (2 * s) * LANES:(2 * s + 1) * LANES].astype(BF16).astype(F32)
        hi = h2[:, (2 * s + 1) * LANES:(2 * s + 2) * LANES].astype(BF16).astype(F32)
        word = (lax.bitcast_convert_type(lo, U32) >> 16) | lax.bitcast_convert_type(hi, U32)
        h2g_ref[pl.ds(s, tm, stride=ROW_WORDS), :] = word


def _mixout(osb, on, x, gout, w_out, gate, g2, shift, scale, n_total, row0, prev=None, tm=256):
    m, d = x.shape
    blk0 = row0 // tm
    row = lambda i: (i, 0)
    out_row = lambda i: (blk0 + i, 0)
    const = lambda i: (0, 0)
    mod_spec = lambda a: pl.BlockSpec((tm, d), row) if a.shape[0] == m else pl.BlockSpec((1, d), const)
    in_specs = [pl.BlockSpec((tm, SB_W), row), pl.BlockSpec((tm, NSA_W), row), pl.BlockSpec((tm, d), row),
                pl.BlockSpec((1, d), const),
                pl.BlockSpec(w_out.shape, const, pipeline_mode=pl.Buffered(1)),
                mod_spec(gate), pl.BlockSpec((1, d), const), mod_spec(shift), mod_spec(scale)]
    args = [osb, on, x, gout, w_out, gate, g2, shift, scale]
    aliases = {}
    if prev is not None:
        in_specs += [pl.BlockSpec(memory_space=pl.ANY)] * 3
        aliases = {len(args) + j: j for j in range(3)}
        args += list(prev)
    return pl.pallas_call(
        _mixout_kernel,
        grid=(m // tm,),
        in_specs=in_specs,
        out_specs=[pl.BlockSpec((tm, d), out_row), pl.BlockSpec((tm, d), out_row),
                   pl.BlockSpec((tm * ROW_WORDS, LANES), out_row)],
        out_shape=[jax.ShapeDtypeStruct((n_total, d), F32), jax.ShapeDtypeStruct((n_total, d), BF16),
                   jax.ShapeDtypeStruct((n_total * ROW_WORDS, LANES), U32)],
        scratch_shapes=[pltpu.VMEM((tm, d), BF16)],
        input_output_aliases=aliases,
        compiler_params=_cparams(("arbitrary",)),
        name="mixout",
    )(*args)


def _first_max(v, idx, axis, n):
    m = jnp.max(v, axis=axis, keepdims=True)
    return m, jnp.min(jnp.where(v == m, idx, float(n)), axis=axis, keepdims=True)


def _router_kernel(h_ref, wrt_ref, bias_ref, eidx_ref, w_ref, routed_ref):
    tm = h_ref.shape[0]
    per = N_EXPERTS // N_EXPERT_GROUPS
    s = _sigmoid(_dot_nt(wrt_ref[...], h_ref[...]))
    biased = s + bias_ref[...]
    b3 = biased.reshape(N_EXPERT_GROUPS, per, tm)
    r3 = lax.broadcasted_iota(jnp.int32, b3.shape, 1).astype(F32)
    m1, i1 = _first_max(b3, r3, 1, per)
    m2 = jnp.max(jnp.where(r3 == i1, -jnp.inf, b3), axis=1, keepdims=True)
    gscore = (m1 + m2).reshape(N_EXPERT_GROUPS, tm)
    gi = lax.broadcasted_iota(jnp.int32, gscore.shape, 0).astype(F32)
    gmask = jnp.zeros(gscore.shape, F32)
    for _ in range(TOPK_GROUPS):
        _, idx = _first_max(gscore, gi, 0, N_EXPERT_GROUPS)
        pick = gi == idx
        gmask = jnp.where(pick, 1.0, gmask)
        gscore = jnp.where(pick, -jnp.inf, gscore)
    cand = jnp.where(gmask.reshape(N_EXPERT_GROUPS, 1, tm) > 0.5, b3, -jnp.inf).reshape(N_EXPERTS, tm)
    ei = lax.broadcasted_iota(jnp.int32, cand.shape, 0).astype(F32)
    ids, ws = [], []
    routed = jnp.zeros(cand.shape, F32)
    for _ in range(TOP_K):
        _, idx = _first_max(cand, ei, 0, N_EXPERTS)
        pick = ei == idx
        ids.append(idx)
        ws.append(jnp.sum(jnp.where(pick, s, 0.0), axis=0, keepdims=True))
        cand = jnp.where(pick, -jnp.inf, cand)
        routed = jnp.where(pick, 1.0, routed)
    w = jnp.concatenate(ws, axis=0)
    eidx_ref[...] = jnp.concatenate(ids, axis=0).astype(jnp.int32)
    w_ref[...] = w / jnp.sum(w, axis=0, keepdims=True) * ROUTED_SCALE
    routed_ref[...] = routed.astype(BF16)


def _router(h2, w_router_t, router_bias, tm=512):
    m, d = h2.shape
    return pl.pallas_call(
        _router_kernel,
        grid=(m // tm,),
        in_specs=[pl.BlockSpec((tm, d), lambda i: (i, 0)),
                  pl.BlockSpec((N_EXPERTS, d), lambda i: (0, 0)),
                  pl.BlockSpec((N_EXPERTS, 1), lambda i: (0, 0))],
        out_specs=[pl.BlockSpec((TOP_K, tm), lambda i: (0, i)), pl.BlockSpec((TOP_K, tm), lambda i: (0, i)),
                   pl.BlockSpec((N_EXPERTS, tm), lambda i: (0, i))],
        out_shape=[jax.ShapeDtypeStruct((TOP_K, m), jnp.int32), jax.ShapeDtypeStruct((TOP_K, m), F32),
                   jax.ShapeDtypeStruct((N_EXPERTS, m), BF16)],
        compiler_params=_cparams(("arbitrary",)),
        name="router",
    )(h2, w_router_t, router_bias.reshape(N_EXPERTS, 1))


def _slots_kernel(eidx_ref, base_ref, o_ref):
    base = base_ref[...]
    ei = lax.broadcasted_iota(jnp.int32, base.shape, 0)
    rows = [jnp.sum(jnp.where(ei == eidx_ref[k:k + 1, :], base, 0.0), axis=0, keepdims=True) for k in range(TOP_K)]
    o_ref[...] = jnp.concatenate(rows, axis=0).astype(jnp.int32)


def _slots(eidx_t, base_t, tm=512):
    m = eidx_t.shape[1]
    return pl.pallas_call(
        _slots_kernel,
        grid=(m // tm,),
        in_specs=[pl.BlockSpec((TOP_K, tm), lambda i: (0, i)), pl.BlockSpec((N_EXPERTS, tm), lambda i: (0, i))],
        out_specs=pl.BlockSpec((TOP_K, tm), lambda i: (0, i)),
        out_shape=jax.ShapeDtypeStruct((TOP_K, m), jnp.int32),
        compiler_params=_cparams(("arbitrary",)),
        name="slots",
    )(eidx_t, base_t)


def _dispatch_plan(eidx_t, routed_t):
    t = eidx_t.shape[1]
    ck = 128
    r3 = routed_t.reshape(N_EXPERTS, t // ck, ck)
    upto = jnp.asarray(np.triu(np.ones((ck, ck), np.float32)), BF16)
    within = jnp.einsum('ecj,ji->eci', r3, upto, preferred_element_type=F32)
    tot = within[:, :, -1]
    incl = (within + (jnp.cumsum(tot, axis=1) - tot)[:, :, None]).reshape(N_EXPERTS, t)
    counts = incl[:, -1].astype(jnp.int32)
    padded = (counts + MOE_ROWS - 1) // MOE_ROWS * MOE_ROWS
    pad_end = jnp.cumsum(padded)
    base_t = (pad_end - padded).astype(F32)[:, None] + incl - routed_t.astype(F32)
    dest = _slots(eidx_t, base_t).T
    n_blk = -(-(t * TOP_K) // MOE_ROWS) + N_EXPERTS
    blk_start = (jnp.arange(n_blk) * MOE_ROWS)[:, None]
    blk_e = jnp.minimum(jnp.sum(pad_end[None, :] <= blk_start, axis=1), N_EXPERTS - 1).astype(jnp.int32)
    n_active = (pad_end[-1] // MOE_ROWS).astype(jnp.int32)
    nxt_blk = (pad_end[blk_e] // MOE_ROWS).astype(jnp.int32)
    nxt_e = jnp.where(nxt_blk < n_active, blk_e[jnp.minimum(nxt_blk, n_blk - 1)], -1).astype(jnp.int32)
    first = jnp.concatenate([jnp.ones((1,), jnp.int32), (blk_e[1:] != blk_e[:-1]).astype(jnp.int32)])
    par = ((jnp.cumsum(first) - 1) % 2).astype(jnp.int32)
    return dest, blk_e, nxt_e, par, n_active.reshape(1), n_blk


def _dispatch_kernel(dest_ref, h2g_ref, xs_ref, sem, *, tm):
    i = pl.program_id(0)

    def body(r, carry):
        src = h2g_ref.at[pl.ds(pl.multiple_of(r * ROW_WORDS, ROW_WORDS), ROW_WORDS), :]
        for k in range(TOP_K):
            d = dest_ref[(i * tm + r) * TOP_K + k]
            pltpu.make_async_copy(src, xs_ref.at[pl.ds(pl.multiple_of(d * ROW_WORDS, ROW_WORDS), ROW_WORDS), :],
                                  sem).start()
        return carry

    lax.fori_loop(0, tm, body, 0)
    done = xs_ref.at[pl.ds(0, tm * TOP_K * ROW_WORDS), :]
    pltpu.make_async_copy(done, done, sem).wait()


def _dispatch(dest_flat, h2g, p_rows, tm=256):
    n_tok = h2g.shape[0] // ROW_WORDS
    grid_spec = pltpu.PrefetchScalarGridSpec(
        num_scalar_prefetch=1,
        grid=(n_tok // tm,),
        in_specs=[pl.BlockSpec((tm * ROW_WORDS, LANES), lambda i, d: (i, 0))],
        out_specs=pl.BlockSpec(memory_space=pl.ANY),
        scratch_shapes=[pltpu.SemaphoreType.DMA(())],
    )
    return pl.pallas_call(
        functools.partial(_dispatch_kernel, tm=tm),
        grid_spec=grid_spec,
        out_shape=jax.ShapeDtypeStruct((p_rows * ROW_WORDS, LANES), U32),
        compiler_params=_cparams(("arbitrary",)),
        name="dispatch",
    )(dest_flat, h2g)


def _experts_kernel(be_ref, nxt_ref, par_ref, nact_ref, x_ref, wg_hbm, wu_hbm, wd_hbm, o_ref,
                    wgf, wuf, wdf, wgb, wub, wdb, sem):
    b = pl.program_id(0)
    active = b < nact_ref[0]
    e = be_ref[b]
    slot = par_ref[b]
    changed = e != be_ref[jnp.maximum(b - 1, 0)]

    def weight_copies(eid, s):
        return [pltpu.make_async_copy(src.at[eid], dst.at[s], sem.at[s, j])
                for j, (src, dst) in enumerate(((wg_hbm, wgf), (wu_hbm, wuf), (wd_hbm, wdf)))]

    @pl.when(active & (b == 0))
    def _():
        for cp in weight_copies(e, slot):
            cp.start()

    @pl.when(active & ((b == 0) | changed))
    def _():
        for cp in weight_copies(e, slot):
            cp.wait()
        nxt = nxt_ref[b]

        @pl.when(nxt >= 0)
        def _():
            for cp in weight_copies(nxt, 1 - slot):
                cp.start()

        ch = 256
        for c in range(0, D_MODEL, ch):
            wgb[c:c + ch, :] = wgf[slot, c:c + ch, :].astype(BF16)
            wub[c:c + ch, :] = wuf[slot, c:c + ch, :].astype(BF16)
        for c in range(0, D_EXPERT, ch // 4):
            wdb[c:c + ch // 4, :] = wdf[slot, c:c + ch // 4, :].astype(BF16)

    @pl.when(active)
    def _():
        chunks = []
        for s in range(ROW_WORDS):
            word = x_ref[pl.ds(s, MOE_ROWS, stride=ROW_WORDS), :]
            chunks.append(lax.bitcast_convert_type(word << 16, F32).astype(BF16))
            chunks.append(lax.bitcast_convert_type(word & jnp.uint32(0xFFFF0000), F32).astype(BF16))
        x = jnp.concatenate(chunks, axis=1)
        gt = _dot(x, wgb[...])
        a = (gt * _sigmoid(gt) * _dot(x, wub[...])).astype(BF16)
        y = _dot(a, wdb[...])
        for j in range(D_CHUNKS):
            o_ref[pl.ds(j, MOE_ROWS, stride=D_CHUNKS), :] = y[:, j * LANES:(j + 1) * LANES]


def _experts(blk_e, nxt_e, par, n_active, x_sorted, wg, wu, wd):
    d = wg.shape[1]
    n_blk = x_sorted.shape[0] // (MOE_ROWS * ROW_WORDS)
    hbm = pl.BlockSpec(memory_space=pl.ANY)
    grid_spec = pltpu.PrefetchScalarGridSpec(
        num_scalar_prefetch=4,
        grid=(n_blk,),
        in_specs=[pl.BlockSpec((MOE_ROWS * ROW_WORDS, LANES), lambda b, be, nx, pa, na: (jnp.minimum(b, na[0] - 1), 0)),
                  hbm, hbm, hbm],
        out_specs=pl.BlockSpec((MOE_ROWS * D_CHUNKS, LANES), lambda b, be, nx, pa, na: (jnp.minimum(b, na[0] - 1), 0)),
        scratch_shapes=[pltpu.VMEM((2, d, D_EXPERT), F32), pltpu.VMEM((2, d, D_EXPERT), F32),
                        pltpu.VMEM((2, D_EXPERT, d), F32),
                        pltpu.VMEM((d, D_EXPERT), BF16), pltpu.VMEM((d, D_EXPERT), BF16),
                        pltpu.VMEM((D_EXPERT, d), BF16),
                        pltpu.SemaphoreType.DMA((2, 3))],
    )
    return pl.pallas_call(
        _experts_kernel,
        grid_spec=grid_spec,
        out_shape=jax.ShapeDtypeStruct((n_blk * MOE_ROWS * D_CHUNKS, LANES), F32),
        compiler_params=_cparams(("arbitrary",)),
        name="experts",
    )(blk_e, nxt_e, par, n_active, x_sorted, wg, wu, wd)


def _final_kernel(dest_ref, x1_ref, h2_ref, w8_ref, ys_ref, wg_ref, wu_ref, wd_ref, gate_ref, gf_ref,
                  shift_ref, scale_ref, y_ref, ybuf, ymoe_ref, sem, *, tm, row0):
    i = pl.program_id(0)
    n = pl.num_programs(0)
    tile_rows = tm * TOP_K * D_CHUNKS

    def issue(tile, slot):
        def body(r, carry):
            for k in range(TOP_K):
                d = dest_ref[(row0 + tile * tm + r) * TOP_K + k]
                pltpu.make_async_copy(
                    ys_ref.at[pl.ds(pl.multiple_of(d * D_CHUNKS, D_CHUNKS), D_CHUNKS), :],
                    ybuf.at[slot, pl.ds(pl.multiple_of((k * tm + r) * D_CHUNKS, D_CHUNKS), D_CHUNKS), :],
                    sem.at[slot]).start()
            return carry
        lax.fori_loop(0, tm, body, 0)

    @pl.when(i == 0)
    def _():
        issue(0, 0)

    @pl.when(i + 1 < n)
    def _():
        issue(i + 1, (i + 1) % 2)

    slot = i % 2
    pltpu.make_async_copy(ys_ref.at[pl.ds(0, tile_rows), :], ybuf.at[slot], sem.at[slot]).wait()
    w8 = w8_ref[...]
    wk = [w8[:, k:k + 1] for k in range(TOP_K)]
    for j in range(D_CHUNKS):
        acc = None
        for k in range(TOP_K):
            v = ybuf[slot, pl.ds(k * tm * D_CHUNKS + j, tm, stride=D_CHUNKS), :] * wk[k]
            acc = v if acc is None else acc + v
        ymoe_ref[:, j * LANES:(j + 1) * LANES] = acc

    h = h2_ref[...]
    gt = _dot(h, wg_ref[...])
    a = (gt * _sigmoid(gt) * _dot(h, wu_ref[...])).astype(BF16)
    f = ymoe_ref[...] + _dot(a, wd_ref[...])
    x2 = x1_ref[...] + gate_ref[...] * f
    y_ref[...] = _rms(x2) * gf_ref[...] * (1.0 + scale_ref[...]) + shift_ref[...]


def _final(dest_flat, x1, h2, w8, y_sorted, wg, wu, wd, gate, gf, shift, scale, m, row0, tm=64):
    d = x1.shape[1]
    blk0 = row0 // tm
    in_row = lambda i, dr: (blk0 + i, 0)
    row = lambda i, dr: (i, 0)
    const = lambda i, dr: (0, 0)
    mod_spec = lambda a: pl.BlockSpec((tm, d), row) if a.shape[0] == m else pl.BlockSpec((1, d), const)
    grid_spec = pltpu.PrefetchScalarGridSpec(
        num_scalar_prefetch=1,
        grid=(m // tm,),
        in_specs=[pl.BlockSpec((tm, d), in_row), pl.BlockSpec((tm, d), in_row), pl.BlockSpec((tm, TOP_K), in_row),
                  pl.BlockSpec(memory_space=pl.ANY),
                  pl.BlockSpec(wg.shape, const), pl.BlockSpec(wu.shape, const), pl.BlockSpec(wd.shape, const),
                  mod_spec(gate), pl.BlockSpec((1, d), const), mod_spec(shift), mod_spec(scale)],
        out_specs=pl.BlockSpec((tm, d), row),
        scratch_shapes=[pltpu.VMEM((2, tm * TOP_K * D_CHUNKS, LANES), F32), pltpu.VMEM((tm, d), F32),
                        pltpu.SemaphoreType.DMA((2,))],
    )
    return pl.pallas_call(
        functools.partial(_final_kernel, tm=tm, row0=row0),
        grid_spec=grid_spec,
        out_shape=jax.ShapeDtypeStruct((m, d), F32),
        compiler_params=_cparams(("arbitrary",)),
        name="final",
    )(dest_flat, x1, h2, w8, y_sorted, wg, wu, wd, gate, gf, shift, scale)


def kernel(x_prompt, x_sample, cache_sb, cache_nsa, state_win, page_table, c_prompt, c_sample, w_ada, b_ada, norm1_g, w_in, cmp_pos_k, cmp_pos_v, cmp_wk, cmp_wv, out_norm_g, w_out, norm2_g, w_router, router_bias, w_gate_e, w_up_e, w_down_e, w_gate_s, w_up_s, w_down_s, normf_g):
    b_p, t, d = x_prompt.shape
    bs, n_new, _ = x_sample.shape
    assert b_p == 1 and d == D_MODEL
    n_s = bs * n_new
    n_tok = t + n_s

    c_all = jnp.concatenate([c_prompt, c_sample], axis=0)
    m_pad = -(-c_all.shape[0] // 8) * 8
    c_all = jnp.pad(c_all, ((0, m_pad - c_all.shape[0]), (0, 0)))
    mod = _ada(c_all, w_ada, b_ada).reshape(m_pad, N_ADA, d)
    mod_p = [mod[0:1, i] for i in range(N_ADA)]
    mod_s = [jnp.repeat(mod[1:1 + bs, i], n_new, axis=0) for i in range(N_ADA)]

    w_main = w_in[:, :COL_GATE]
    w_gate = jnp.pad(w_in[:, COL_GATE:], ((0, 0), (0, LANES - N_GATE)))
    w_cat = jnp.concatenate([w_main, w_gate], axis=1).astype(BF16)
    row1 = lambda v: v.reshape(1, -1)

    xp = x_prompt.reshape(t, d)
    xs = x_sample.reshape(n_s, d)
    (qsb_p, sbrows_p, kvsb_p, qn_p, nsarows_p, nsab_p, winrows_p, winb_p, gates_p) = _project(
        xp, row1(norm1_g), mod_p[0], mod_p[1], w_cat)
    (qsb_s, sbrows_s, _, qn_s, nsarows_s, _, winrows_s, _, gates_s) = _project(
        xs, row1(norm1_g), mod_s[0], mod_s[1], w_cat)

    osb_p = _sb_prompt(qsb_p, kvsb_p)
    wpos = jnp.stack([jnp.repeat(cmp_pos_k, HEAD_DIM, axis=1), jnp.repeat(cmp_pos_v, HEAD_DIM, axis=1)])
    wc = jnp.stack([cmp_wk, cmp_wv]).astype(BF16)
    kvc_p = _compress_prompt(nsarows_p, wpos, wc)
    on_p = _nsa_prompt_t(qn_p, gates_p, kvc_p, nsab_p, winb_p)

    per_seq = lambda a: a.reshape(bs, n_new, a.shape[-1])
    n_phys = cache_sb.shape[0]
    w_buf = state_win.shape[1]
    osb_s = _sb_sample(page_table, per_seq(qsb_s), per_seq(sbrows_s), cache_sb.reshape(n_phys, -1, HEAD_DIM))
    on_s, win_s = _nsa_sample(page_table, per_seq(qn_s), per_seq(gates_s), per_seq(nsarows_s), per_seq(winrows_s),
                              cache_nsa.reshape(n_phys, -1, HEAD_DIM), state_win.reshape(bs, -1, HEAD_DIM), wpos, wc)

    w_out_b = w_out.astype(BF16)
    gout = out_norm_g.reshape(1, -1)
    bufs = _mixout(osb_p, on_p, xp, gout, w_out_b, mod_p[2], row1(norm2_g), mod_p[3], mod_p[4], n_tok, 0)
    x1, h2, h2g = _mixout(osb_s.reshape(n_s, SB_W), on_s.reshape(n_s, NSA_W), xs, gout, w_out_b,
                          mod_s[2], row1(norm2_g), mod_s[3], mod_s[4], n_tok, t, prev=bufs)

    eidx_t, w_t, routed_t = _router(h2, w_router.T.astype(BF16), router_bias)
    dest, blk_e, nxt_e, par, n_active, n_blk = _dispatch_plan(eidx_t, routed_t)
    dest_flat = dest.reshape(-1)
    x_sorted = _dispatch(dest_flat, h2g, n_blk * MOE_ROWS)
    y_sorted = _experts(blk_e, nxt_e, par, n_active, x_sorted, w_gate_e, w_up_e, w_down_e)
    w8 = w_t.T
    ws = (w_gate_s.astype(BF16), w_up_s.astype(BF16), w_down_s.astype(BF16))
    y_p = _final(dest_flat, x1, h2, w8, y_sorted, *ws, mod_p[5], row1(normf_g), mod_p[6], mod_p[7], t, 0)
    y_s = _final(dest_flat, x1, h2, w8, y_sorted, *ws, mod_s[5], row1(normf_g), mod_s[6], mod_s[7], n_s, t)

    keep = min(WINDOW, t)
    return (y_p.reshape(1, t, d), y_s.reshape(bs, n_new, d),
            sbrows_p.reshape(1, t, 2, N_SB_HEADS, HEAD_DIM), sbrows_s.reshape(bs, n_new, 2, N_SB_HEADS, HEAD_DIM),
            nsarows_p.reshape(1, t, 4, N_NSA_KV, HEAD_DIM), nsarows_s.reshape(bs, n_new, 4, N_NSA_KV, HEAD_DIM),
            winrows_p[t - keep:].reshape(1, keep, 2, N_NSA_KV, HEAD_DIM),
            win_s.reshape(bs, w_buf, 2, N_NSA_KV, HEAD_DIM))
```

```python
import functools

import numpy as np
import jax
import jax.numpy as jnp
from jax import lax
from jax.experimental import pallas as pl
from jax.experimental.pallas import tpu as pltpu

F32 = jnp.float32
BF16 = jnp.bfloat16
U32 = jnp.uint32

D_MODEL = 2048
HEAD_DIM = 128
N_SB_HEADS = 8
N_NSA_HEADS = 8
N_NSA_KV = 2
NSA_GROUP = N_NSA_HEADS // N_NSA_KV
CMP_BLOCK = 32
CMP_STRIDE = 16
SLC_BLOCK = 64
SLC_TOP = 16
WINDOW = 512
FORCE = 1e4
NEG = -1e30
N_EXPERTS = 256
TOP_K = 8
N_EXPERT_GROUPS = 8
TOPK_GROUPS = 4
D_EXPERT = D_MODEL // 4
ROUTED_SCALE = 2.5
MOE_ROWS = 128
N_ADA = 8
EPS = 1e-6
ATT_SCALE = HEAD_DIM ** -0.5

SB_W = N_SB_HEADS * HEAD_DIM
NSA_W = N_NSA_HEADS * HEAD_DIM
KV_W = N_NSA_KV * HEAD_DIM
COL_QSB = 0
COL_KVSB = SB_W
COL_QN = 3 * SB_W
COL_NSA = COL_QN + NSA_W
COL_WIN = COL_NSA + 4 * KV_W
COL_GATE = COL_WIN + 2 * KV_W
N_GATE = 3 * N_NSA_HEADS
LANES = 128
V7X_VMEM_LIMIT = 56 * 1024 * 1024
D_CHUNKS = D_MODEL // LANES
ROW_WORDS = D_CHUNKS // 2

NT = (((1,), (1,)), ((), ()))


def _cparams(sem):
    return pltpu.CompilerParams(dimension_semantics=sem, vmem_limit_bytes=V7X_VMEM_LIMIT)


def _sigmoid(x):
    return 1.0 / (1.0 + jnp.exp(-x))


def _split_bf16(x):
    hi = x.astype(BF16)
    lo = (x - hi.astype(F32)).astype(BF16)
    return hi, lo


def _dot(a, b):
    return jnp.dot(a, b, preferred_element_type=F32)


def _dot_nt(a, b):
    return lax.dot_general(a, b, NT, preferred_element_type=F32)


def _rms(x):
    return x * lax.rsqrt(jnp.mean(x * x, axis=-1, keepdims=True) + EPS)


def _masked_softmax(s, mask):
    s = jnp.where(mask, s, NEG)
    m = jnp.max(s, axis=-1, keepdims=True)
    e = jnp.where(mask, jnp.exp(s - m), 0.0)
    return e / jnp.maximum(jnp.sum(e, axis=-1, keepdims=True), 1e-30)


def _head_slope(head):
    out = jnp.full(head.shape, 2.0 ** -N_NSA_HEADS, F32)
    for i in range(N_NSA_HEADS - 1):
        out = jnp.where(head == i, 2.0 ** -(i + 1), out)
    return out


def _top_blocks_by_rank(score, n_top, n_cand):
    blk = lax.broadcasted_iota(jnp.int32, score.shape, 1)
    rank = jnp.zeros(score.shape, F32)
    for i in range(n_cand):
        c = score[:, i:i + 1]
        tie = jnp.where(blk > i, 1.0, 0.0)
        rank = rank + jnp.where(c > score, 1.0, jnp.where(c == score, tie, 0.0))
    return jnp.where(rank < n_top, 1.0, 0.0)


def _slc_scores(imp, qpos):
    blk = lax.broadcasted_iota(jnp.int32, imp.shape, 1)
    qb = qpos // SLC_BLOCK
    forced = (blk == 0) | ((blk <= qb) & (blk >= qb - 1))
    return jnp.where(blk > qb, -1.0, jnp.where(forced, FORCE, imp))


def _gate_cols(gates, branch, group, n_rows_per_head):
    lane = lax.broadcasted_iota(jnp.int32, gates.shape, 1)
    cols = [jnp.sum(jnp.where(lane == branch * N_NSA_HEADS + group * NSA_GROUP + h, gates, 0.0),
                    axis=-1, keepdims=True) for h in range(NSA_GROUP)]
    return jnp.concatenate(cols, axis=0)


def _ada_kernel(c_ref, w_ref, b_ref, o_ref):
    c = c_ref[...]
    a = (c * _sigmoid(c)).astype(BF16)
    o_ref[...] = _dot(a, w_ref[...].astype(BF16)) + b_ref[...]


def _ada(c, w_ada, b_ada, tn=1024):
    m, d = c.shape
    n = w_ada.shape[1]
    return pl.pallas_call(
        _ada_kernel,
        grid=(n // tn,),
        in_specs=[pl.BlockSpec((m, d), lambda j: (0, 0)),
                  pl.BlockSpec((d, tn), lambda j: (0, j)),
                  pl.BlockSpec((1, tn), lambda j: (0, j))],
        out_specs=pl.BlockSpec((m, tn), lambda j: (0, j)),
        out_shape=jax.ShapeDtypeStruct((m, n), F32),
        compiler_params=_cparams(("arbitrary",)),
        name="ada",
    )(c, w_ada, b_ada.reshape(1, n))


def _proj_kernel(x_ref, g_ref, shift_ref, scale_ref, w_ref,
                 qsb_ref, sbrows_ref, kvsb_ref, qn_ref, nsarows_ref, nsab_ref,
                 winrows_ref, winb_ref, gates_ref):
    h = (_rms(x_ref[...]) * g_ref[...] * (1.0 + scale_ref[...]) + shift_ref[...]).astype(BF16)
    ch = 512

    def mm(c0, n):
        return _dot(h, w_ref[:, c0:c0 + n])

    for c in range(0, SB_W, ch):
        qsb_ref[:, c:c + ch] = mm(COL_QSB + c, ch).astype(BF16)
    for c in range(0, 2 * SB_W, ch):
        r = mm(COL_KVSB + c, ch)
        sbrows_ref[:, c:c + ch] = r
        kvsb_ref[:, c:c + ch] = r.astype(BF16)
    for c in range(0, NSA_W, ch):
        qn_ref[:, c:c + ch] = mm(COL_QN + c, ch).astype(BF16)
    for c in range(0, 4 * KV_W, ch):
        r = mm(COL_NSA + c, ch)
        nsarows_ref[:, c:c + ch] = r
        nsab_ref[:, c:c + ch] = r.astype(BF16)
    r = mm(COL_WIN, 2 * KV_W)
    winrows_ref[...] = r
    winb_ref[...] = r.astype(BF16)
    gates_ref[...] = _sigmoid(mm(COL_GATE, LANES))


def _project(x, g, shift, scale, w_cat, tm=256):
    m, d = x.shape
    row = lambda i: (i, 0)
    const = lambda i: (0, 0)
    mod_spec = lambda a: pl.BlockSpec((tm, d), row) if a.shape[0] == m else pl.BlockSpec((1, d), const)
    widths = [(SB_W, BF16), (2 * SB_W, F32), (2 * SB_W, BF16), (NSA_W, BF16), (4 * KV_W, F32),
              (4 * KV_W, BF16), (2 * KV_W, F32), (2 * KV_W, BF16), (LANES, F32)]
    return pl.pallas_call(
        _proj_kernel,
        grid=(m // tm,),
        in_specs=[pl.BlockSpec((tm, d), row),
                  pl.BlockSpec((1, d), const),
                  mod_spec(shift), mod_spec(scale),
                  pl.BlockSpec(w_cat.shape, const, pipeline_mode=pl.Buffered(1))],
        out_specs=[pl.BlockSpec((tm, w), row) for w, _ in widths],
        out_shape=[jax.ShapeDtypeStruct((m, w), dt) for w, dt in widths],
        compiler_params=_cparams(("arbitrary",)),
        name="project",
    )(x, g, shift, scale, w_cat)


def _sb_logits(z):
    l1m = jnp.minimum(-z, 0.0) - jnp.log(1.0 + jnp.exp(-jnp.abs(z)))
    return l1m + z, l1m


def _sb_prompt_kernel(q_ref, k_ref, v_ref, o_ref, *, tq, heads):
    i = pl.program_id(1)
    row = lax.broadcasted_iota(jnp.int32, (2 * tq, tq), 0)
    col = lax.broadcasted_iota(jnp.int32, (2 * tq, tq), 1)
    later2 = jnp.where(jnp.where(row >= tq, row - tq, row) > col, 1.0, 0.0).astype(BF16)
    causal = lax.broadcasted_iota(jnp.int32, (tq, tq), 1) < lax.broadcasted_iota(jnp.int32, (tq, tq), 0)

    hsl = [slice(hh * HEAD_DIM, (hh + 1) * HEAD_DIM) for hh in range(heads)]

    def block(j, state, diagonal):
        off = pl.multiple_of(j * tq, tq)
        zs = [_dot_nt(q_ref[:, hs], k_ref[pl.ds(off, tq), hs]) * ATT_SCALE for hs in hsl]
        logits = [_sb_logits(z) for z in zs]
        if diagonal:
            logits = [(ls, jnp.where(causal, l1m, 0.0)) for ls, l1m in logits]
        split = [jnp.concatenate(_split_bf16(l1m), axis=1) for _, l1m in logits]
        suffix = [_dot(sp, later2) + state[hh][0] for hh, sp in enumerate(split)]
        a = [jnp.exp(logits[hh][0] + suffix[hh]) for hh in range(heads)]
        if diagonal:
            a = [jnp.where(causal, x, 0.0) for x in a]
        acc = [state[hh][1] + _dot(a[hh].astype(BF16), v_ref[pl.ds(off, tq), hsl[hh]]) for hh in range(heads)]
        return tuple((suffix[hh][:, :1] + logits[hh][1][:, :1], acc[hh]) for hh in range(heads))

    init = tuple((jnp.zeros((tq, 1), F32), jnp.zeros((tq, HEAD_DIM), F32)) for _ in range(heads))
    state = block(i, init, True)
    state = lax.fori_loop(0, i, lambda t, st: block(i - 1 - t, st, False), state)
    for hh in range(heads):
        o_ref[:, hh * HEAD_DIM:(hh + 1) * HEAD_DIM] = state[hh][1]


def _sb_prompt(qsb, kvsb, tq=256, heads=4):
    t = qsb.shape[0]
    w = heads * HEAD_DIM
    n_hp = N_SB_HEADS // heads
    return pl.pallas_call(
        functools.partial(_sb_prompt_kernel, tq=tq, heads=heads),
        grid=(n_hp, t // tq),
        in_specs=[pl.BlockSpec((tq, w), lambda h, i: (i, h)),
                  pl.BlockSpec((t, w), lambda h, i: (0, h)),
                  pl.BlockSpec((t, w), lambda h, i: (0, n_hp + h))],
        out_specs=pl.BlockSpec((tq, w), lambda h, i: (i, h)),
        out_shape=jax.ShapeDtypeStruct((t, SB_W), F32),
        compiler_params=_cparams(("arbitrary", "arbitrary")),
        name="sb_prompt",
    )(qsb, kvsb, kvsb)


def _sb_sample_kernel(pt_ref, q_ref, new_ref, *refs, n_new, ppg):
    cache_refs = refs[:ppg]
    o_ref, qbd_ref, carry_ref, acc_ref = refs[ppg:]
    p = pl.program_id(1)
    rows = N_SB_HEADS * n_new
    kk = lax.broadcasted_iota(jnp.int32, (2 * LANES, 2 * LANES), 0)
    cc = lax.broadcasted_iota(jnp.int32, (2 * LANES, 2 * LANES), 1)
    kk = jnp.where(kk >= LANES, kk - LANES, kk)
    later_and_all = jnp.where((cc >= LANES) | (kk > cc), 1.0, 0.0).astype(BF16)
    stride = 2 * N_SB_HEADS
    hsl = [slice(h * HEAD_DIM, (h + 1) * HEAD_DIM) for h in range(N_SB_HEADS)]
    rsl = [slice(h * n_new, (h + 1) * n_new) for h in range(N_SB_HEADS)]

    def scores(kmat, diagonal):
        z = _dot_nt(qbd_ref[...].astype(BF16), kmat) * ATT_SCALE
        ls, l1m = _sb_logits(z)
        causal = None
        if diagonal:
            key = lax.broadcasted_iota(jnp.int32, (rows, LANES), 1)
            qry = lax.broadcasted_iota(jnp.int32, (rows, LANES), 0) % n_new
            causal = key < qry
            l1m = jnp.where(causal, l1m, 0.0)
        hi, lo = _split_bf16(l1m)
        return ls, _dot(jnp.concatenate([hi, lo], axis=1), later_and_all), causal

    def accumulate(ls, st, causal, vmat, carry):
        a = jnp.exp(ls + st[:, :LANES] + carry)
        if causal is not None:
            a = jnp.where(causal, a, 0.0)
        o_all = _dot(a.astype(BF16), vmat)
        for h in range(N_SB_HEADS):
            acc_ref[rsl[h], :] += o_all[rsl[h], hsl[h]]
        return carry + st[:, LANES:]

    @pl.when(p == 0)
    def _():
        acc_ref[...] = jnp.zeros_like(acc_ref)
        qbd_ref[...] = jnp.zeros_like(qbd_ref)
        for h in range(N_SB_HEADS):
            qbd_ref[rsl[h], hsl[h]] = q_ref[0, :, hsl[h]].astype(F32)
        pad = jnp.zeros((LANES - n_new, SB_W), F32)
        knew = jnp.concatenate([new_ref[0, :, :SB_W], pad], axis=0).astype(BF16)
        vnew = jnp.concatenate([new_ref[0, :, SB_W:], pad], axis=0).astype(BF16)
        ls, st, causal = scores(knew, True)
        carry_ref[...] = accumulate(ls, st, causal, vnew, jnp.zeros((rows, LANES), F32))

    def page_mat(ref, c0):
        return jnp.concatenate([ref[0, pl.ds(c0 + h, LANES, stride=stride), :].astype(BF16)
                                for h in range(N_SB_HEADS)], axis=1)

    qbd = qbd_ref[...].astype(BF16)
    zs = [_dot_nt(qbd, page_mat(ref, 0)) * ATT_SCALE for ref in cache_refs]
    logits = [_sb_logits(z) for z in zs]
    sts = [_dot(jnp.concatenate(_split_bf16(l1m), axis=1), later_and_all) for _, l1m in logits]
    carry = carry_ref[...]
    probs = []
    for (ls, _), st in zip(logits, sts):
        probs.append(jnp.exp(ls + st[:, :LANES] + carry).astype(BF16))
        carry = carry + st[:, LANES:]
    carry_ref[...] = carry
    outs = [_dot(a, page_mat(ref, N_SB_HEADS)) for a, ref in zip(probs, cache_refs)]
    for h in range(N_SB_HEADS):
        acc = acc_ref[rsl[h], :]
        for o_all in outs:
            acc = acc + o_all[rsl[h], hsl[h]]
        acc_ref[rsl[h], :] = acc

    @pl.when(p == pl.num_programs(1) - 1)
    def _():
        for h in range(N_SB_HEADS):
            o_ref[0, :, hsl[h]] = acc_ref[rsl[h], :]


def _sb_sample(page_table, qsb, sbrows, cache, ppg=8):
    bs, n_new, _ = qsb.shape
    n_pages = page_table.shape[1]
    page_rows = cache.shape[1]
    assert page_rows == LANES * 2 * N_SB_HEADS and n_pages % ppg == 0
    rows = N_SB_HEADS * n_new

    def page_map(q):
        return lambda b, p, pt: (pt[b * n_pages + n_pages - 1 - (p * ppg + q)], 0, 0)

    grid_spec = pltpu.PrefetchScalarGridSpec(
        num_scalar_prefetch=1,
        grid=(bs, n_pages // ppg),
        in_specs=[pl.BlockSpec((1, n_new, SB_W), lambda b, p, pt: (b, 0, 0)),
                  pl.BlockSpec((1, n_new, 2 * SB_W), lambda b, p, pt: (b, 0, 0))]
                 + [pl.BlockSpec((1, page_rows, HEAD_DIM), page_map(q)) for q in range(ppg)],
        out_specs=pl.BlockSpec((1, n_new, SB_W), lambda b, p, pt: (b, 0, 0)),
        scratch_shapes=[pltpu.VMEM((rows, SB_W), F32), pltpu.VMEM((rows, LANES), F32),
                        pltpu.VMEM((rows, HEAD_DIM), F32)],
    )
    return pl.pallas_call(
        functools.partial(_sb_sample_kernel, n_new=n_new, ppg=ppg),
        grid_spec=grid_spec,
        out_shape=jax.ShapeDtypeStruct((bs, n_new, SB_W), F32),
        compiler_params=_cparams(("arbitrary", "arbitrary")),
        name="sb_sample",
    )(page_table.reshape(-1), qsb, sbrows, *([cache] * ppg))


def _pool_chunks(x, w1, w2):
    x3 = x.reshape(x.shape[0] // CMP_STRIDE, CMP_STRIDE, x.shape[1])
    return jnp.sum(x3 * w1[None], axis=1), jnp.sum(x3 * w2[None], axis=1)


def _compress_kernel(rows_ref, wpos_ref, wc_ref, o_ref, first_ref, second_ref, *, step_rows):
    n_rows = rows_ref.shape[0]
    n_chunks = n_rows // CMP_STRIDE
    w1 = wpos_ref[0, :CMP_STRIDE, :]
    w2 = wpos_ref[0, CMP_STRIDE:, :]
    cps = step_rows // CMP_STRIDE

    def body(s, carry):
        r0 = pl.multiple_of(s * step_rows, step_rows)
        c0 = pl.multiple_of(s * cps, cps)
        f, sec = _pool_chunks(rows_ref[pl.ds(r0, step_rows), :], w1, w2)
        first_ref[pl.ds(c0, cps), :] = f
        second_ref[pl.ds(c0, cps), :] = sec
        return carry

    lax.fori_loop(0, n_rows // step_rows, body, 0)
    pre = first_ref[...] + pltpu.roll(second_ref[...], n_chunks - 1, axis=0)
    for g in range(N_NSA_KV):
        sl = slice(g * HEAD_DIM, (g + 1) * HEAD_DIM)
        o_ref[0, :, sl] = _dot(pre[:, sl].astype(BF16), wc_ref[0, g]).astype(BF16)


def _compress_prompt(nsarows, wpos, wc, step_rows=1024):
    t = nsarows.shape[0]
    n_chunks = t // CMP_STRIDE
    return pl.pallas_call(
        functools.partial(_compress_kernel, step_rows=step_rows),
        grid=(2,),
        in_specs=[pl.BlockSpec((t, KV_W), lambda c: (0, c)),
                  pl.BlockSpec((1, CMP_BLOCK, KV_W), lambda c: (c, 0, 0)),
                  pl.BlockSpec((1, N_NSA_KV, HEAD_DIM, HEAD_DIM), lambda c: (c, 0, 0, 0))],
        out_specs=pl.BlockSpec((1, n_chunks, KV_W), lambda c: (c, 0, 0)),
        out_shape=jax.ShapeDtypeStruct((2, n_chunks, KV_W), BF16),
        scratch_shapes=[pltpu.VMEM((n_chunks, KV_W), F32), pltpu.VMEM((n_chunks, KV_W), F32)],
        compiler_params=_cparams(("arbitrary",)),
        name="compress_prompt",
    )(nsarows, wpos, wc)


def _importance_matrix(n_cmp, n_slc_lanes):
    m = np.arange(n_cmp)[:, None]
    j = np.arange(n_slc_lanes)[None, :]
    d = m - (SLC_BLOCK // CMP_STRIDE) * j
    w = np.where((d >= 0) & (d <= 2), 1.0, np.where((d == -1) | (d == 3), 0.5, 0.0))
    return jnp.asarray(w, BF16)


def _block_expansion(n_keys):
    b = np.arange(LANES)[:, None]
    k = np.arange(n_keys)[None, :]
    return jnp.asarray((b == k // SLC_BLOCK).astype(np.float32), BF16)


def _top_blocks_t(score, n_top):
    blk = lax.broadcasted_iota(jnp.int32, score.shape, 0).astype(F32)
    sel = jnp.zeros(score.shape, F32)
    for _ in range(n_top):
        m = jnp.max(score, axis=0, keepdims=True)
        idx = jnp.min(jnp.where(score == m, blk, float(score.shape[0])), axis=0, keepdims=True)
        pick = blk == idx
        sel = jnp.where(pick, 1.0, sel)
        score = jnp.where(pick, -2.0, score)
    return sel


def _nsa_prompt_t_kernel(q_ref, gate_ref, kc_ref, vct_ref, sk_ref, svt_ref, wk_ref, wvt_ref,
                         et_ref, mimpt_ref, o_ref, cb_ref, *, tq, tk):
    i = pl.program_id(0)
    start = i * tq
    cols = NSA_GROUP * tq
    groups = range(N_NSA_KV)
    gsl = [slice(g * HEAD_DIM, (g + 1) * HEAD_DIM) for g in groups]
    hsl = [slice(h * tq, (h + 1) * tq) for h in range(NSA_GROUP)]
    slope_of = lambda g, h: 2.0 ** -(g * NSA_GROUP + h + 1)
    drop = 2.0 * NEG
    n_cmp = kc_ref.shape[1]

    @pl.when(i == 0)
    def _():
        kio = lax.broadcasted_iota(jnp.int32, (tk, tq), 0).astype(F32)
        for hh in range(N_NSA_HEADS):
            cb_ref[hh] = (2.0 ** -(hh + 1)) * kio

    qpos = start + lax.broadcasted_iota(jnp.int32, (1, tq), 1)
    qpos_f = qpos.astype(F32)

    def head_row(fn):
        return jnp.concatenate([fn(h) for h in range(NSA_GROUP)], axis=1)

    qg, o_cmp, selb, first_tile = [], [], [], []
    dist_c = qpos - (CMP_STRIDE * lax.broadcasted_iota(jnp.int32, (n_cmp, tq), 0) + (CMP_BLOCK - 1))
    seen_c = dist_c >= 0
    dist_cf = dist_c.astype(F32)
    for g in groups:
        qg.append(jnp.concatenate(
            [q_ref[:, (g * NSA_GROUP + h) * HEAD_DIM:(g * NSA_GROUP + h + 1) * HEAD_DIM] for h in range(NSA_GROUP)],
            axis=0))
        zc = _dot_nt(kc_ref[0, :, gsl[g]], qg[g]) * ATT_SCALE
        ps = []
        for h in range(NSA_GROUP):
            s = jnp.where(seen_c, zc[:, hsl[h]] - slope_of(g, h) * dist_cf, NEG)
            e = jnp.where(seen_c, jnp.exp(s - jnp.max(s, axis=0, keepdims=True)), 0.0)
            ps.append(e / jnp.maximum(jnp.sum(e, axis=0, keepdims=True), 1e-30))
        o_cmp.append(_dot(vct_ref[gsl[g], :], jnp.concatenate(ps, axis=1).astype(BF16)))
        hi, lo = _split_bf16(ps[0] + ps[1] + ps[2] + ps[3])
        imp = _dot(mimpt_ref[...], hi) + _dot(mimpt_ref[...], lo)
        blk = lax.broadcasted_iota(jnp.int32, imp.shape, 0)
        qb = qpos // SLC_BLOCK
        forced = (blk == 0) | ((blk <= qb) & (blk >= qb - 1))
        score = jnp.where(blk > qb, -1.0, jnp.where(forced, FORCE, imp))
        sel = _top_blocks_t(score, SLC_TOP)
        selb.append(((sel - 1.0) * (-drop)).astype(BF16))
        far = jnp.min(jnp.where((sel > 0.5) & (blk > 0), blk.astype(F32), float(LANES)))
        first_tile.append(far.astype(jnp.int32) // (tk // SLC_BLOCK))

    def attend(k_ref, vt_ref, t_lo, t_hi, selected):
        def tile(t, state, edge, active=tuple(groups)):
            off = pl.multiple_of(t * tk, tk)
            off_f = off.astype(F32)
            zs = {g: _dot_nt(k_ref[pl.ds(off, tk), gsl[g]], qg[g]) for g in active}
            mbs = {g: _dot(et_ref[t], selb[g]) for g in active} if selected else None
            if edge:
                dist = qpos - (off + lax.broadcasted_iota(jnp.int32, (tk, tq), 0))
                ok = (dist >= 0) if selected else ((dist >= 0) & (dist < WINDOW))
            out = []
            for g in groups:
                if g not in active:
                    out.append(state[g])
                    continue
                parts = []
                for h in range(NSA_GROUP):
                    sh = zs[g][:, hsl[h]] * ATT_SCALE + cb_ref[g * NSA_GROUP + h]
                    if selected:
                        sh = sh + mbs[g]
                    if edge:
                        sh = jnp.where(ok, sh, drop)
                    parts.append(sh)
                s = jnp.concatenate(parts, axis=1)
                shift = head_row(lambda h: slope_of(g, h) * (qpos_f - off_f))
                m_old, l_old, acc_old = state[g]
                m_new = jnp.maximum(m_old, jnp.max(s, axis=0, keepdims=True) - shift)
                e = jnp.exp(s - (shift + m_new))
                alpha = jnp.exp(m_old - m_new)
                pv = _dot(vt_ref[t, gsl[g], :], e.astype(BF16))
                out.append((m_new, l_old * alpha + jnp.sum(e, axis=0, keepdims=True), acc_old * alpha + pv))
            return tuple(out)

        init = tuple((jnp.full((1, cols), NEG, F32), jnp.zeros((1, cols), F32), jnp.zeros((HEAD_DIM, cols), F32))
                     for _ in groups)
        if selected:
            edge_t = t_hi - 1
            lo = [jnp.clip(first_tile[g], 1, edge_t) for g in groups]
            both = jnp.maximum(lo[0], lo[1])
            state = lax.fori_loop(0, jnp.minimum(edge_t, 1), lambda t, st: tile(t, st, False), init)
            for g in groups:
                state = lax.fori_loop(lo[g], both, lambda t, st, g=g: tile(t, st, False, (g,)), state)
            state = lax.fori_loop(both, edge_t, lambda t, st: tile(t, st, False), state)
            state = tile(edge_t, state, True)
        else:
            state = lax.fori_loop(t_lo, t_hi, lambda t, st: tile(t, st, True), init)
        return [state[g][2] / jnp.maximum(state[g][1], 1e-30) for g in groups]

    last = (start + tq - 1) // tk
    o_slc = attend(sk_ref, svt_ref, 0, last + 1, True)
    o_win = attend(wk_ref, wvt_ref, jnp.maximum(start - WINDOW, 0) // tk, last + 1, False)

    gates_t = gate_ref[...].T
    for g in groups:
        def gate_row(branch):
            return head_row(lambda h: gates_t[branch * N_NSA_HEADS + g * NSA_GROUP + h:
                                              branch * N_NSA_HEADS + g * NSA_GROUP + h + 1, :])
        o = gate_row(0) * o_cmp[g] + gate_row(1) * o_slc[g] + gate_row(2) * o_win[g]
        for h in range(NSA_GROUP):
            c = (g * NSA_GROUP + h) * HEAD_DIM
            o_ref[:, c:c + HEAD_DIM] = o[:, hsl[h]].T


def _nsa_prompt_t(qn, gates, kvc, nsab, winb, tq=128, tk=256):
    t = qn.shape[0]
    n_cmp = kvc.shape[1]
    n_tiles = t // tk
    tiles_t = lambda v: v.T.reshape(KV_W, n_tiles, tk).transpose(1, 0, 2)
    svt = tiles_t(nsab[:, 3 * KV_W:])
    wvt = tiles_t(winb[:, KV_W:])
    vct = kvc[1].T
    et = jnp.transpose(_block_expansion(t)).reshape(n_tiles, tk, LANES)
    mimpt = jnp.transpose(_importance_matrix(n_cmp, LANES))
    once = pl.Buffered(1)
    full = lambda a: pl.BlockSpec(a.shape, lambda i: (0,) * a.ndim, pipeline_mode=once)
    col = lambda c: pl.BlockSpec((t, KV_W), lambda i: (0, c), pipeline_mode=once)
    return pl.pallas_call(
        functools.partial(_nsa_prompt_t_kernel, tq=tq, tk=tk),
        grid=(t // tq,),
        in_specs=[pl.BlockSpec((tq, NSA_W), lambda i: (i, 0)),
                  pl.BlockSpec((tq, LANES), lambda i: (i, 0)),
                  pl.BlockSpec((1, n_cmp, KV_W), lambda i: (0, 0, 0), pipeline_mode=once),
                  full(vct), col(2), full(svt), col(0), full(wvt), full(et), full(mimpt)],
        out_specs=pl.BlockSpec((tq, NSA_W), lambda i: (i, 0)),
        out_shape=jax.ShapeDtypeStruct((t, NSA_W), F32),
        scratch_shapes=[pltpu.VMEM((N_NSA_HEADS, tk, tq), F32)],
        compiler_params=_cparams(("arbitrary",)),
        name="nsa_prompt",
    )(qn, gates, kvc, vct, nsab, svt, winb, wvt, et, mimpt)


def _nsa_sample_kernel(pt_ref, q_ref, gate_ref, newrows_ref, newwin_ref, win_ref, wpos_ref, wc_ref,
                       e_ref, mimp_ref, *refs, n_new, n_pages, ppg):
    cache_refs = refs[:ppg]
    o_ref, winout_ref, first_ref, second_ref, ks_ref, vs_ref = refs[ppg:]
    p = pl.program_id(1)
    page = LANES
    past = n_pages * page
    cpp = page // CMP_STRIDE
    n_cmp = first_ref.shape[1]
    n_keys = ks_ref.shape[0]
    w_buf = win_ref.shape[1] // (2 * N_NSA_KV)
    gsl = [slice(g * HEAD_DIM, (g + 1) * HEAD_DIM) for g in range(N_NSA_KV)]
    kinds = 4 * N_NSA_KV

    @pl.when(p == 0)
    def _():
        first_ref[...] = jnp.zeros_like(first_ref)
        second_ref[...] = jnp.zeros_like(second_ref)

    for q, cache_ref in enumerate(cache_refs):
        pg = p * ppg + q
        c0 = pl.multiple_of(pg * cpp, cpp)
        r0 = pl.multiple_of(pg * page, page)
        for g in range(N_NSA_KV):
            for kv in range(2):
                x = cache_ref[0, pl.ds(kv * N_NSA_KV + g, page, stride=kinds), :]
                f, sec = _pool_chunks(x, wpos_ref[kv, :CMP_STRIDE, gsl[g]], wpos_ref[kv, CMP_STRIDE:, gsl[g]])
                first_ref[kv, pl.ds(c0, cpp), gsl[g]] = f
                second_ref[kv, pl.ds(c0, cpp), gsl[g]] = sec
            ks_ref[pl.ds(r0, page), gsl[g]] = cache_ref[0, pl.ds(2 * N_NSA_KV + g, page, stride=kinds), :].astype(BF16)
            vs_ref[pl.ds(r0, page), gsl[g]] = cache_ref[0, pl.ds(3 * N_NSA_KV + g, page, stride=kinds), :].astype(BF16)

    @pl.when(p == pl.num_programs(1) - 1)
    def _():
        new = newrows_ref[0]
        zpad = jnp.zeros((CMP_STRIDE - n_new, KV_W), F32)
        kvc = []
        for kv in range(2):
            xc = jnp.concatenate([new[:, kv * KV_W:(kv + 1) * KV_W], zpad], axis=0)
            first_ref[kv, past // CMP_STRIDE:past // CMP_STRIDE + 1, :] = jnp.sum(
                xc * wpos_ref[kv, :CMP_STRIDE, :], axis=0, keepdims=True)
            second_ref[kv, past // CMP_STRIDE:past // CMP_STRIDE + 1, :] = jnp.sum(
                xc * wpos_ref[kv, CMP_STRIDE:, :], axis=0, keepdims=True)
            pre = first_ref[kv] + pltpu.roll(second_ref[kv], n_cmp - 1, axis=0)
            kvc.append([_dot(pre[:, gsl[g]].astype(BF16), wc_ref[kv, g]).astype(BF16) for g in range(N_NSA_KV)])
        tail = jnp.zeros((n_keys - past - n_new, KV_W), F32)
        ks_ref[past:, :] = jnp.concatenate([new[:, 2 * KV_W:3 * KV_W], tail], axis=0).astype(BF16)
        vs_ref[past:, :] = jnp.concatenate([new[:, 3 * KV_W:4 * KV_W], tail], axis=0).astype(BF16)

        nwin = newwin_ref[0]
        wr = 2 * N_NSA_KV
        winout_ref[0, :(w_buf - n_new) * wr, :] = win_ref[0, n_new * wr:, :]
        for c in range(wr):
            winout_ref[0, pl.ds((w_buf - n_new) * wr + c, n_new, stride=wr), :] = nwin[:, c * HEAD_DIM:(c + 1) * HEAD_DIM]
        wtail = jnp.zeros((LANES - n_new, HEAD_DIM), F32)

        def window(c):
            return jnp.concatenate([win_ref[0, pl.ds(c, w_buf, stride=wr), :],
                                    nwin[:, c * HEAD_DIM:(c + 1) * HEAD_DIM], wtail], axis=0).astype(BF16)

        n_win = w_buf + LANES
        rows = NSA_GROUP * n_new
        r = lax.broadcasted_iota(jnp.int32, (rows, 1), 0)
        qpos = past + r % n_new
        qpos_q = past + lax.broadcasted_iota(jnp.int32, (n_new, 1), 0)
        kc_end = CMP_STRIDE * lax.broadcasted_iota(jnp.int32, (1, n_cmp), 1) + (CMP_BLOCK - 1)
        dist_s = qpos - lax.broadcasted_iota(jnp.int32, (1, n_keys), 1)
        dist_w = qpos - (past - w_buf + lax.broadcasted_iota(jnp.int32, (1, n_win), 1))
        gates = gate_ref[0]

        groups = range(N_NSA_KV)
        qgs = [jnp.concatenate(
            [q_ref[0, :, (g * NSA_GROUP + h) * HEAD_DIM:(g * NSA_GROUP + h + 1) * HEAD_DIM]
             for h in range(NSA_GROUP)], axis=0) for g in groups]
        slopes = [_head_slope(g * NSA_GROUP + r // n_new) for g in groups]
        z_cmp = [_dot_nt(qgs[g], kvc[0][g]) for g in groups]
        z_slc = [_dot_nt(qgs[g], ks_ref[:, gsl[g]]) for g in groups]
        z_win = [_dot_nt(qgs[g], window(g)) for g in groups]
        pcs = [_masked_softmax(z_cmp[g] * ATT_SCALE - slopes[g] * (qpos - kc_end).astype(F32), kc_end <= qpos)
               for g in groups]
        o_cmps = [_dot(pcs[g].astype(BF16), kvc[1][g]) for g in groups]
        imps = []
        for pc in pcs:
            p4 = pc[0:n_new] + pc[n_new:2 * n_new] + pc[2 * n_new:3 * n_new] + pc[3 * n_new:4 * n_new]
            hi, lo = _split_bf16(p4)
            imps.append(_dot(hi, mimp_ref[...]) + _dot(lo, mimp_ref[...]))
        n_slc = -(-(past + n_new) // SLC_BLOCK)
        sel_all = _top_blocks_by_rank(
            _slc_scores(jnp.concatenate(imps, axis=0), jnp.concatenate([qpos_q] * N_NSA_KV, axis=0)), SLC_TOP, n_slc)
        keeps = [_dot(jnp.concatenate([sel_all[g * n_new:(g + 1) * n_new]] * NSA_GROUP, axis=0).astype(BF16),
                      e_ref[...]) > 0.5 for g in groups]
        pss = [_masked_softmax(z_slc[g] * ATT_SCALE - slopes[g] * dist_s.astype(F32), (dist_s >= 0) & keeps[g])
               for g in groups]
        pws = [_masked_softmax(z_win[g] * ATT_SCALE - slopes[g] * dist_w.astype(F32),
                               (dist_w >= 0) & (dist_w < WINDOW)) for g in groups]
        o_slcs = [_dot(pss[g].astype(BF16), vs_ref[:, gsl[g]]) for g in groups]
        o_wins = [_dot(pws[g].astype(BF16), window(N_NSA_KV + g)) for g in groups]

        for g in groups:
            o = (_gate_cols(gates, 0, g, n_new) * o_cmps[g] + _gate_cols(gates, 1, g, n_new) * o_slcs[g]
                 + _gate_cols(gates, 2, g, n_new) * o_wins[g])
            for h in range(NSA_GROUP):
                c = (g * NSA_GROUP + h) * HEAD_DIM
                o_ref[0, :, c:c + HEAD_DIM] = o[h * n_new:(h + 1) * n_new]


def _nsa_sample(page_table, qn, gates, nsarows, winrows, cache, state_win, wpos, wc, ppg=8):
    bs, n_new, _ = qn.shape
    n_pages = page_table.shape[1]
    page_rows = cache.shape[1]
    assert page_rows == LANES * 4 * N_NSA_KV and n_pages % ppg == 0
    past = n_pages * LANES
    win_rows = state_win.shape[1]
    n_keys = past + LANES
    n_cmp = 2 * (past // CMP_STRIDE)
    e = _block_expansion(n_keys)
    mimp = _importance_matrix(n_cmp, LANES)
    seq = lambda b, p, pt: (b, 0, 0)
    c3 = lambda b, p, pt: (0, 0, 0)

    def page_map(q):
        return lambda b, p, pt: (pt[b * n_pages + p * ppg + q], 0, 0)

    grid_spec = pltpu.PrefetchScalarGridSpec(
        num_scalar_prefetch=1,
        grid=(bs, n_pages // ppg),
        in_specs=[pl.BlockSpec((1, n_new, NSA_W), seq),
                  pl.BlockSpec((1, n_new, LANES), seq),
                  pl.BlockSpec((1, n_new, 4 * KV_W), seq),
                  pl.BlockSpec((1, n_new, 2 * KV_W), seq),
                  pl.BlockSpec((1, win_rows, HEAD_DIM), seq),
                  pl.BlockSpec(wpos.shape, c3),
                  pl.BlockSpec(wc.shape, lambda b, p, pt: (0, 0, 0, 0)),
                  pl.BlockSpec(e.shape, lambda b, p, pt: (0, 0)),
                  pl.BlockSpec(mimp.shape, lambda b, p, pt: (0, 0))]
                 + [pl.BlockSpec((1, page_rows, HEAD_DIM), page_map(q)) for q in range(ppg)],
        out_specs=[pl.BlockSpec((1, n_new, NSA_W), seq),
                   pl.BlockSpec((1, win_rows, HEAD_DIM), seq)],
        scratch_shapes=[pltpu.VMEM((2, n_cmp, KV_W), F32), pltpu.VMEM((2, n_cmp, KV_W), F32),
                        pltpu.VMEM((n_keys, KV_W), BF16), pltpu.VMEM((n_keys, KV_W), BF16)],
    )
    return pl.pallas_call(
        functools.partial(_nsa_sample_kernel, n_new=n_new, n_pages=n_pages, ppg=ppg),
        grid_spec=grid_spec,
        out_shape=[jax.ShapeDtypeStruct((bs, n_new, NSA_W), F32),
                   jax.ShapeDtypeStruct((bs, win_rows, HEAD_DIM), F32)],
        compiler_params=_cparams(("arbitrary", "arbitrary")),
        name="nsa_sample",
    )(page_table.reshape(-1), qn, gates, nsarows, winrows, state_win, wpos, wc, e, mimp, *([cache] * ppg))


def _mixout_kernel(osb_ref, on_ref, x_ref, gout_ref, w_ref, gate_ref, g2_ref, shift_ref, scale_ref, *rest):
    x1_ref, h2_ref, h2g_ref, nrm_ref = rest[-4:]
    tm = x_ref.shape[0]
    for h in range(N_SB_HEADS + N_NSA_HEADS):
        src = osb_ref if h < N_SB_HEADS else on_ref
        c = (h % N_SB_HEADS) * HEAD_DIM
        sl = slice(h * HEAD_DIM, (h + 1) * HEAD_DIM)
        nrm_ref[:, sl] = (_rms(src[:, c:c + HEAD_DIM]) * gout_ref[:, sl]).astype(BF16)
    x1 = x_ref[...] + gate_ref[...] * _dot(nrm_ref[...], w_ref[...])
    x1_ref[...] = x1
    h2 = _rms(x1) * g2_ref[...] * (1.0 + scale_ref[...]) + shift_ref[...]
    h2_ref[...] = h2.astype(BF16)
    for s in range(ROW_WORDS):
        lo = h2[:, (2 * s) * LANES:(2 * s + 1) * LANES].astype(BF16).astype(F32)
        hi = h2[:, (2 * s + 1) * LANES:(2 * s + 2) * LANES].astype(BF16).astype(F32)
        word = (lax.bitcast_convert_type(lo, U32) >> 16) | lax.bitcast_convert_type(hi, U32)
        h2g_ref[pl.ds(s, tm, stride=ROW_WORDS), :] = word


def _mixout(osb, on, x, gout, w_out, gate, g2, shift, scale, n_total, row0, prev=None, tm=256):
    m, d = x.shape
    blk0 = row0 // tm
    row = lambda i: (i, 0)
    out_row = lambda i: (blk0 + i, 0)
    const = lambda i: (0, 0)
    mod_spec = lambda a: pl.BlockSpec((tm, d), row) if a.shape[0] == m else pl.BlockSpec((1, d), const)
    in_specs = [pl.BlockSpec((tm, SB_W), row), pl.BlockSpec((tm, NSA_W), row), pl.BlockSpec((tm, d), row),
                pl.BlockSpec((1, d), const),
                pl.BlockSpec(w_out.shape, const, pipeline_mode=pl.Buffered(1)),
                mod_spec(gate), pl.BlockSpec((1, d), const), mod_spec(shift), mod_spec(scale)]
    args = [osb, on, x, gout, w_out, gate, g2, shift, scale]
    aliases = {}
    if prev is not None:
        in_specs += [pl.BlockSpec(memory_space=pl.ANY)] * 3
        aliases = {len(args) + j: j for j in range(3)}
        args += list(prev)
    return pl.pallas_call(
        _mixout_kernel,
        grid=(m // tm,),
        in_specs=in_specs,
        out_specs=[pl.BlockSpec((tm, d), out_row), pl.BlockSpec((tm, d), out_row),
                   pl.BlockSpec((tm * ROW_WORDS, LANES), out_row)],
        out_shape=[jax.ShapeDtypeStruct((n_total, d), F32), jax.ShapeDtypeStruct((n_total, d), BF16),
                   jax.ShapeDtypeStruct((n_total * ROW_WORDS, LANES), U32)],
        scratch_shapes=[pltpu.VMEM((tm, d), BF16)],
        input_output_aliases=aliases,
        compiler_params=_cparams(("arbitrary",)),
        name="mixout",
    )(*args)


def _first_max(v, idx, axis, n):
    m = jnp.max(v, axis=axis, keepdims=True)
    return m, jnp.min(jnp.where(v == m, idx, float(n)), axis=axis, keepdims=True)


def _router_kernel(h_ref, wrt_ref, bias_ref, eidx_ref, w_ref, routed_ref):
    tm = h_ref.shape[0]
    per = N_EXPERTS // N_EXPERT_GROUPS
    s = _sigmoid(_dot_nt(wrt_ref[...], h_ref[...]))
    biased = s + bias_ref[...]
    b3 = biased.reshape(N_EXPERT_GROUPS, per, tm)
    r3 = lax.broadcasted_iota(jnp.int32, b3.shape, 1).astype(F32)
    m1, i1 = _first_max(b3, r3, 1, per)
    m2 = jnp.max(jnp.where(r3 == i1, -jnp.inf, b3), axis=1, keepdims=True)
    gscore = (m1 + m2).reshape(N_EXPERT_GROUPS, tm)
    gi = lax.broadcasted_iota(jnp.int32, gscore.shape, 0).astype(F32)
    gmask = jnp.zeros(gscore.shape, F32)
    for _ in range(TOPK_GROUPS):
        _, idx = _first_max(gscore, gi, 0, N_EXPERT_GROUPS)
        pick = gi == idx
        gmask = jnp.where(pick, 1.0, gmask)
        gscore = jnp.where(pick, -jnp.inf, gscore)
    cand = jnp.where(gmask.reshape(N_EXPERT_GROUPS, 1, tm) > 0.5, b3, -jnp.inf).reshape(N_EXPERTS, tm)
    ei = lax.broadcasted_iota(jnp.int32, cand.shape, 0).astype(F32)
    ids, ws = [], []
    routed = jnp.zeros(cand.shape, F32)
    for _ in range(TOP_K):
        _, idx = _first_max(cand, ei, 0, N_EXPERTS)
        pick = ei == idx
        ids.append(idx)
        ws.append(jnp.sum(jnp.where(pick, s, 0.0), axis=0, keepdims=True))
        cand = jnp.where(pick, -jnp.inf, cand)
        routed = jnp.where(pick, 1.0, routed)
    w = jnp.concatenate(ws, axis=0)
    eidx_ref[...] = jnp.concatenate(ids, axis=0).astype(jnp.int32)
    w_ref[...] = w / jnp.sum(w, axis=0, keepdims=True) * ROUTED_SCALE
    routed_ref[...] = routed.astype(BF16)


def _router(h2, w_router_t, router_bias, tm=512):
    m, d = h2.shape
    return pl.pallas_call(
        _router_kernel,
        grid=(m // tm,),
        in_specs=[pl.BlockSpec((tm, d), lambda i: (i, 0)),
                  pl.BlockSpec((N_EXPERTS, d), lambda i: (0, 0)),
                  pl.BlockSpec((N_EXPERTS, 1), lambda i: (0, 0))],
        out_specs=[pl.BlockSpec((TOP_K, tm), lambda i: (0, i)), pl.BlockSpec((TOP_K, tm), lambda i: (0, i)),
                   pl.BlockSpec((N_EXPERTS, tm), lambda i: (0, i))],
        out_shape=[jax.ShapeDtypeStruct((TOP_K, m), jnp.int32), jax.ShapeDtypeStruct((TOP_K, m), F32),
                   jax.ShapeDtypeStruct((N_EXPERTS, m), BF16)],
        compiler_params=_cparams(("arbitrary",)),
        name="router",
    )(h2, w_router_t, router_bias.reshape(N_EXPERTS, 1))


def _slots_kernel(eidx_ref, base_ref, o_ref):
    base = base_ref[...]
    ei = lax.broadcasted_iota(jnp.int32, base.shape, 0)
    rows = [jnp.sum(jnp.where(ei == eidx_ref[k:k + 1, :], base, 0.0), axis=0, keepdims=True) for k in range(TOP_K)]
    o_ref[...] = jnp.concatenate(rows, axis=0).astype(jnp.int32)


def _slots(eidx_t, base_t, tm=512):
    m = eidx_t.shape[1]
    return pl.pallas_call(
        _slots_kernel,
        grid=(m // tm,),
        in_specs=[pl.BlockSpec((TOP_K, tm), lambda i: (0, i)), pl.BlockSpec((N_EXPERTS, tm), lambda i: (0, i))],
        out_specs=pl.BlockSpec((TOP_K, tm), lambda i: (0, i)),
        out_shape=jax.ShapeDtypeStruct((TOP_K, m), jnp.int32),
        compiler_params=_cparams(("arbitrary",)),
        name="slots",
    )(eidx_t, base_t)


def _dispatch_plan(eidx_t, routed_t):
    t = eidx_t.shape[1]
    ck = 128
    r3 = routed_t.reshape(N_EXPERTS, t // ck, ck)
    upto = jnp.asarray(np.triu(np.ones((ck, ck), np.float32)), BF16)
    within = jnp.einsum('ecj,ji->eci', r3, upto, preferred_element_type=F32)
    tot = within[:, :, -1]
    incl = (within + (jnp.cumsum(tot, axis=1) - tot)[:, :, None]).reshape(N_EXPERTS, t)
    counts = incl[:, -1].astype(jnp.int32)
    padded = (counts + MOE_ROWS - 1) // MOE_ROWS * MOE_ROWS
    pad_end = jnp.cumsum(padded)
    base_t = (pad_end - padded).astype(F32)[:, None] + incl - routed_t.astype(F32)
    dest = _slots(eidx_t, base_t).T
    n_blk = -(-(t * TOP_K) // MOE_ROWS) + N_EXPERTS
    blk_start = (jnp.arange(n_blk) * MOE_ROWS)[:, None]
    blk_e = jnp.minimum(jnp.sum(pad_end[None, :] <= blk_start, axis=1), N_EXPERTS - 1).astype(jnp.int32)
    n_active = (pad_end[-1] // MOE_ROWS).astype(jnp.int32)
    nxt_blk = (pad_end[blk_e] // MOE_ROWS).astype(jnp.int32)
    nxt_e = jnp.where(nxt_blk < n_active, blk_e[jnp.minimum(nxt_blk, n_blk - 1)], -1).astype(jnp.int32)
    first = jnp.concatenate([jnp.ones((1,), jnp.int32), (blk_e[1:] != blk_e[:-1]).astype(jnp.int32)])
    par = ((jnp.cumsum(first) - 1) % 2).astype(jnp.int32)
    return dest, blk_e, nxt_e, par, n_active.reshape(1), n_blk


def _dispatch_kernel(dest_ref, h2g_ref, xs_ref, sem, *, tm):
    i = pl.program_id(0)

    def body(r, carry):
        src = h2g_ref.at[pl.ds(pl.multiple_of(r * ROW_WORDS, ROW_WORDS), ROW_WORDS), :]
        for k in range(TOP_K):
            d = dest_ref[(i * tm + r) * TOP_K + k]
            pltpu.make_async_copy(src, xs_ref.at[pl.ds(pl.multiple_of(d * ROW_WORDS, ROW_WORDS), ROW_WORDS), :],
                                  sem).start()
        return carry

    lax.fori_loop(0, tm, body, 0)
    done = xs_ref.at[pl.ds(0, tm * TOP_K * ROW_WORDS), :]
    pltpu.make_async_copy(done, done, sem).wait()


def _dispatch(dest_flat, h2g, p_rows, tm=256):
    n_tok = h2g.shape[0] // ROW_WORDS
    grid_spec = pltpu.PrefetchScalarGridSpec(
        num_scalar_prefetch=1,
        grid=(n_tok // tm,),
        in_specs=[pl.BlockSpec((tm * ROW_WORDS, LANES), lambda i, d: (i, 0))],
        out_specs=pl.BlockSpec(memory_space=pl.ANY),
        scratch_shapes=[pltpu.SemaphoreType.DMA(())],
    )
    return pl.pallas_call(
        functools.partial(_dispatch_kernel, tm=tm),
        grid_spec=grid_spec,
        out_shape=jax.ShapeDtypeStruct((p_rows * ROW_WORDS, LANES), U32),
        compiler_params=_cparams(("arbitrary",)),
        name="dispatch",
    )(dest_flat, h2g)


def _experts_kernel(be_ref, nxt_ref, par_ref, nact_ref, x_ref, wg_hbm, wu_hbm, wd_hbm, o_ref,
                    wgf, wuf, wdf, wgb, wub, wdb, sem):
    b = pl.program_id(0)
    active = b < nact_ref[0]
    e = be_ref[b]
    slot = par_ref[b]
    changed = e != be_ref[jnp.maximum(b - 1, 0)]

    def weight_copies(eid, s):
        return [pltpu.make_async_copy(src.at[eid], dst.at[s], sem.at[s, j])
                for j, (src, dst) in enumerate(((wg_hbm, wgf), (wu_hbm, wuf), (wd_hbm, wdf)))]

    @pl.when(active & (b == 0))
    def _():
        for cp in weight_copies(e, slot):
            cp.start()

    @pl.when(active & ((b == 0) | changed))
    def _():
        for cp in weight_copies(e, slot):
            cp.wait()
        nxt = nxt_ref[b]

        @pl.when(nxt >= 0)
        def _():
            for cp in weight_copies(nxt, 1 - slot):
                cp.start()

        ch = 256
        for c in range(0, D_MODEL, ch):
            wgb[c:c + ch, :] = wgf[slot, c:c + ch, :].astype(BF16)
            wub[c:c + ch, :] = wuf[slot, c:c + ch, :].astype(BF16)
        for c in range(0, D_EXPERT, ch // 4):
            wdb[c:c + ch // 4, :] = wdf[slot, c:c + ch // 4, :].astype(BF16)

    @pl.when(active)
    def _():
        chunks = []
        for s in range(ROW_WORDS):
            word = x_ref[pl.ds(s, MOE_ROWS, stride=ROW_WORDS), :]
            chunks.append(lax.bitcast_convert_type(word << 16, F32).astype(BF16))
            chunks.append(lax.bitcast_convert_type(word & jnp.uint32(0xFFFF0000), F32).astype(BF16))
        x = jnp.concatenate(chunks, axis=1)
        gt = _dot(x, wgb[...])
        a = (gt * _sigmoid(gt) * _dot(x, wub[...])).astype(BF16)
        y = _dot(a, wdb[...])
        for j in range(D_CHUNKS):
            o_ref[pl.ds(j, MOE_ROWS, stride=D_CHUNKS), :] = y[:, j * LANES:(j + 1) * LANES]


def _experts(blk_e, nxt_e, par, n_active, x_sorted, wg, wu, wd):
    d = wg.shape[1]
    n_blk = x_sorted.shape[0] // (MOE_ROWS * ROW_WORDS)
    hbm = pl.BlockSpec(memory_space=pl.ANY)
    grid_spec = pltpu.PrefetchScalarGridSpec(
        num_scalar_prefetch=4,
        grid=(n_blk,),
        in_specs=[pl.BlockSpec((MOE_ROWS * ROW_WORDS, LANES), lambda b, be, nx, pa, na: (jnp.minimum(b, na[0] - 1), 0)),
                  hbm, hbm, hbm],
        out_specs=pl.BlockSpec((MOE_ROWS * D_CHUNKS, LANES), lambda b, be, nx, pa, na: (jnp.minimum(b, na[0] - 1), 0)),
        scratch_shapes=[pltpu.VMEM((2, d, D_EXPERT), F32), pltpu.VMEM((2, d, D_EXPERT), F32),
                        pltpu.VMEM((2, D_EXPERT, d), F32),
                        pltpu.VMEM((d, D_EXPERT), BF16), pltpu.VMEM((d, D_EXPERT), BF16),
                        pltpu.VMEM((D_EXPERT, d), BF16),
                        pltpu.SemaphoreType.DMA((2, 3))],
    )
    return pl.pallas_call(
        _experts_kernel,
        grid_spec=grid_spec,
        out_shape=jax.ShapeDtypeStruct((n_blk * MOE_ROWS * D_CHUNKS, LANES), F32),
        compiler_params=_cparams(("arbitrary",)),
        name="experts",
    )(blk_e, nxt_e, par, n_active, x_sorted, wg, wu, wd)


def _final_kernel(dest_ref, x1_ref, h2_ref, w8_ref, ys_ref, wg_ref, wu_ref, wd_ref, gate_ref, gf_ref,
                  shift_ref, scale_ref, y_ref, ybuf, ymoe_ref, sem, *, tm, row0):
    i = pl.program_id(0)
    n = pl.num_programs(0)
    tile_rows = tm * TOP_K * D_CHUNKS

    def issue(tile, slot):
        def body(r, carry):
            for k in range(TOP_K):
                d = dest_ref[(row0 + tile * tm + r) * TOP_K + k]
                pltpu.make_async_copy(
                    ys_ref.at[pl.ds(pl.multiple_of(d * D_CHUNKS, D_CHUNKS), D_CHUNKS), :],
                    ybuf.at[slot, pl.ds(pl.multiple_of((k * tm + r) * D_CHUNKS, D_CHUNKS), D_CHUNKS), :],
                    sem.at[slot]).start()
            return carry
        lax.fori_loop(0, tm, body, 0)

    @pl.when(i == 0)
    def _():
        issue(0, 0)

    @pl.when(i + 1 < n)
    def _():
        issue(i + 1, (i + 1) % 2)

    slot = i % 2
    pltpu.make_async_copy(ys_ref.at[pl.ds(0, tile_rows), :], ybuf.at[slot], sem.at[slot]).wait()
    w8 = w8_ref[...]
    wk = [w8[:, k:k + 1] for k in range(TOP_K)]
    for j in range(D_CHUNKS):
        acc = None
        for k in range(TOP_K):
            v = ybuf[slot, pl.ds(k * tm * D_CHUNKS + j, tm, stride=D_CHUNKS), :] * wk[k]
            acc = v if acc is None else acc + v
        ymoe_ref[:, j * LANES:(j + 1) * LANES] = acc

    h = h2_ref[...]
    gt = _dot(h, wg_ref[...])
    a = (gt * _sigmoid(gt) * _dot(h, wu_ref[...])).astype(BF16)
    f = ymoe_ref[...] + _dot(a, wd_ref[...])
    x2 = x1_ref[...] + gate_ref[...] * f
    y_ref[...] = _rms(x2) * gf_ref[...] * (1.0 + scale_ref[...]) + shift_ref[...]


def _final(dest_flat, x1, h2, w8, y_sorted, wg, wu, wd, gate, gf, shift, scale, m, row0, tm=64):
    d = x1.shape[1]
    blk0 = row0 // tm
    in_row = lambda i, dr: (blk0 + i, 0)
    row = lambda i, dr: (i, 0)
    const = lambda i, dr: (0, 0)
    mod_spec = lambda a: pl.BlockSpec((tm, d), row) if a.shape[0] == m else pl.BlockSpec((1, d), const)
    grid_spec = pltpu.PrefetchScalarGridSpec(
        num_scalar_prefetch=1,
        grid=(m // tm,),
        in_specs=[pl.BlockSpec((tm, d), in_row), pl.BlockSpec((tm, d), in_row), pl.BlockSpec((tm, TOP_K), in_row),
                  pl.BlockSpec(memory_space=pl.ANY),
                  pl.BlockSpec(wg.shape, const), pl.BlockSpec(wu.shape, const), pl.BlockSpec(wd.shape, const),
                  mod_spec(gate), pl.BlockSpec((1, d), const), mod_spec(shift), mod_spec(scale)],
        out_specs=pl.BlockSpec((tm, d), row),
        scratch_shapes=[pltpu.VMEM((2, tm * TOP_K * D_CHUNKS, LANES), F32), pltpu.VMEM((tm, d), F32),
                        pltpu.SemaphoreType.DMA((2,))],
    )
    return pl.pallas_call(
        functools.partial(_final_kernel, tm=tm, row0=row0),
        grid_spec=grid_spec,
        out_shape=jax.ShapeDtypeStruct((m, d), F32),
        compiler_params=_cparams(("arbitrary",)),
        name="final",
    )(dest_flat, x1, h2, w8, y_sorted, wg, wu, wd, gate, gf, shift, scale)


def kernel(x_prompt, x_sample, cache_sb, cache_nsa, state_win, page_table, c_prompt, c_sample, w_ada, b_ada, norm1_g, w_in, cmp_pos_k, cmp_pos_v, cmp_wk, cmp_wv, out_norm_g, w_out, norm2_g, w_router, router_bias, w_gate_e, w_up_e, w_down_e, w_gate_s, w_up_s, w_down_s, normf_g):
    b_p, t, d = x_prompt.shape
    bs, n_new, _ = x_sample.shape
    assert b_p == 1 and d == D_MODEL
    n_s = bs * n_new
    n_tok = t + n_s

    c_all = jnp.concatenate([c_prompt, c_sample], axis=0)
    m_pad = -(-c_all.shape[0] // 8) * 8
    c_all = jnp.pad(c_all, ((0, m_pad - c_all.shape[0]), (0, 0)))
    mod = _ada(c_all, w_ada, b_ada).reshape(m_pad, N_ADA, d)
    mod_p = [mod[0:1, i] for i in range(N_ADA)]
    mod_s = [jnp.repeat(mod[1:1 + bs, i], n_new, axis=0) for i in range(N_ADA)]

    w_main = w_in[:, :COL_GATE]
    w_gate = jnp.pad(w_in[:, COL_GATE:], ((0, 0), (0, LANES - N_GATE)))
    w_cat = jnp.concatenate([w_main, w_gate], axis=1).astype(BF16)
    row1 = lambda v: v.reshape(1, -1)

    xp = x_prompt.reshape(t, d)
    xs = x_sample.reshape(n_s, d)
    (qsb_p, sbrows_p, kvsb_p, qn_p, nsarows_p, nsab_p, winrows_p, winb_p, gates_p) = _project(
        xp, row1(norm1_g), mod_p[0], mod_p[1], w_cat)
    (qsb_s, sbrows_s, _, qn_s, nsarows_s, _, winrows_s, _, gates_s) = _project(
        xs, row1(norm1_g), mod_s[0], mod_s[1], w_cat)

    osb_p = _sb_prompt(qsb_p, kvsb_p)
    wpos = jnp.stack([jnp.repeat(cmp_pos_k, HEAD_DIM, axis=1), jnp.repeat(cmp_pos_v, HEAD_DIM, axis=1)])
    wc = jnp.stack([cmp_wk, cmp_wv]).astype(BF16)
    kvc_p = _compress_prompt(nsarows_p, wpos, wc)
    on_p = _nsa_prompt_t(qn_p, gates_p, kvc_p, nsab_p, winb_p)

    per_seq = lambda a: a.reshape(bs, n_new, a.shape[-1])
    n_phys = cache_sb.shape[0]
    w_buf = state_win.shape[1]
    osb_s = _sb_sample(page_table, per_seq(qsb_s), per_seq(sbrows_s), cache_sb.reshape(n_phys, -1, HEAD_DIM))
    on_s, win_s = _nsa_sample(page_table, per_seq(qn_s), per_seq(gates_s), per_seq(nsarows_s), per_seq(winrows_s),
                              cache_nsa.reshape(n_phys, -1, HEAD_DIM), state_win.reshape(bs, -1, HEAD_DIM), wpos, wc)

    w_out_b = w_out.astype(BF16)
    gout = out_norm_g.reshape(1, -1)
    bufs = _mixout(osb_p, on_p, xp, gout, w_out_b, mod_p[2], row1(norm2_g), mod_p[3], mod_p[4], n_tok, 0)
    x1, h2, h2g = _mixout(osb_s.reshape(n_s, SB_W), on_s.reshape(n_s, NSA_W), xs, gout, w_out_b,
                          mod_s[2], row1(norm2_g), mod_s[3], mod_s[4], n_tok, t, prev=bufs)

    eidx_t, w_t, routed_t = _router(h2, w_router.T.astype(BF16), router_bias)
    dest, blk_e, nxt_e, par, n_active, n_blk = _dispatch_plan(eidx_t, routed_t)
    dest_flat = dest.reshape(-1)
    x_sorted = _dispatch(dest_flat, h2g, n_blk * MOE_ROWS)
    y_sorted = _experts(blk_e, nxt_e, par, n_active, x_sorted, w_gate_e, w_up_e, w_down_e)
    w8 = w_t.T
    ws = (w_gate_s.astype(BF16), w_up_s.astype(BF16), w_down_s.astype(BF16))
    y_p = _final(dest_flat, x1, h2, w8, y_sorted, *ws, mod_p[5], row1(normf_g), mod_p[6], mod_p[7], t, 0)
    y_s = _final(dest_flat, x1, h2, w8, y_sorted, *ws, mod_s[5], row1(normf_g), mod_s[6], mod_s[7], n_s, t)

    keep = min(WINDOW, t)
    return (y_p.reshape(1, t, d), y_s.reshape(bs, n_new, d),
            sbrows_p.reshape(1, t, 2, N_SB_HEADS, HEAD_DIM), sbrows_s.reshape(bs, n_new, 2, N_SB_HEADS, HEAD_DIM),
            nsarows_p.reshape(1, t, 4, N_NSA_KV, HEAD_DIM), nsarows_s.reshape(bs, n_new, 4, N_NSA_KV, HEAD_DIM),
            winrows_p[t - keep:].reshape(1, keep, 2, N_NSA_KV, HEAD_DIM),
            win_s.reshape(bs, w_buf, 2, N_NSA_KV, HEAD_DIM))
```

```python
import functools

import numpy as np
import jax
import jax.numpy as jnp
from jax import lax
from jax.experimental import pallas as pl
from jax.experimental.pallas import tpu as pltpu

F32 = jnp.float32
BF16 = jnp.bfloat16
U32 = jnp.uint32

D_MODEL = 2048
HEAD_DIM = 128
N_SB_HEADS = 8
N_NSA_HEADS = 8
N_NSA_KV = 2
NSA_GROUP = N_NSA_HEADS // N_NSA_KV
CMP_BLOCK = 32
CMP_STRIDE = 16
SLC_BLOCK = 64
SLC_TOP = 16
WINDOW = 512
FORCE = 1e4
NEG = -1e30
N_EXPERTS = 256
TOP_K = 8
N_EXPERT_GROUPS = 8
TOPK_GROUPS = 4
D_EXPERT = D_MODEL // 4
ROUTED_SCALE = 2.5
MOE_ROWS = 128
N_ADA = 8
EPS = 1e-6
ATT_SCALE = HEAD_DIM ** -0.5

SB_W = N_SB_HEADS * HEAD_DIM
NSA_W = N_NSA_HEADS * HEAD_DIM
KV_W = N_NSA_KV * HEAD_DIM
COL_QSB = 0
COL_KVSB = SB_W
COL_QN = 3 * SB_W
COL_NSA = COL_QN + NSA_W
COL_WIN = COL_NSA + 4 * KV_W
COL_GATE = COL_WIN + 2 * KV_W
N_GATE = 3 * N_NSA_HEADS
LANES = 128
V7X_VMEM_LIMIT = 56 * 1024 * 1024
D_CHUNKS = D_MODEL // LANES
ROW_WORDS = D_CHUNKS // 2

NT = (((1,), (1,)), ((), ()))


def _cparams(sem):
    return pltpu.CompilerParams(dimension_semantics=sem, vmem_limit_bytes=V7X_VMEM_LIMIT)


def _sigmoid(x):
    return 1.0 / (1.0 + jnp.exp(-x))


def _split_bf16(x):
    hi = x.astype(BF16)
    lo = (x - hi.astype(F32)).astype(BF16)
    return hi, lo


def _dot(a, b):
    return jnp.dot(a, b, preferred_element_type=F32)


def _dot_nt(a, b):
    return lax.dot_general(a, b, NT, preferred_element_type=F32)


def _rms(x):
    return x * lax.rsqrt(jnp.mean(x * x, axis=-1, keepdims=True) + EPS)


def _masked_softmax(s, mask):
    s = jnp.where(mask, s, NEG)
    m = jnp.max(s, axis=-1, keepdims=True)
    e = jnp.where(mask, jnp.exp(s - m), 0.0)
    return e / jnp.maximum(jnp.sum(e, axis=-1, keepdims=True), 1e-30)


def _head_slope(head):
    out = jnp.full(head.shape, 2.0 ** -N_NSA_HEADS, F32)
    for i in range(N_NSA_HEADS - 1):
        out = jnp.where(head == i, 2.0 ** -(i + 1), out)
    return out


def _top_blocks_by_rank(score, n_top, n_cand):
    blk = lax.broadcasted_iota(jnp.int32, score.shape, 1)
    rank = jnp.zeros(score.shape, F32)
    for i in range(n_cand):
        c = score[:, i:i + 1]
        tie = jnp.where(blk > i, 1.0, 0.0)
        rank = rank + jnp.where(c > score, 1.0, jnp.where(c == score, tie, 0.0))
    return jnp.where(rank < n_top, 1.0, 0.0)


def _slc_scores(imp, qpos):
    blk = lax.broadcasted_iota(jnp.int32, imp.shape, 1)
    qb = qpos // SLC_BLOCK
    forced = (blk == 0) | ((blk <= qb) & (blk >= qb - 1))
    return jnp.where(blk > qb, -1.0, jnp.where(forced, FORCE, imp))


def _gate_cols(gates, branch, group, n_rows_per_head):
    lane = lax.broadcasted_iota(jnp.int32, gates.shape, 1)
    cols = [jnp.sum(jnp.where(lane == branch * N_NSA_HEADS + group * NSA_GROUP + h, gates, 0.0),
                    axis=-1, keepdims=True) for h in range(NSA_GROUP)]
    return jnp.concatenate(cols, axis=0)


def _ada_kernel(c_ref, w_ref, b_ref, o_ref):
    c = c_ref[...]
    a = (c * _sigmoid(c)).astype(BF16)
    o_ref[...] = _dot(a, w_ref[...].astype(BF16)) + b_ref[...]


def _ada(c, w_ada, b_ada, tn=1024):
    m, d = c.shape
    n = w_ada.shape[1]
    return pl.pallas_call(
        _ada_kernel,
        grid=(n // tn,),
        in_specs=[pl.BlockSpec((m, d), lambda j: (0, 0)),
                  pl.BlockSpec((d, tn), lambda j: (0, j)),
                  pl.BlockSpec((1, tn), lambda j: (0, j))],
        out_specs=pl.BlockSpec((m, tn), lambda j: (0, j)),
        out_shape=jax.ShapeDtypeStruct((m, n), F32),
        compiler_params=_cparams(("arbitrary",)),
        name="ada",
    )(c, w_ada, b_ada.reshape(1, n))


def _proj_kernel(x_ref, g_ref, shift_ref, scale_ref, w_ref,
                 qsb_ref, sbrows_ref, kvsb_ref, qn_ref, nsarows_ref, nsab_ref,
                 winrows_ref, winb_ref, gates_ref):
    h = (_rms(x_ref[...]) * g_ref[...] * (1.0 + scale_ref[...]) + shift_ref[...]).astype(BF16)
    ch = 512

    def mm(c0, n):
        return _dot(h, w_ref[:, c0:c0 + n])

    for c in range(0, SB_W, ch):
        qsb_ref[:, c:c + ch] = mm(COL_QSB + c, ch).astype(BF16)
    for c in range(0, 2 * SB_W, ch):
        r = mm(COL_KVSB + c, ch)
        sbrows_ref[:, c:c + ch] = r
        kvsb_ref[:, c:c + ch] = r.astype(BF16)
    for c in range(0, NSA_W, ch):
        qn_ref[:, c:c + ch] = mm(COL_QN + c, ch).astype(BF16)
    for c in range(0, 4 * KV_W, ch):
        r = mm(COL_NSA + c, ch)
        nsarows_ref[:, c:c + ch] = r
        nsab_ref[:, c:c + ch] = r.astype(BF16)
    r = mm(COL_WIN, 2 * KV_W)
    winrows_ref[...] = r
    winb_ref[...] = r.astype(BF16)
    gates_ref[...] = _sigmoid(mm(COL_GATE, LANES))


def _project(x, g, shift, scale, w_cat, tm=256):
    m, d = x.shape
    row = lambda i: (i, 0)
    const = lambda i: (0, 0)
    mod_spec = lambda a: pl.BlockSpec((tm, d), row) if a.shape[0] == m else pl.BlockSpec((1, d), const)
    widths = [(SB_W, BF16), (2 * SB_W, F32), (2 * SB_W, BF16), (NSA_W, BF16), (4 * KV_W, F32),
              (4 * KV_W, BF16), (2 * KV_W, F32), (2 * KV_W, BF16), (LANES, F32)]
    return pl.pallas_call(
        _proj_kernel,
        grid=(m // tm,),
        in_specs=[pl.BlockSpec((tm, d), row),
                  pl.BlockSpec((1, d), const),
                  mod_spec(shift), mod_spec(scale),
                  pl.BlockSpec(w_cat.shape, const, pipeline_mode=pl.Buffered(1))],
        out_specs=[pl.BlockSpec((tm, w), row) for w, _ in widths],
        out_shape=[jax.ShapeDtypeStruct((m, w), dt) for w, dt in widths],
        compiler_params=_cparams(("arbitrary",)),
        name="project",
    )(x, g, shift, scale, w_cat)


def _sb_logits(z):
    l1m = jnp.minimum(-z, 0.0) - jnp.log(1.0 + jnp.exp(-jnp.abs(z)))
    return l1m + z, l1m


def _sb_prompt_kernel(q_ref, k_ref, v_ref, o_ref, *, tq, heads):
    i = pl.program_id(1)
    row = lax.broadcasted_iota(jnp.int32, (2 * tq, tq), 0)
    col = lax.broadcasted_iota(jnp.int32, (2 * tq, tq), 1)
    later2 = jnp.where(jnp.where(row >= tq, row - tq, row) > col, 1.0, 0.0).astype(BF16)
    causal = lax.broadcasted_iota(jnp.int32, (tq, tq), 1) < lax.broadcasted_iota(jnp.int32, (tq, tq), 0)

    hsl = [slice(hh * HEAD_DIM, (hh + 1) * HEAD_DIM) for hh in range(heads)]

    def block(j, state, diagonal):
        off = pl.multiple_of(j * tq, tq)
        zs = [_dot_nt(q_ref[:, hs], k_ref[pl.ds(off, tq), hs]) * ATT_SCALE for hs in hsl]
        logits = [_sb_logits(z) for z in zs]
        if diagonal:
            logits = [(ls, jnp.where(causal, l1m, 0.0)) for ls, l1m in logits]
        split = [jnp.concatenate(_split_bf16(l1m), axis=1) for _, l1m in logits]
        suffix = [_dot(sp, later2) + state[hh][0] for hh, sp in enumerate(split)]
        a = [jnp.exp(logits[hh][0] + suffix[hh]) for hh in range(heads)]
        if diagonal:
            a = [jnp.where(causal, x, 0.0) for x in a]
        acc = [state[hh][1] + _dot(a[hh].astype(BF16), v_ref[pl.ds(off, tq), hsl[hh]]) for hh in range(heads)]
        return tuple((suffix[hh][:, :1] + logits[hh][1][:, :1], acc[hh]) for hh in range(heads))

    init = tuple((jnp.zeros((tq, 1), F32), jnp.zeros((tq, HEAD_DIM), F32)) for _ in range(heads))
    state = block(i, init, True)
    state = lax.fori_loop(0, i, lambda t, st: block(i - 1 - t, st, False), state)
    for hh in range(heads):
        o_ref[:, hh * HEAD_DIM:(hh + 1) * HEAD_DIM] = state[hh][1]


def _sb_prompt(qsb, kvsb, tq=256, heads=4):
    t = qsb.shape[0]
    w = heads * HEAD_DIM
    n_hp = N_SB_HEADS // heads
    return pl.pallas_call(
        functools.partial(_sb_prompt_kernel, tq=tq, heads=heads),
        grid=(n_hp, t // tq),
        in_specs=[pl.BlockSpec((tq, w), lambda h, i: (i, h)),
                  pl.BlockSpec((t, w), lambda h, i: (0, h)),
                  pl.BlockSpec((t, w), lambda h, i: (0, n_hp + h))],
        out_specs=pl.BlockSpec((tq, w), lambda h, i: (i, h)),
        out_shape=jax.ShapeDtypeStruct((t, SB_W), F32),
        compiler_params=_cparams(("arbitrary", "arbitrary")),
        name="sb_prompt",
    )(qsb, kvsb, kvsb)


def _sb_sample_kernel(pt_ref, q_ref, new_ref, *refs, n_new, ppg):
    cache_refs = refs[:ppg]
    o_ref, qbd_ref, carry_ref, acc_ref = refs[ppg:]
    p = pl.program_id(1)
    rows = N_SB_HEADS * n_new
    kk = lax.broadcasted_iota(jnp.int32, (2 * LANES, 2 * LANES), 0)
    cc = lax.broadcasted_iota(jnp.int32, (2 * LANES, 2 * LANES), 1)
    kk = jnp.where(kk >= LANES, kk - LANES, kk)
    later_and_all = jnp.where((cc >= LANES) | (kk > cc), 1.0, 0.0).astype(BF16)
    stride = 2 * N_SB_HEADS
    hsl = [slice(h * HEAD_DIM, (h + 1) * HEAD_DIM) for h in range(N_SB_HEADS)]
    rsl = [slice(h * n_new, (h + 1) * n_new) for h in range(N_SB_HEADS)]

    def scores(kmat, diagonal):
        z = _dot_nt(qbd_ref[...].astype(BF16), kmat) * ATT_SCALE
        ls, l1m = _sb_logits(z)
        causal = None
        if diagonal:
            key = lax.broadcasted_iota(jnp.int32, (rows, LANES), 1)
            qry = lax.broadcasted_iota(jnp.int32, (rows, LANES), 0) % n_new
            causal = key < qry
            l1m = jnp.where(causal, l1m, 0.0)
        hi, lo = _split_bf16(l1m)
        return ls, _dot(jnp.concatenate([hi, lo], axis=1), later_and_all), causal

    def accumulate(ls, st, causal, vmat, carry):
        a = jnp.exp(ls + st[:, :LANES] + carry)
        if causal is not None:
            a = jnp.where(causal, a, 0.0)
        o_all = _dot(a.astype(BF16), vmat)
        for h in range(N_SB_HEADS):
            acc_ref[rsl[h], :] += o_all[rsl[h], hsl[h]]
        return carry + st[:, LANES:]

    @pl.when(p == 0)
    def _():
        acc_ref[...] = jnp.zeros_like(acc_ref)
        qbd_ref[...] = jnp.zeros_like(qbd_ref)
        for h in range(N_SB_HEADS):
            qbd_ref[rsl[h], hsl[h]] = q_ref[0, :, hsl[h]].astype(F32)
        pad = jnp.zeros((LANES - n_new, SB_W), F32)
        knew = jnp.concatenate([new_ref[0, :, :SB_W], pad], axis=0).astype(BF16)
        vnew = jnp.concatenate([new_ref[0, :, SB_W:], pad], axis=0).astype(BF16)
        ls, st, causal = scores(knew, True)
        carry_ref[...] = accumulate(ls, st, causal, vnew, jnp.zeros((rows, LANES), F32))

    def page_mat(ref, c0):
        return jnp.concatenate([ref[0, pl.ds(c0 + h, LANES, stride=stride), :].astype(BF16)
                                for h in range(N_SB_HEADS)], axis=1)

    qbd = qbd_ref[...].astype(BF16)
    zs = [_dot_nt(qbd, page_mat(ref, 0)) * ATT_SCALE for ref in cache_refs]
    logits = [_sb_logits(z) for z in zs]
    sts = [_dot(jnp.concatenate(_split_bf16(l1m), axis=1), later_and_all) for _, l1m in logits]
    carry = carry_ref[...]
    probs = []
    for (ls, _), st in zip(logits, sts):
        probs.append(jnp.exp(ls + st[:, :LANES] + carry).astype(BF16))
        carry = carry + st[:, LANES:]
    carry_ref[...] = carry
    outs = [_dot(a, page_mat(ref, N_SB_HEADS)) for a, ref in zip(probs, cache_refs)]
    for h in range(N_SB_HEADS):
        acc = acc_ref[rsl[h], :]
        for o_all in outs:
            acc = acc + o_all[rsl[h], hsl[h]]
        acc_ref[rsl[h], :] = acc

    @pl.when(p == pl.num_programs(1) - 1)
    def _():
        for h in range(N_SB_HEADS):
            o_ref[0, :, hsl[h]] = acc_ref[rsl[h], :]


def _sb_sample(page_table, qsb, sbrows, cache, ppg=16):
    bs, n_new, _ = qsb.shape
    n_pages = page_table.shape[1]
    page_rows = cache.shape[1]
    assert page_rows == LANES * 2 * N_SB_HEADS and n_pages % ppg == 0
    rows = N_SB_HEADS * n_new

    def page_map(q):
        return lambda b, p, pt: (pt[b * n_pages + n_pages - 1 - (p * ppg + q)], 0, 0)

    grid_spec = pltpu.PrefetchScalarGridSpec(
        num_scalar_prefetch=1,
        grid=(bs, n_pages // ppg),
        in_specs=[pl.BlockSpec((1, n_new, SB_W), lambda b, p, pt: (b, 0, 0)),
                  pl.BlockSpec((1, n_new, 2 * SB_W), lambda b, p, pt: (b, 0, 0))]
                 + [pl.BlockSpec((1, page_rows, HEAD_DIM), page_map(q)) for q in range(ppg)],
        out_specs=pl.BlockSpec((1, n_new, SB_W), lambda b, p, pt: (b, 0, 0)),
        scratch_shapes=[pltpu.VMEM((rows, SB_W), F32), pltpu.VMEM((rows, LANES), F32),
                        pltpu.VMEM((rows, HEAD_DIM), F32)],
    )
    return pl.pallas_call(
        functools.partial(_sb_sample_kernel, n_new=n_new, ppg=ppg),
        grid_spec=grid_spec,
        out_shape=jax.ShapeDtypeStruct((bs, n_new, SB_W), F32),
        compiler_params=_cparams(("arbitrary", "arbitrary")),
        name="sb_sample",
    )(page_table.reshape(-1), qsb, sbrows, *([cache] * ppg))


def _pool_chunks(x, w1, w2):
    x3 = x.reshape(x.shape[0] // CMP_STRIDE, CMP_STRIDE, x.shape[1])
    return jnp.sum(x3 * w1[None], axis=1), jnp.sum(x3 * w2[None], axis=1)


def _compress_kernel(rows_ref, wpos_ref, wc_ref, o_ref, first_ref, second_ref, *, step_rows):
    n_rows = rows_ref.shape[0]
    n_chunks = n_rows // CMP_STRIDE
    w1 = wpos_ref[0, :CMP_STRIDE, :]
    w2 = wpos_ref[0, CMP_STRIDE:, :]
    cps = step_rows // CMP_STRIDE

    def body(s, carry):
        r0 = pl.multiple_of(s * step_rows, step_rows)
        c0 = pl.multiple_of(s * cps, cps)
        f, sec = _pool_chunks(rows_ref[pl.ds(r0, step_rows), :], w1, w2)
        first_ref[pl.ds(c0, cps), :] = f
        second_ref[pl.ds(c0, cps), :] = sec
        return carry

    lax.fori_loop(0, n_rows // step_rows, body, 0)
    pre = first_ref[...] + pltpu.roll(second_ref[...], n_chunks - 1, axis=0)
    for g in range(N_NSA_KV):
        sl = slice(g * HEAD_DIM, (g + 1) * HEAD_DIM)
        o_ref[0, :, sl] = _dot(pre[:, sl].astype(BF16), wc_ref[0, g]).astype(BF16)


def _compress_prompt(nsarows, wpos, wc, step_rows=1024):
    t = nsarows.shape[0]
    n_chunks = t // CMP_STRIDE
    return pl.pallas_call(
        functools.partial(_compress_kernel, step_rows=step_rows),
        grid=(2,),
        in_specs=[pl.BlockSpec((t, KV_W), lambda c: (0, c)),
                  pl.BlockSpec((1, CMP_BLOCK, KV_W), lambda c: (c, 0, 0)),
                  pl.BlockSpec((1, N_NSA_KV, HEAD_DIM, HEAD_DIM), lambda c: (c, 0, 0, 0))],
        out_specs=pl.BlockSpec((1, n_chunks, KV_W), lambda c: (c, 0, 0)),
        out_shape=jax.ShapeDtypeStruct((2, n_chunks, KV_W), BF16),
        scratch_shapes=[pltpu.VMEM((n_chunks, KV_W), F32), pltpu.VMEM((n_chunks, KV_W), F32)],
        compiler_params=_cparams(("arbitrary",)),
        name="compress_prompt",
    )(nsarows, wpos, wc)


def _importance_matrix(n_cmp, n_slc_lanes):
    m = np.arange(n_cmp)[:, None]
    j = np.arange(n_slc_lanes)[None, :]
    d = m - (SLC_BLOCK // CMP_STRIDE) * j
    w = np.where((d >= 0) & (d <= 2), 1.0, np.where((d == -1) | (d == 3), 0.5, 0.0))
    return jnp.asarray(w, BF16)


def _block_expansion(n_keys):
    b = np.arange(LANES)[:, None]
    k = np.arange(n_keys)[None, :]
    return jnp.asarray((b == k // SLC_BLOCK).astype(np.float32), BF16)


def _top_blocks_t(score, n_top):
    blk = lax.broadcasted_iota(jnp.int32, score.shape, 0).astype(F32)
    sel = jnp.zeros(score.shape, F32)
    for _ in range(n_top):
        m = jnp.max(score, axis=0, keepdims=True)
        idx = jnp.min(jnp.where(score == m, blk, float(score.shape[0])), axis=0, keepdims=True)
        pick = blk == idx
        sel = jnp.where(pick, 1.0, sel)
        score = jnp.where(pick, -2.0, score)
    return sel


def _nsa_prompt_t_kernel(q_ref, gate_ref, kc_ref, vct_ref, sk_ref, svt_ref, wk_ref, wvt_ref,
                         et_ref, mimpt_ref, o_ref, cb_ref, *, tq, tk):
    i = pl.program_id(0)
    start = i * tq
    cols = NSA_GROUP * tq
    groups = range(N_NSA_KV)
    gsl = [slice(g * HEAD_DIM, (g + 1) * HEAD_DIM) for g in groups]
    hsl = [slice(h * tq, (h + 1) * tq) for h in range(NSA_GROUP)]
    slope_of = lambda g, h: 2.0 ** -(g * NSA_GROUP + h + 1)
    drop = 2.0 * NEG
    n_cmp = kc_ref.shape[1]

    @pl.when(i == 0)
    def _():
        kio = lax.broadcasted_iota(jnp.int32, (tk, tq), 0).astype(F32)
        for hh in range(N_NSA_HEADS):
            cb_ref[hh] = (2.0 ** -(hh + 1)) * kio

    qpos = start + lax.broadcasted_iota(jnp.int32, (1, tq), 1)
    qpos_f = qpos.astype(F32)

    def head_row(fn):
        return jnp.concatenate([fn(h) for h in range(NSA_GROUP)], axis=1)

    qg, o_cmp, selb, first_tile = [], [], [], []
    dist_c = qpos - (CMP_STRIDE * lax.broadcasted_iota(jnp.int32, (n_cmp, tq), 0) + (CMP_BLOCK - 1))
    seen_c = dist_c >= 0
    dist_cf = dist_c.astype(F32)
    for g in groups:
        qg.append(jnp.concatenate(
            [q_ref[:, (g * NSA_GROUP + h) * HEAD_DIM:(g * NSA_GROUP + h + 1) * HEAD_DIM] for h in range(NSA_GROUP)],
            axis=0))
        zc = _dot_nt(kc_ref[0, :, gsl[g]], qg[g]) * ATT_SCALE
        ps = []
        for h in range(NSA_GROUP):
            s = jnp.where(seen_c, zc[:, hsl[h]] - slope_of(g, h) * dist_cf, NEG)
            e = jnp.where(seen_c, jnp.exp(s - jnp.max(s, axis=0, keepdims=True)), 0.0)
            ps.append(e / jnp.maximum(jnp.sum(e, axis=0, keepdims=True), 1e-30))
        o_cmp.append(_dot(vct_ref[gsl[g], :], jnp.concatenate(ps, axis=1).astype(BF16)))
        hi, lo = _split_bf16(ps[0] + ps[1] + ps[2] + ps[3])
        imp = _dot(mimpt_ref[...], hi) + _dot(mimpt_ref[...], lo)
        blk = lax.broadcasted_iota(jnp.int32, imp.shape, 0)
        qb = qpos // SLC_BLOCK
        forced = (blk == 0) | ((blk <= qb) & (blk >= qb - 1))
        score = jnp.where(blk > qb, -1.0, jnp.where(forced, FORCE, imp))
        sel = _top_blocks_t(score, SLC_TOP)
        selb.append(((sel - 1.0) * (-drop)).astype(BF16))
        far = jnp.min(jnp.where((sel > 0.5) & (blk > 0), blk.astype(F32), float(LANES)))
        first_tile.append(far.astype(jnp.int32) // (tk // SLC_BLOCK))

    def attend(k_ref, vt_ref, t_lo, t_hi, selected):
        def tile(t, state, edge, active=tuple(groups)):
            off = pl.multiple_of(t * tk, tk)
            off_f = off.astype(F32)
            zs = {g: _dot_nt(k_ref[pl.ds(off, tk), gsl[g]], qg[g]) for g in active}
            mbs = {g: _dot(et_ref[t], selb[g]) for g in active} if selected else None
            if edge:
                dist = qpos - (off + lax.broadcasted_iota(jnp.int32, (tk, tq), 0))
                ok = (dist >= 0) if selected else ((dist >= 0) & (dist < WINDOW))
            out = []
            for g in groups:
                if g not in active:
                    out.append(state[g])
                    continue
                parts = []
                for h in range(NSA_GROUP):
                    sh = zs[g][:, hsl[h]] * ATT_SCALE + cb_ref[g * NSA_GROUP + h]
                    if selected:
                        sh = sh + mbs[g]
                    if edge:
                        sh = jnp.where(ok, sh, drop)
                    parts.append(sh)
                s = jnp.concatenate(parts, axis=1)
                shift = head_row(lambda h: slope_of(g, h) * (qpos_f - off_f))
                m_old, l_old, acc_old = state[g]
                m_new = jnp.maximum(m_old, jnp.max(s, axis=0, keepdims=True) - shift)
                e = jnp.exp(s - (shift + m_new))
                alpha = jnp.exp(m_old - m_new)
                pv = _dot(vt_ref[t, gsl[g], :], e.astype(BF16))
                out.append((m_new, l_old * alpha + jnp.sum(e, axis=0, keepdims=True), acc_old * alpha + pv))
            return tuple(out)

        init = tuple((jnp.full((1, cols), NEG, F32), jnp.zeros((1, cols), F32), jnp.zeros((HEAD_DIM, cols), F32))
                     for _ in groups)
        if selected:
            edge_t = t_hi - 1
            lo = [jnp.clip(first_tile[g], 1, edge_t) for g in groups]
            both = jnp.maximum(lo[0], lo[1])
            state = lax.fori_loop(0, jnp.minimum(edge_t, 1), lambda t, st: tile(t, st, False), init)
            for g in groups:
                state = lax.fori_loop(lo[g], both, lambda t, st, g=g: tile(t, st, False, (g,)), state)
            state = lax.fori_loop(both, edge_t, lambda t, st: tile(t, st, False), state)
            state = tile(edge_t, state, True)
        else:
            state = lax.fori_loop(t_lo, t_hi, lambda t, st: tile(t, st, True), init)
        return [state[g][2] / jnp.maximum(state[g][1], 1e-30) for g in groups]

    last = (start + tq - 1) // tk
    o_slc = attend(sk_ref, svt_ref, 0, last + 1, True)
    o_win = attend(wk_ref, wvt_ref, jnp.maximum(start - WINDOW, 0) // tk, last + 1, False)

    gates_t = gate_ref[...].T
    for g in groups:
        def gate_row(branch):
            return head_row(lambda h: gates_t[branch * N_NSA_HEADS + g * NSA_GROUP + h:
                                              branch * N_NSA_HEADS + g * NSA_GROUP + h + 1, :])
        o = gate_row(0) * o_cmp[g] + gate_row(1) * o_slc[g] + gate_row(2) * o_win[g]
        for h in range(NSA_GROUP):
            c = (g * NSA_GROUP + h) * HEAD_DIM
            o_ref[:, c:c + HEAD_DIM] = o[:, hsl[h]].T


def _nsa_prompt_t(qn, gates, kvc, nsab, winb, tq=128, tk=256):
    t = qn.shape[0]
    n_cmp = kvc.shape[1]
    n_tiles = t // tk
    tiles_t = lambda v: v.T.reshape(KV_W, n_tiles, tk).transpose(1, 0, 2)
    svt = tiles_t(nsab[:, 3 * KV_W:])
    wvt = tiles_t(winb[:, KV_W:])
    vct = kvc[1].T
    et = jnp.transpose(_block_expansion(t)).reshape(n_tiles, tk, LANES)
    mimpt = jnp.transpose(_importance_matrix(n_cmp, LANES))
    once = pl.Buffered(1)
    full = lambda a: pl.BlockSpec(a.shape, lambda i: (0,) * a.ndim, pipeline_mode=once)
    col = lambda c: pl.BlockSpec((t, KV_W), lambda i: (0, c), pipeline_mode=once)
    return pl.pallas_call(
        functools.partial(_nsa_prompt_t_kernel, tq=tq, tk=tk),
        grid=(t // tq,),
        in_specs=[pl.BlockSpec((tq, NSA_W), lambda i: (i, 0)),
                  pl.BlockSpec((tq, LANES), lambda i: (i, 0)),
                  pl.BlockSpec((1, n_cmp, KV_W), lambda i: (0, 0, 0), pipeline_mode=once),
                  full(vct), col(2), full(svt), col(0), full(wvt), full(et), full(mimpt)],
        out_specs=pl.BlockSpec((tq, NSA_W), lambda i: (i, 0)),
        out_shape=jax.ShapeDtypeStruct((t, NSA_W), F32),
        scratch_shapes=[pltpu.VMEM((N_NSA_HEADS, tk, tq), F32)],
        compiler_params=_cparams(("arbitrary",)),
        name="nsa_prompt",
    )(qn, gates, kvc, vct, nsab, svt, winb, wvt, et, mimpt)


def _nsa_sample_kernel(pt_ref, q_ref, gate_ref, newrows_ref, newwin_ref, win_ref, wpos_ref, wc_ref,
                       e_ref, mimp_ref, *refs, n_new, n_pages, ppg):
    cache_refs = refs[:ppg]
    o_ref, winout_ref, first_ref, second_ref, ks_ref, vs_ref = refs[ppg:]
    p = pl.program_id(1)
    page = LANES
    past = n_pages * page
    cpp = page // CMP_STRIDE
    n_cmp = first_ref.shape[1]
    n_keys = ks_ref.shape[0]
    w_buf = win_ref.shape[1] // (2 * N_NSA_KV)
    gsl = [slice(g * HEAD_DIM, (g + 1) * HEAD_DIM) for g in range(N_NSA_KV)]
    kinds = 4 * N_NSA_KV

    @pl.when(p == 0)
    def _():
        first_ref[...] = jnp.zeros_like(first_ref)
        second_ref[...] = jnp.zeros_like(second_ref)

    for q, cache_ref in enumerate(cache_refs):
        pg = p * ppg + q
        c0 = pl.multiple_of(pg * cpp, cpp)
        r0 = pl.multiple_of(pg * page, page)
        for g in range(N_NSA_KV):
            for kv in range(2):
                x = cache_ref[0, pl.ds(kv * N_NSA_KV + g, page, stride=kinds), :]
                f, sec = _pool_chunks(x, wpos_ref[kv, :CMP_STRIDE, gsl[g]], wpos_ref[kv, CMP_STRIDE:, gsl[g]])
                first_ref[kv, pl.ds(c0, cpp), gsl[g]] = f
                second_ref[kv, pl.ds(c0, cpp), gsl[g]] = sec
            ks_ref[pl.ds(r0, page), gsl[g]] = cache_ref[0, pl.ds(2 * N_NSA_KV + g, page, stride=kinds), :].astype(BF16)
            vs_ref[pl.ds(r0, page), gsl[g]] = cache_ref[0, pl.ds(3 * N_NSA_KV + g, page, stride=kinds), :].astype(BF16)

    @pl.when(p == pl.num_programs(1) - 1)
    def _():
        new = newrows_ref[0]
        zpad = jnp.zeros((CMP_STRIDE - n_new, KV_W), F32)
        kvc = []
        for kv in range(2):
            xc = jnp.concatenate([new[:, kv * KV_W:(kv + 1) * KV_W], zpad], axis=0)
            first_ref[kv, past // CMP_STRIDE:past // CMP_STRIDE + 1, :] = jnp.sum(
                xc * wpos_ref[kv, :CMP_STRIDE, :], axis=0, keepdims=True)
            second_ref[kv, past // CMP_STRIDE:past // CMP_STRIDE + 1, :] = jnp.sum(
                xc * wpos_ref[kv, CMP_STRIDE:, :], axis=0, keepdims=True)
            pre = first_ref[kv] + pltpu.roll(second_ref[kv], n_cmp - 1, axis=0)
            kvc.append([_dot(pre[:, gsl[g]].astype(BF16), wc_ref[kv, g]).astype(BF16) for g in range(N_NSA_KV)])
        tail = jnp.zeros((n_keys - past - n_new, KV_W), F32)
        ks_ref[past:, :] = jnp.concatenate([new[:, 2 * KV_W:3 * KV_W], tail], axis=0).astype(BF16)
        vs_ref[past:, :] = jnp.concatenate([new[:, 3 * KV_W:4 * KV_W], tail], axis=0).astype(BF16)

        nwin = newwin_ref[0]
        wr = 2 * N_NSA_KV
        winout_ref[0, :(w_buf - n_new) * wr, :] = win_ref[0, n_new * wr:, :]
        for c in range(wr):
            winout_ref[0, pl.ds((w_buf - n_new) * wr + c, n_new, stride=wr), :] = nwin[:, c * HEAD_DIM:(c + 1) * HEAD_DIM]
        wtail = jnp.zeros((LANES - n_new, HEAD_DIM), F32)

        def window(c):
            return jnp.concatenate([win_ref[0, pl.ds(c, w_buf, stride=wr), :],
                                    nwin[:, c * HEAD_DIM:(c + 1) * HEAD_DIM], wtail], axis=0).astype(BF16)

        n_win = w_buf + LANES
        rows = NSA_GROUP * n_new
        r = lax.broadcasted_iota(jnp.int32, (rows, 1), 0)
        qpos = past + r % n_new
        qpos_q = past + lax.broadcasted_iota(jnp.int32, (n_new, 1), 0)
        kc_end = CMP_STRIDE * lax.broadcasted_iota(jnp.int32, (1, n_cmp), 1) + (CMP_BLOCK - 1)
        dist_s = qpos - lax.broadcasted_iota(jnp.int32, (1, n_keys), 1)
        dist_w = qpos - (past - w_buf + lax.broadcasted_iota(jnp.int32, (1, n_win), 1))
        gates = gate_ref[0]

        groups = range(N_NSA_KV)
        qgs = [jnp.concatenate(
            [q_ref[0, :, (g * NSA_GROUP + h) * HEAD_DIM:(g * NSA_GROUP + h + 1) * HEAD_DIM]
             for h in range(NSA_GROUP)], axis=0) for g in groups]
        slopes = [_head_slope(g * NSA_GROUP + r // n_new) for g in groups]
        z_cmp = [_dot_nt(qgs[g], kvc[0][g]) for g in groups]
        z_slc = [_dot_nt(qgs[g], ks_ref[:, gsl[g]]) for g in groups]
        z_win = [_dot_nt(qgs[g], window(g)) for g in groups]
        pcs = [_masked_softmax(z_cmp[g] * ATT_SCALE - slopes[g] * (qpos - kc_end).astype(F32), kc_end <= qpos)
               for g in groups]
        o_cmps = [_dot(pcs[g].astype(BF16), kvc[1][g]) for g in groups]
        imps = []
        for pc in pcs:
            p4 = pc[0:n_new] + pc[n_new:2 * n_new] + pc[2 * n_new:3 * n_new] + pc[3 * n_new:4 * n_new]
            hi, lo = _split_bf16(p4)
            imps.append(_dot(hi, mimp_ref[...]) + _dot(lo, mimp_ref[...]))
        n_slc = -(-(past + n_new) // SLC_BLOCK)
        sel_all = _top_blocks_by_rank(
            _slc_scores(jnp.concatenate(imps, axis=0), jnp.concatenate([qpos_q] * N_NSA_KV, axis=0)), SLC_TOP, n_slc)
        keeps = [_dot(jnp.concatenate([sel_all[g * n_new:(g + 1) * n_new]] * NSA_GROUP, axis=0).astype(BF16),
                      e_ref[...]) > 0.5 for g in groups]
        pss = [_masked_softmax(z_slc[g] * ATT_SCALE - slopes[g] * dist_s.astype(F32), (dist_s >= 0) & keeps[g])
               for g in groups]
        pws = [_masked_softmax(z_win[g] * ATT_SCALE - slopes[g] * dist_w.astype(F32),
                               (dist_w >= 0) & (dist_w < WINDOW)) for g in groups]
        o_slcs = [_dot(pss[g].astype(BF16), vs_ref[:, gsl[g]]) for g in groups]
        o_wins = [_dot(pws[g].astype(BF16), window(N_NSA_KV + g)) for g in groups]

        for g in groups:
            o = (_gate_cols(gates, 0, g, n_new) * o_cmps[g] + _gate_cols(gates, 1, g, n_new) * o_slcs[g]
                 + _gate_cols(gates, 2, g, n_new) * o_wins[g])
            for h in range(NSA_GROUP):
                c = (g * NSA_GROUP + h) * HEAD_DIM
                o_ref[0, :, c:c + HEAD_DIM] = o[h * n_new:(h + 1) * n_new]


def _nsa_sample(page_table, qn, gates, nsarows, winrows, cache, state_win, wpos, wc, ppg=16):
    bs, n_new, _ = qn.shape
    n_pages = page_table.shape[1]
    page_rows = cache.shape[1]
    assert page_rows == LANES * 4 * N_NSA_KV and n_pages % ppg == 0
    past = n_pages * LANES
    win_rows = state_win.shape[1]
    n_keys = past + LANES
    n_cmp = 2 * (past // CMP_STRIDE)
    e = _block_expansion(n_keys)
    mimp = _importance_matrix(n_cmp, LANES)
    seq = lambda b, p, pt: (b, 0, 0)
    c3 = lambda b, p, pt: (0, 0, 0)

    def page_map(q):
        return lambda b, p, pt: (pt[b * n_pages + p * ppg + q], 0, 0)

    grid_spec = pltpu.PrefetchScalarGridSpec(
        num_scalar_prefetch=1,
        grid=(bs, n_pages // ppg),
        in_specs=[pl.BlockSpec((1, n_new, NSA_W), seq),
                  pl.BlockSpec((1, n_new, LANES), seq),
                  pl.BlockSpec((1, n_new, 4 * KV_W), seq),
                  pl.BlockSpec((1, n_new, 2 * KV_W), seq),
                  pl.BlockSpec((1, win_rows, HEAD_DIM), seq),
                  pl.BlockSpec(wpos.shape, c3),
                  pl.BlockSpec(wc.shape, lambda b, p, pt: (0, 0, 0, 0)),
                  pl.BlockSpec(e.shape, lambda b, p, pt: (0, 0)),
                  pl.BlockSpec(mimp.shape, lambda b, p, pt: (0, 0))]
                 + [pl.BlockSpec((1, page_rows, HEAD_DIM), page_map(q)) for q in range(ppg)],
        out_specs=[pl.BlockSpec((1, n_new, NSA_W), seq),
                   pl.BlockSpec((1, win_rows, HEAD_DIM), seq)],
        scratch_shapes=[pltpu.VMEM((2, n_cmp, KV_W), F32), pltpu.VMEM((2, n_cmp, KV_W), F32),
                        pltpu.VMEM((n_keys, KV_W), BF16), pltpu.VMEM((n_keys, KV_W), BF16)],
    )
    return pl.pallas_call(
        functools.partial(_nsa_sample_kernel, n_new=n_new, n_pages=n_pages, ppg=ppg),
        grid_spec=grid_spec,
        out_shape=[jax.ShapeDtypeStruct((bs, n_new, NSA_W), F32),
                   jax.ShapeDtypeStruct((bs, win_rows, HEAD_DIM), F32)],
        compiler_params=_cparams(("arbitrary", "arbitrary")),
        name="nsa_sample",
    )(page_table.reshape(-1), qn, gates, nsarows, winrows, state_win, wpos, wc, e, mimp, *([cache] * ppg))


def _mixout_kernel(osb_ref, on_ref, x_ref, gout_ref, w_ref, gate_ref, g2_ref, shift_ref, scale_ref, *rest):
    x1_ref, h2_ref, h2g_ref, nrm_ref = rest[-4:]
    tm = x_ref.shape[0]
    for h in range(N_SB_HEADS + N_NSA_HEADS):
        src = osb_ref if h < N_SB_HEADS else on_ref
        c = (h % N_SB_HEADS) * HEAD_DIM
        sl = slice(h * HEAD_DIM, (h + 1) * HEAD_DIM)
        nrm_ref[:, sl] = (_rms(src[:, c:c + HEAD_DIM]) * gout_ref[:, sl]).astype(BF16)
    x1 = x_ref[...] + gate_ref[...] * _dot(nrm_ref[...], w_ref[...])
    x1_ref[...] = x1
    h2 = _rms(x1) * g2_ref[...] * (1.0 + scale_ref[...]) + shift_ref[...]
    h2_ref[...] = h2.astype(BF16)
    for s in range(ROW_WORDS):
        lo = h2[:, (2 * s) * LANES:(2 * s + 1) * LANES].astype(BF16).astype(F32)
        hi = h2[:, (2 * s + 1) * LANES:(2 * s + 2) * LANES].astype(BF16).astype(F32)
        word = (lax.bitcast_convert_type(lo, U32) >> 16) | lax.bitcast_convert_type(hi, U32)
        h2g_ref[pl.ds(s, tm, stride=ROW_WORDS), :] = word


def _mixout(osb, on, x, gout, w_out, gate, g2, shift, scale, n_total, row0, prev=None, tm=256):
    m, d = x.shape
    blk0 = row0 // tm
    row = lambda i: (i, 0)
    out_row = lambda i: (blk0 + i, 0)
    const = lambda i: (0, 0)
    mod_spec = lambda a: pl.BlockSpec((tm, d), row) if a.shape[0] == m else pl.BlockSpec((1, d), const)
    in_specs = [pl.BlockSpec((tm, SB_W), row), pl.BlockSpec((tm, NSA_W), row), pl.BlockSpec((tm, d), row),
                pl.BlockSpec((1, d), const),
                pl.BlockSpec(w_out.shape, const, pipeline_mode=pl.Buffered(1)),
                mod_spec(gate), pl.BlockSpec((1, d), const), mod_spec(shift), mod_spec(scale)]
    args = [osb, on, x, gout, w_out, gate, g2, shift, scale]
    aliases = {}
    if prev is not None:
        in_specs += [pl.BlockSpec(memory_space=pl.ANY)] * 3
        aliases = {len(args) + j: j for j in range(3)}
        args += list(prev)
    return pl.pallas_call(
        _mixout_kernel,
        grid=(m // tm,),
        in_specs=in_specs,
        out_specs=[pl.BlockSpec((tm, d), out_row), pl.BlockSpec((tm, d), out_row),
                   pl.BlockSpec((tm * ROW_WORDS, LANES), out_row)],
        out_shape=[jax.ShapeDtypeStruct((n_total, d), F32), jax.ShapeDtypeStruct((n_total, d), BF16),
                   jax.ShapeDtypeStruct((n_total * ROW_WORDS, LANES), U32)],
        scratch_shapes=[pltpu.VMEM((tm, d), BF16)],
        input_output_aliases=aliases,
        compiler_params=_cparams(("arbitrary",)),
        name="mixout",
    )(*args)


def _first_max(v, idx, axis, n):
    m = jnp.max(v, axis=axis, keepdims=True)
    return m, jnp.min(jnp.where(v == m, idx, float(n)), axis=axis, keepdims=True)


def _router_kernel(h_ref, wrt_ref, bias_ref, eidx_ref, w_ref, routed_ref):
    tm = h_ref.shape[0]
    per = N_EXPERTS // N_EXPERT_GROUPS
    s = _sigmoid(_dot_nt(wrt_ref[...], h_ref[...]))
    biased = s + bias_ref[...]
    b3 = biased.reshape(N_EXPERT_GROUPS, per, tm)
    r3 = lax.broadcasted_iota(jnp.int32, b3.shape, 1).astype(F32)
    m1, i1 = _first_max(b3, r3, 1, per)
    m2 = jnp.max(jnp.where(r3 == i1, -jnp.inf, b3), axis=1, keepdims=True)
    gscore = (m1 + m2).reshape(N_EXPERT_GROUPS, tm)
    gi = lax.broadcasted_iota(jnp.int32, gscore.shape, 0).astype(F32)
    gmask = jnp.zeros(gscore.shape, F32)
    for _ in range(TOPK_GROUPS):
        _, idx = _first_max(gscore, gi, 0, N_EXPERT_GROUPS)
        pick = gi == idx
        gmask = jnp.where(pick, 1.0, gmask)
        gscore = jnp.where(pick, -jnp.inf, gscore)
    cand = jnp.where(gmask.reshape(N_EXPERT_GROUPS, 1, tm) > 0.5, b3, -jnp.inf).reshape(N_EXPERTS, tm)
    ei = lax.broadcasted_iota(jnp.int32, cand.shape, 0).astype(F32)
    ids, ws = [], []
    routed = jnp.zeros(cand.shape, F32)
    for _ in range(TOP_K):
        _, idx = _first_max(cand, ei, 0, N_EXPERTS)
        pick = ei == idx
        ids.append(idx)
        ws.append(jnp.sum(jnp.where(pick, s, 0.0), axis=0, keepdims=True))
        cand = jnp.where(pick, -jnp.inf, cand)
        routed = jnp.where(pick, 1.0, routed)
    w = jnp.concatenate(ws, axis=0)
    eidx_ref[...] = jnp.concatenate(ids, axis=0).astype(jnp.int32)
    w_ref[...] = w / jnp.sum(w, axis=0, keepdims=True) * ROUTED_SCALE
    routed_ref[...] = routed.astype(BF16)


def _router(h2, w_router_t, router_bias, tm=512):
    m, d = h2.shape
    return pl.pallas_call(
        _router_kernel,
        grid=(m // tm,),
        in_specs=[pl.BlockSpec((tm, d), lambda i: (i, 0)),
                  pl.BlockSpec((N_EXPERTS, d), lambda i: (0, 0)),
                  pl.BlockSpec((N_EXPERTS, 1), lambda i: (0, 0))],
        out_specs=[pl.BlockSpec((TOP_K, tm), lambda i: (0, i)), pl.BlockSpec((TOP_K, tm), lambda i: (0, i)),
                   pl.BlockSpec((N_EXPERTS, tm), lambda i: (0, i))],
        out_shape=[jax.ShapeDtypeStruct((TOP_K, m), jnp.int32), jax.ShapeDtypeStruct((TOP_K, m), F32),
                   jax.ShapeDtypeStruct((N_EXPERTS, m), BF16)],
        compiler_params=_cparams(("arbitrary",)),
        name="router",
    )(h2, w_router_t, router_bias.reshape(N_EXPERTS, 1))


def _slots_kernel(eidx_ref, base_ref, o_ref):
    base = base_ref[...]
    ei = lax.broadcasted_iota(jnp.int32, base.shape, 0)
    rows = [jnp.sum(jnp.where(ei == eidx_ref[k:k + 1, :], base, 0.0), axis=0, keepdims=True) for k in range(TOP_K)]
    o_ref[...] = jnp.concatenate(rows, axis=0).astype(jnp.int32)


def _slots(eidx_t, base_t, tm=512):
    m = eidx_t.shape[1]
    return pl.pallas_call(
        _slots_kernel,
        grid=(m // tm,),
        in_specs=[pl.BlockSpec((TOP_K, tm), lambda i: (0, i)), pl.BlockSpec((N_EXPERTS, tm), lambda i: (0, i))],
        out_specs=pl.BlockSpec((TOP_K, tm), lambda i: (0, i)),
        out_shape=jax.ShapeDtypeStruct((TOP_K, m), jnp.int32),
        compiler_params=_cparams(("arbitrary",)),
        name="slots",
    )(eidx_t, base_t)


def _dispatch_plan(eidx_t, routed_t):
    t = eidx_t.shape[1]
    ck = 128
    r3 = routed_t.reshape(N_EXPERTS, t // ck, ck)
    upto = jnp.asarray(np.triu(np.ones((ck, ck), np.float32)), BF16)
    within = jnp.einsum('ecj,ji->eci', r3, upto, preferred_element_type=F32)
    tot = within[:, :, -1]
    incl = (within + (jnp.cumsum(tot, axis=1) - tot)[:, :, None]).reshape(N_EXPERTS, t)
    counts = incl[:, -1].astype(jnp.int32)
    padded = (counts + MOE_ROWS - 1) // MOE_ROWS * MOE_ROWS
    pad_end = jnp.cumsum(padded)
    base_t = (pad_end - padded).astype(F32)[:, None] + incl - routed_t.astype(F32)
    dest = _slots(eidx_t, base_t).T
    n_blk = -(-(t * TOP_K) // MOE_ROWS) + N_EXPERTS
    blk_start = (jnp.arange(n_blk) * MOE_ROWS)[:, None]
    blk_e = jnp.minimum(jnp.sum(pad_end[None, :] <= blk_start, axis=1), N_EXPERTS - 1).astype(jnp.int32)
    n_active = (pad_end[-1] // MOE_ROWS).astype(jnp.int32)
    nxt_blk = (pad_end[blk_e] // MOE_ROWS).astype(jnp.int32)
    nxt_e = jnp.where(nxt_blk < n_active, blk_e[jnp.minimum(nxt_blk, n_blk - 1)], -1).astype(jnp.int32)
    first = jnp.concatenate([jnp.ones((1,), jnp.int32), (blk_e[1:] != blk_e[:-1]).astype(jnp.int32)])
    par = ((jnp.cumsum(first) - 1) % 2).astype(jnp.int32)
    return dest, blk_e, nxt_e, par, n_active.reshape(1), n_blk


def _dispatch_kernel(dest_ref, h2g_ref, xs_ref, sem, *, tm):
    i = pl.program_id(0)

    def body(r, carry):
        src = h2g_ref.at[pl.ds(pl.multiple_of(r * ROW_WORDS, ROW_WORDS), ROW_WORDS), :]
        for k in range(TOP_K):
            d = dest_ref[(i * tm + r) * TOP_K + k]
            pltpu.make_async_copy(src, xs_ref.at[pl.ds(pl.multiple_of(d * ROW_WORDS, ROW_WORDS), ROW_WORDS), :],
                                  sem).start(priority=k % 2)
        return carry

    lax.fori_loop(0, tm, body, 0)
    done = xs_ref.at[pl.ds(0, tm * TOP_K * ROW_WORDS), :]
    pltpu.make_async_copy(done, done, sem).wait()


def _dispatch(dest_flat, h2g, p_rows, tm=256):
    n_tok = h2g.shape[0] // ROW_WORDS
    grid_spec = pltpu.PrefetchScalarGridSpec(
        num_scalar_prefetch=1,
        grid=(n_tok // tm,),
        in_specs=[pl.BlockSpec((tm * ROW_WORDS, LANES), lambda i, d: (i, 0))],
        out_specs=pl.BlockSpec(memory_space=pl.ANY),
        scratch_shapes=[pltpu.SemaphoreType.DMA(())],
    )
    return pl.pallas_call(
        functools.partial(_dispatch_kernel, tm=tm),
        grid_spec=grid_spec,
        out_shape=jax.ShapeDtypeStruct((p_rows * ROW_WORDS, LANES), U32),
        compiler_params=_cparams(("arbitrary",)),
        name="dispatch",
    )(dest_flat, h2g)


def _experts_kernel(be_ref, nxt_ref, par_ref, nact_ref, x_ref, wg_hbm, wu_hbm, wd_hbm, o_ref,
                    wgf, wuf, wdf, wgb, wub, wdb, sem):
    b = pl.program_id(0)
    active = b < nact_ref[0]
    e = be_ref[b]
    slot = par_ref[b]
    changed = e != be_ref[jnp.maximum(b - 1, 0)]

    def weight_copies(eid, s):
        return [pltpu.make_async_copy(src.at[eid], dst.at[s], sem.at[s, j])
                for j, (src, dst) in enumerate(((wg_hbm, wgf), (wu_hbm, wuf), (wd_hbm, wdf)))]

    @pl.when(active & (b == 0))
    def _():
        for cp in weight_copies(e, slot):
            cp.start()

    @pl.when(active & ((b == 0) | changed))
    def _():
        for cp in weight_copies(e, slot):
            cp.wait()
        nxt = nxt_ref[b]

        @pl.when(nxt >= 0)
        def _():
            for cp in weight_copies(nxt, 1 - slot):
                cp.start()

        ch = 256
        for c in range(0, D_MODEL, ch):
            wgb[c:c + ch, :] = wgf[slot, c:c + ch, :].astype(BF16)
            wub[c:c + ch, :] = wuf[slot, c:c + ch, :].astype(BF16)
        for c in range(0, D_EXPERT, ch // 4):
            wdb[c:c + ch // 4, :] = wdf[slot, c:c + ch // 4, :].astype(BF16)

    @pl.when(active)
    def _():
        chunks = []
        for s in range(ROW_WORDS):
            word = x_ref[pl.ds(s, MOE_ROWS, stride=ROW_WORDS), :]
            chunks.append(lax.bitcast_convert_type(word << 16, F32).astype(BF16))
            chunks.append(lax.bitcast_convert_type(word & jnp.uint32(0xFFFF0000), F32).astype(BF16))
        x = jnp.concatenate(chunks, axis=1)
        gt = _dot(x, wgb[...])
        a = (gt * _sigmoid(gt) * _dot(x, wub[...])).astype(BF16)
        y = _dot(a, wdb[...])
        for j in range(D_CHUNKS):
            o_ref[pl.ds(j, MOE_ROWS, stride=D_CHUNKS), :] = y[:, j * LANES:(j + 1) * LANES]


def _experts(blk_e, nxt_e, par, n_active, x_sorted, wg, wu, wd):
    d = wg.shape[1]
    n_blk = x_sorted.shape[0] // (MOE_ROWS * ROW_WORDS)
    hbm = pl.BlockSpec(memory_space=pl.ANY)
    grid_spec = pltpu.PrefetchScalarGridSpec(
        num_scalar_prefetch=4,
        grid=(n_blk,),
        in_specs=[pl.BlockSpec((MOE_ROWS * ROW_WORDS, LANES), lambda b, be, nx, pa, na: (jnp.minimum(b, na[0] - 1), 0)),
                  hbm, hbm, hbm],
        out_specs=pl.BlockSpec((MOE_ROWS * D_CHUNKS, LANES), lambda b, be, nx, pa, na: (jnp.minimum(b, na[0] - 1), 0)),
        scratch_shapes=[pltpu.VMEM((2, d, D_EXPERT), F32), pltpu.VMEM((2, d, D_EXPERT), F32),
                        pltpu.VMEM((2, D_EXPERT, d), F32),
                        pltpu.VMEM((d, D_EXPERT), BF16), pltpu.VMEM((d, D_EXPERT), BF16),
                        pltpu.VMEM((D_EXPERT, d), BF16),
                        pltpu.SemaphoreType.DMA((2, 3))],
    )
    return pl.pallas_call(
        _experts_kernel,
        grid_spec=grid_spec,
        out_shape=jax.ShapeDtypeStruct((n_blk * MOE_ROWS * D_CHUNKS, LANES), F32),
        compiler_params=_cparams(("arbitrary",)),
        name="experts",
    )(blk_e, nxt_e, par, n_active, x_sorted, wg, wu, wd)


def _final_kernel(dest_ref, x1_ref, h2_ref, w8_ref, ys_ref, wg_ref, wu_ref, wd_ref, gate_ref, gf_ref,
                  shift_ref, scale_ref, y_ref, ybuf, ymoe_ref, sem, *, tm, row0):
    i = pl.program_id(0)
    n = pl.num_programs(0)
    tile_rows = tm * TOP_K * D_CHUNKS

    def issue(tile, slot):
        def body(r, carry):
            for k in range(TOP_K):
                d = dest_ref[(row0 + tile * tm + r) * TOP_K + k]
                pltpu.make_async_copy(
                    ys_ref.at[pl.ds(pl.multiple_of(d * D_CHUNKS, D_CHUNKS), D_CHUNKS), :],
                    ybuf.at[slot, pl.ds(pl.multiple_of((k * tm + r) * D_CHUNKS, D_CHUNKS), D_CHUNKS), :],
                    sem.at[slot]).start(priority=k % 2)
            return carry
        lax.fori_loop(0, tm, body, 0)

    @pl.when(i == 0)
    def _():
        issue(0, 0)

    @pl.when(i + 1 < n)
    def _():
        issue(i + 1, (i + 1) % 2)

    slot = i % 2
    pltpu.make_async_copy(ys_ref.at[pl.ds(0, tile_rows), :], ybuf.at[slot], sem.at[slot]).wait()
    w8 = w8_ref[...]
    wk = [w8[:, k:k + 1] for k in range(TOP_K)]
    for j in range(D_CHUNKS):
        acc = None
        for k in range(TOP_K):
            v = ybuf[slot, pl.ds(k * tm * D_CHUNKS + j, tm, stride=D_CHUNKS), :] * wk[k]
            acc = v if acc is None else acc + v
        ymoe_ref[:, j * LANES:(j + 1) * LANES] = acc

    h = h2_ref[...]
    gt = _dot(h, wg_ref[...])
    a = (gt * _sigmoid(gt) * _dot(h, wu_ref[...])).astype(BF16)
    f = ymoe_ref[...] + _dot(a, wd_ref[...])
    x2 = x1_ref[...] + gate_ref[...] * f
    y_ref[...] = _rms(x2) * gf_ref[...] * (1.0 + scale_ref[...]) + shift_ref[...]


def _final(dest_flat, x1, h2, w8, y_sorted, wg, wu, wd, gate, gf, shift, scale, m, row0, tm=64):
    d = x1.shape[1]
    blk0 = row0 // tm
    in_row = lambda i, dr: (blk0 + i, 0)
    row = lambda i, dr: (i, 0)
    const = lambda i, dr: (0, 0)
    mod_spec = lambda a: pl.BlockSpec((tm, d), row) if a.shape[0] == m else pl.BlockSpec((1, d), const)
    grid_spec = pltpu.PrefetchScalarGridSpec(
        num_scalar_prefetch=1,
        grid=(m // tm,),
        in_specs=[pl.BlockSpec((tm, d), in_row), pl.BlockSpec((tm, d), in_row), pl.BlockSpec((tm, TOP_K), in_row),
                  pl.BlockSpec(memory_space=pl.ANY),
                  pl.BlockSpec(wg.shape, const), pl.BlockSpec(wu.shape, const), pl.BlockSpec(wd.shape, const),
                  mod_spec(gate), pl.BlockSpec((1, d), const), mod_spec(shift), mod_spec(scale)],
        out_specs=pl.BlockSpec((tm, d), row),
        scratch_shapes=[pltpu.VMEM((2, tm * TOP_K * D_CHUNKS, LANES), F32), pltpu.VMEM((tm, d), F32),
                        pltpu.SemaphoreType.DMA((2,))],
    )
    return pl.pallas_call(
        functools.partial(_final_kernel, tm=tm, row0=row0),
        grid_spec=grid_spec,
        out_shape=jax.ShapeDtypeStruct((m, d), F32),
        compiler_params=_cparams(("arbitrary",)),
        name="final",
    )(dest_flat, x1, h2, w8, y_sorted, wg, wu, wd, gate, gf, shift, scale)


def kernel(x_prompt, x_sample, cache_sb, cache_nsa, state_win, page_table, c_prompt, c_sample, w_ada, b_ada, norm1_g, w_in, cmp_pos_k, cmp_pos_v, cmp_wk, cmp_wv, out_norm_g, w_out, norm2_g, w_router, router_bias, w_gate_e, w_up_e, w_down_e, w_gate_s, w_up_s, w_down_s, normf_g):
    b_p, t, d = x_prompt.shape
    bs, n_new, _ = x_sample.shape
    assert b_p == 1 and d == D_MODEL
    n_s = bs * n_new
    n_tok = t + n_s

    c_all = jnp.concatenate([c_prompt, c_sample], axis=0)
    m_pad = -(-c_all.shape[0] // 8) * 8
    c_all = jnp.pad(c_all, ((0, m_pad - c_all.shape[0]), (0, 0)))
    mod = _ada(c_all, w_ada, b_ada).reshape(m_pad, N_ADA, d)
    mod_p = [mod[0:1, i] for i in range(N_ADA)]
    mod_s = [jnp.repeat(mod[1:1 + bs, i], n_new, axis=0) for i in range(N_ADA)]

    w_main = w_in[:, :COL_GATE]
    w_gate = jnp.pad(w_in[:, COL_GATE:], ((0, 0), (0, LANES - N_GATE)))
    w_cat = jnp.concatenate([w_main, w_gate], axis=1).astype(BF16)
    row1 = lambda v: v.reshape(1, -1)

    xp = x_prompt.reshape(t, d)
    xs = x_sample.reshape(n_s, d)
    (qsb_p, sbrows_p, kvsb_p, qn_p, nsarows_p, nsab_p, winrows_p, winb_p, gates_p) = _project(
        xp, row1(norm1_g), mod_p[0], mod_p[1], w_cat)
    (qsb_s, sbrows_s, _, qn_s, nsarows_s, _, winrows_s, _, gates_s) = _project(
        xs, row1(norm1_g), mod_s[0], mod_s[1], w_cat)

    osb_p = _sb_prompt(qsb_p, kvsb_p)
    wpos = jnp.stack([jnp.repeat(cmp_pos_k, HEAD_DIM, axis=1), jnp.repeat(cmp_pos_v, HEAD_DIM, axis=1)])
    wc = jnp.stack([cmp_wk, cmp_wv]).astype(BF16)
    kvc_p = _compress_prompt(nsarows_p, wpos, wc)
    on_p = _nsa_prompt_t(qn_p, gates_p, kvc_p, nsab_p, winb_p)

    per_seq = lambda a: a.reshape(bs, n_new, a.shape[-1])
    n_phys = cache_sb.shape[0]
    w_buf = state_win.shape[1]
    osb_s = _sb_sample(page_table, per_seq(qsb_s), per_seq(sbrows_s), cache_sb.reshape(n_phys, -1, HEAD_DIM))
    on_s, win_s = _nsa_sample(page_table, per_seq(qn_s), per_seq(gates_s), per_seq(nsarows_s), per_seq(winrows_s),
                              cache_nsa.reshape(n_phys, -1, HEAD_DIM), state_win.reshape(bs, -1, HEAD_DIM), wpos, wc)

    w_out_b = w_out.astype(BF16)
    gout = out_norm_g.reshape(1, -1)
    bufs = _mixout(osb_p, on_p, xp, gout, w_out_b, mod_p[2], row1(norm2_g), mod_p[3], mod_p[4], n_tok, 0)
    x1, h2, h2g = _mixout(osb_s.reshape(n_s, SB_W), on_s.reshape(n_s, NSA_W), xs, gout, w_out_b,
                          mod_s[2], row1(norm2_g), mod_s[3], mod_s[4], n_tok, t, prev=bufs)

    eidx_t, w_t, routed_t = _router(h2, w_router.T.astype(BF16), router_bias)
    dest, blk_e, nxt_e, par, n_active, n_blk = _dispatch_plan(eidx_t, routed_t)
    dest_flat = dest.reshape(-1)
    x_sorted = _dispatch(dest_flat, h2g, n_blk * MOE_ROWS)
    y_sorted = _experts(blk_e, nxt_e, par, n_active, x_sorted, w_gate_e, w_up_e, w_down_e)
    w8 = w_t.T
    ws = (w_gate_s.astype(BF16), w_up_s.astype(BF16), w_down_s.astype(BF16))
    y_p = _final(dest_flat, x1, h2, w8, y_sorted, *ws, mod_p[5], row1(normf_g), mod_p[6], mod_p[7], t, 0)
    y_s = _final(dest_flat, x1, h2, w8, y_sorted, *ws, mod_s[5], row1(normf_g), mod_s[6], mod_s[7], n_s, t)

    keep = min(WINDOW, t)
    return (y_p.reshape(1, t, d), y_s.reshape(bs, n_new, d),
            sbrows_p.reshape(1, t, 2, N_SB_HEADS, HEAD_DIM), sbrows_s.reshape(bs, n_new, 2, N_SB_HEADS, HEAD_DIM),
            nsarows_p.reshape(1, t, 4, N_NSA_KV, HEAD_DIM), nsarows_s.reshape(bs, n_new, 4, N_NSA_KV, HEAD_DIM),
            winrows_p[t - keep:].reshape(1, keep, 2, N_NSA_KV, HEAD_DIM),
            win_s.reshape(bs, w_buf, 2, N_NSA_KV, HEAD_DIM))
```
